```python
import jax, jax.numpy as jnp
from jax import lax
import numpy as np

D_MODEL = 4096
BATCH = 4
SEQ = 4096
DEPTH = 2

BLOCK_Q = 128
N_BRANCH = 4
BRANCH_WIDTH = 1024
A_HEADS = 8
A_KV_HEADS = 2
A_DIM = 128
IDX_HEADS = 16
IDX_DIM = 64
IDX_TOPK_MAX = 256
B_HEADS = 8
B_DIM = 128
C_HEADS = 16
C_KV_HEADS = 2
C_DIM = 64
WINDOW = 128
D_HEADS = 8
D_Q_LORA = 768
D_KV_LORA = 256
D_NOPE = 128
D_ROPE = 64
D_VDIM = 128
ROPE_THETA = 10000.0
RMS_EPS = 1e-6
LN_EPS = 1e-5
DEEPNORM_ALPHA = (2 * DEPTH) ** 0.25
DEEPNORM_BETA = (8 * DEPTH) ** -0.25
MAX_START_POS = 1024

IN_SEGMENTS = (
    ('a_q', A_HEADS * A_DIM), ('a_k', A_KV_HEADS * A_DIM), ('a_v', A_KV_HEADS * A_DIM),
    ('a_iq', IDX_HEADS * IDX_DIM), ('a_ik', IDX_DIM), ('a_iw', IDX_HEADS), ('a_z', BRANCH_WIDTH),
    ('b_q', B_HEADS * B_DIM), ('b_k', B_HEADS * B_DIM), ('b_v', B_HEADS * B_DIM), ('b_f', B_HEADS), ('b_z', BRANCH_WIDTH),
    ('c_q', C_HEADS * C_DIM), ('c_k', C_KV_HEADS * C_DIM), ('c_v', C_KV_HEADS * C_DIM), ('c_z', BRANCH_WIDTH),
    ('d_cq', D_Q_LORA), ('d_ckv', D_KV_LORA), ('d_kr', D_ROPE), ('d_z', BRANCH_WIDTH),
)
IN_WIDTH = sum(n for _, n in IN_SEGMENTS)

kernel_name = 'hybrid_gated_four_mixer_deepnorm'


def split_columns(h):
    sizes = [n for _, n in IN_SEGMENTS]
    offsets = [int(o) for o in np.cumsum(sizes)[:-1]]
    return dict(zip([name for name, _ in IN_SEGMENTS], jnp.split(h, offsets, axis=-1)))


def alibi_slopes(n_heads):
    return jnp.asarray([2.0 ** (-8.0 * (h + 1) / n_heads) for h in range(n_heads)], dtype=jnp.float32)


def to_blocks(t, nb):
    return jnp.moveaxis(t.reshape(t.shape[0], nb, BLOCK_Q, *t.shape[2:]), 1, 0)


def from_blocks(t):
    t = jnp.moveaxis(t, 0, 1)
    return t.reshape(t.shape[0], t.shape[1] * t.shape[2], -1)


def rms_norm(x, g):
    xf = x.astype(jnp.float32)
    return (xf * lax.rsqrt(jnp.mean(jnp.square(xf), -1, keepdims=True) + RMS_EPS) * g).astype(x.dtype)


def layer_norm(x, g, b):
    xf = x.astype(jnp.float32)
    mu = jnp.mean(xf, -1, keepdims=True)
    var = jnp.mean(jnp.square(xf - mu), -1, keepdims=True)
    return ((xf - mu) * lax.rsqrt(var + LN_EPS) * g + b).astype(x.dtype)


def rope(t, positions):
    half = t.shape[-1] // 2
    inv_freq = ROPE_THETA ** (-jnp.arange(half, dtype=jnp.float32) / half)
    ang = positions.astype(jnp.float32)[..., None] * inv_freq
    cos = jnp.cos(ang)[:, :, None, :]
    sin = jnp.sin(ang)[:, :, None, :]
    t1 = t[..., :half].astype(jnp.float32)
    t2 = t[..., half:].astype(jnp.float32)
    return jnp.concatenate([t1 * cos - t2 * sin, t1 * sin + t2 * cos], -1).astype(t.dtype)


def indexed_sparse_attention(q, k, v, iq, ik, iw):
    B, S = q.shape[:2]
    nb = S // BLOCK_Q
    topk = min(IDX_TOPK_MAX, S // 4)
    slopes = alibi_slopes(A_HEADS).reshape(A_KV_HEADS, A_HEADS // A_KV_HEADS)
    scale = A_DIM ** -0.5
    w = iw.astype(jnp.float32) * (IDX_HEADS ** -0.5 * IDX_DIM ** -0.5)
    kpos = jnp.arange(S)
    gather = jax.vmap(lambda arr, idx: arr[idx])

    def block(args):
        i, q_i, iq_i, w_i = args
        qpos = i * BLOCK_Q + jnp.arange(BLOCK_Q)
        causal = kpos[None, :] <= qpos[:, None]
        rel = jax.nn.relu(jnp.einsum('bqhd,bsd->bqhs', iq_i, ik).astype(jnp.float32))
        score = jnp.einsum('bqhs,bqh->bqs', rel, w_i)
        score = jnp.where(causal[None], score, -jnp.inf)
        top_val, top_idx = lax.top_k(score, topk)
        valid = jnp.isfinite(top_val)
        k_sel = gather(k, top_idx)
        v_sel = gather(v, top_idx)
        logits = jnp.einsum('bqgrd,bqkgd->bgrqk', q_i, k_sel).astype(jnp.float32) * scale
        dist = (qpos[None, :, None] - top_idx).astype(jnp.float32)
        logits = logits - slopes[None, :, :, None, None] * dist[:, None, None]
        logits = jnp.where(valid[:, None, None], logits, -jnp.inf)
        p = jax.nn.softmax(logits, axis=-1)
        return jnp.einsum('bgrqk,bqkgd->bqgrd', p.astype(v.dtype), v_sel)

    out = lax.map(block, (jnp.arange(nb), to_blocks(q, nb), to_blocks(iq, nb), to_blocks(w, nb)))
    return from_blocks(out)


def blocked_causal_attention(q, k, v, scale, cum_log_forget=None):
    B, S = q.shape[:2]
    nb = S // BLOCK_Q
    kpos = jnp.arange(S)
    xs = (jnp.arange(nb), to_blocks(q, nb))
    if cum_log_forget is not None:
        c_keys = jnp.transpose(cum_log_forget, (0, 2, 1))
        xs = xs + (to_blocks(cum_log_forget, nb),)

    def block(args):
        i, q_i = args[0], args[1]
        qpos = i * BLOCK_Q + jnp.arange(BLOCK_Q)
        logits = jnp.einsum('bqhd,bkhd->bhqk', q_i, k).astype(jnp.float32) * scale
        if cum_log_forget is not None:
            c_q = jnp.transpose(args[2], (0, 2, 1))
            logits = logits + (c_q[..., None] - c_keys[:, :, None, :])
        causal = kpos[None, :] <= qpos[:, None]
        logits = jnp.where(causal[None, None], logits, -jnp.inf)
        p = jax.nn.softmax(logits, axis=-1)
        return jnp.einsum('bhqk,bkhd->bqhd', p.astype(v.dtype), v)

    return from_blocks(lax.map(block, xs))


def sliding_window_sink_attention(q, k, v, sinks):
    B, S = q.shape[:2]
    nb = S // WINDOW
    G, R = C_KV_HEADS, C_HEADS // C_KV_HEADS
    slopes = alibi_slopes(C_HEADS).reshape(G, R)
    qb = q.reshape(B, nb, WINDOW, G, R, C_DIM)
    kb = k.reshape(B, nb, WINDOW, G, C_DIM)
    vb = v.reshape(B, nb, WINDOW, G, C_DIM)
    shift = lambda t: jnp.pad(t, ((0, 0), (1, 0), (0, 0), (0, 0), (0, 0)))[:, :-1]
    kcat = jnp.concatenate([shift(kb), kb], axis=2)
    vcat = jnp.concatenate([shift(vb), vb], axis=2)
    logits = jnp.einsum('bnqgrd,bnkgd->bngrqk', qb, kcat).astype(jnp.float32) * (C_DIM ** -0.5)
    qi = jnp.arange(WINDOW)[:, None]
    kj = jnp.arange(2 * WINDOW)[None, :]
    dist = qi + WINDOW - kj
    band = (dist >= 0) & (dist < WINDOW)
    has_prev = (jnp.arange(nb) > 0)[:, None, None] | (kj >= WINDOW)[None]
    mask = band[None] & has_prev
    logits = logits - slopes[None, None, :, :, None, None] * dist.astype(jnp.float32)
    logits = jnp.where(mask[None, :, None, None], logits, -jnp.inf)
    sink = sinks.astype(jnp.float32).reshape(G, R)[None, None, :, :, None]
    lse = jnp.logaddexp(jax.nn.logsumexp(logits, axis=-1), sink)
    p = jnp.exp(logits - lse[..., None])
    out = jnp.einsum('bngrqk,bnkgd->bnqgrd', p.astype(v.dtype), vcat)
    return out.reshape(B, S, C_HEADS * C_DIM)


def latent_attention(c_q, c_kv, k_rope_in, positions, q_gain, q_up, kv_gain, kv_up):
    B, S = c_q.shape[:2]
    q = jnp.einsum('bsc,ce->bse', rms_norm(c_q, q_gain), q_up).reshape(B, S, D_HEADS, D_NOPE + D_ROPE)
    q = jnp.concatenate([q[..., :D_NOPE], rope(q[..., D_NOPE:], positions)], -1)
    kv = jnp.einsum('bsc,ce->bse', rms_norm(c_kv, kv_gain), kv_up).reshape(B, S, D_HEADS, D_NOPE + D_VDIM)
    k_rope = rope(k_rope_in[:, :, None, :], positions)
    k = jnp.concatenate([kv[..., :D_NOPE], jnp.broadcast_to(k_rope, (B, S, D_HEADS, D_ROPE))], -1)
    v = kv[..., D_NOPE:]
    return blocked_causal_attention(q, k, v, (D_NOPE + D_ROPE) ** -0.5)


def hybrid_layer(x, positions, w_in, w_gate, w_branch, w_out, dq_gain, dq_up, dkv_gain, dkv_up, f_bias, sinks, ln_gain, ln_bias):
    B, S, _ = x.shape
    p = split_columns(jnp.einsum('bsd,de->bse', x, w_in))
    o_a = indexed_sparse_attention(
        p['a_q'].reshape(B, S, A_KV_HEADS, A_HEADS // A_KV_HEADS, A_DIM),
        p['a_k'].reshape(B, S, A_KV_HEADS, A_DIM), p['a_v'].reshape(B, S, A_KV_HEADS, A_DIM),
        p['a_iq'].reshape(B, S, IDX_HEADS, IDX_DIM), p['a_ik'], p['a_iw'])
    log_f = jax.nn.log_sigmoid(p['b_f'].astype(jnp.float32) + f_bias.astype(jnp.float32))
    o_b = blocked_causal_attention(
        p['b_q'].reshape(B, S, B_HEADS, B_DIM), p['b_k'].reshape(B, S, B_HEADS, B_DIM),
        p['b_v'].reshape(B, S, B_HEADS, B_DIM), B_DIM ** -0.5, cum_log_forget=jnp.cumsum(log_f, axis=1))
    o_c = sliding_window_sink_attention(
        p['c_q'].reshape(B, S, C_KV_HEADS, C_HEADS // C_KV_HEADS, C_DIM),
        p['c_k'].reshape(B, S, C_KV_HEADS, C_DIM), p['c_v'].reshape(B, S, C_KV_HEADS, C_DIM), sinks)
    o_d = latent_attention(p['d_cq'], p['d_ckv'], p['d_kr'], positions, dq_gain, dq_up, dkv_gain, dkv_up)
    branches = (o_a * jax.nn.silu(p['a_z']), o_b * jax.nn.silu(p['b_z']),
                o_c * jax.nn.silu(p['c_z']), o_d * jax.nn.silu(p['d_z']))
    merged = jnp.zeros_like(x)
    for n in range(N_BRANCH):
        gate = jax.nn.sigmoid(jnp.einsum('bsd,de->bse', x, w_gate[n]))
        merged = merged + gate * jnp.einsum('bsc,cd->bsd', branches[n], w_branch[n])
    y = jnp.einsum('bsd,de->bse', merged, w_out)
    return layer_norm(DEEPNORM_ALPHA * x + y, ln_gain, ln_bias)


def setup_inputs(seed: int = 0) -> dict:
    key = jax.random.key(seed)
    ks = jax.random.split(key, 16)
    nrm = lambda k, shape, s: jax.random.normal(k, shape, jnp.float32) * s
    x = nrm(ks[0], (BATCH, SEQ, D_MODEL), 1.0)
    start = jax.random.randint(ks[1], (BATCH, 1), 0, MAX_START_POS, dtype=jnp.int32)
    positions = start + jnp.arange(SEQ, dtype=jnp.int32)[None, :]
    w_in = nrm(ks[2], (DEPTH, D_MODEL, IN_WIDTH), D_MODEL ** -0.5)
    w_gate = nrm(ks[3], (DEPTH, N_BRANCH, D_MODEL, D_MODEL), D_MODEL ** -0.5)
    w_branch = nrm(ks[4], (DEPTH, N_BRANCH, BRANCH_WIDTH, D_MODEL), BRANCH_WIDTH ** -0.5 * DEEPNORM_BETA)
    w_out = nrm(ks[5], (DEPTH, D_MODEL, D_MODEL), D_MODEL ** -0.5 * DEEPNORM_BETA)
    dq_gain = 1.0 + nrm(ks[6], (DEPTH, D_Q_LORA), 0.02)
    dq_up = nrm(ks[7], (DEPTH, D_Q_LORA, D_HEADS * (D_NOPE + D_ROPE)), D_Q_LORA ** -0.5)
    dkv_gain = 1.0 + nrm(ks[8], (DEPTH, D_KV_LORA), 0.02)
    dkv_up = nrm(ks[9], (DEPTH, D_KV_LORA, D_HEADS * (D_NOPE + D_VDIM)), D_KV_LORA ** -0.5)
    f_bias = jax.random.uniform(ks[10], (DEPTH, B_HEADS), jnp.float32, 1.0, 5.0)
    sinks = nrm(ks[11], (DEPTH, C_HEADS), 0.5)
    ln_gain = 1.0 + nrm(ks[12], (DEPTH, D_MODEL), 0.02)
    ln_bias = nrm(ks[13], (DEPTH, D_MODEL), 0.02)
    return {'x': x, 'positions': positions, 'w_in': w_in, 'w_gate': w_gate, 'w_branch': w_branch,
            'w_out': w_out, 'dq_gain': dq_gain, 'dq_up': dq_up, 'dkv_gain': dkv_gain, 'dkv_up': dkv_up,
            'f_bias': f_bias, 'sinks': sinks, 'ln_gain': ln_gain, 'ln_bias': ln_bias}


def reference(x, positions, w_in, w_gate, w_branch, w_out, dq_gain, dq_up, dkv_gain, dkv_up, f_bias, sinks, ln_gain, ln_bias):
    for l in range(DEPTH):
        x = hybrid_layer(x, positions, w_in[l], w_gate[l], w_branch[l], w_out[l], dq_gain[l], dq_up[l],
                         dkv_gain[l], dkv_up[l], f_bias[l], sinks[l], ln_gain[l], ln_bias[l])
    return x
```

```python
import functools

import numpy as np
import jax
import jax.numpy as jnp
from jax import lax
from jax.experimental import pallas as pl
from jax.experimental.pallas import tpu as pltpu

BRANCH_WIDTH = 1024
A_HEADS, A_KV_HEADS, A_DIM = 8, 2, 128
IDX_HEADS, IDX_DIM, IDX_TOPK_MAX = 16, 64, 256
B_HEADS, B_DIM = 8, 128
C_HEADS, C_KV_HEADS, C_DIM, WINDOW = 16, 2, 64, 128
D_HEADS, D_Q_LORA, D_KV_LORA, D_NOPE, D_ROPE, D_VDIM = 8, 768, 256, 128, 64, 128
ROPE_THETA = 10000.0
RMS_EPS = 1e-6
LN_EPS = 1e-5

IN_SEGMENTS = (
    ('a_q', 1024), ('a_k', 256), ('a_v', 256), ('a_iq', 1024), ('a_ik', 64), ('a_iw', 16), ('a_z', 1024),
    ('b_q', 1024), ('b_k', 1024), ('b_v', 1024), ('b_f', 8), ('b_z', 1024),
    ('c_q', 1024), ('c_k', 128), ('c_v', 128), ('c_z', 1024),
    ('d_cq', 768), ('d_ckv', 256), ('d_kr', 64), ('d_z', 1024),
)

P_LAYOUT = (
    ('d_cq', 768), ('d_ckv', 256), ('a_q', 1024), ('a_iq', 1024), ('a_z', 1024),
    ('b_q', 1024), ('b_k', 1024), ('b_v', 1024), ('b_z', 1024), ('c_q', 1024), ('c_z', 1024),
    ('d_z', 1024), ('a_k', 256), ('a_v', 256), ('a_ik', 128), ('d_kr', 128), ('c_k', 128), ('c_v', 128),
)
P_OFF = {}
_o = 0
for _n, _w in P_LAYOUT:
    assert _o % _w == 0
    P_OFF[_n] = _o
    _o += _w
P_WIDTH = _o

LANES = 128
NEG = -1e30
INT_MIN = -2 ** 31
VMEM_LIMIT = 56 * 1024 * 1024

F32 = jnp.float32
BF16 = jnp.bfloat16


def _params(sem, vmem=VMEM_LIMIT):
    return pltpu.CompilerParams(dimension_semantics=sem, vmem_limit_bytes=vmem)


def _dot(a, b):
    return jnp.dot(a, b, preferred_element_type=F32)


def _dot_nt(a, b):
    return lax.dot_general(a, b, (((1,), (1,)), ((), ())), preferred_element_type=F32)


def _alibi_slopes(n_heads):
    return [float(np.float32(2.0 ** (-8.0 * (h + 1) / n_heads))) for h in range(n_heads)]


def _mm_kernel(x_ref, w_ref, o_ref):
    o_ref[...] = _dot(x_ref[...], w_ref[...]).astype(o_ref.dtype)


def _matmul(x, w, out_dtype, tm=1024, tn=1024):
    m, k = x.shape
    n = w.shape[1]
    tm, tn = min(tm, m), min(tn, n)
    return pl.pallas_call(
        _mm_kernel,
        grid=(m // tm, n // tn),
        in_specs=[pl.BlockSpec((tm, k), lambda i, j: (i, 0)),
                  pl.BlockSpec((k, tn), lambda i, j: (0, j))],
        out_specs=pl.BlockSpec((tm, tn), lambda i, j: (i, j)),
        out_shape=jax.ShapeDtypeStruct((m, n), out_dtype),
        compiler_params=_params(("parallel", "parallel")),
        name="dense_matmul",
    )(x, w)


def _merge_kernel(x_ref, wg_ref, b0_ref, b1_ref, b2_ref, b3_ref, wb_ref, o_ref, acc_ref):
    n = pl.program_id(2)
    gate = jax.nn.sigmoid(_dot(x_ref[...], wg_ref[0]))

    @pl.when(n == 0)
    def _():
        acc_ref[...] = gate * _dot(b0_ref[...], wb_ref[0])

    for idx, b_ref in ((1, b1_ref), (2, b2_ref), (3, b3_ref)):
        @pl.when(n == idx)
        def _(b_ref=b_ref):
            acc_ref[...] += gate * _dot(b_ref[...], wb_ref[0])

    @pl.when(n == 3)
    def _():
        o_ref[...] = acc_ref[...].astype(o_ref.dtype)


def _gated_merge(xb, wg, branches, wb, tm=1024, tn=512):
    t, d = xb.shape
    tm, tn = min(tm, t), min(tn, d)
    bw = branches[0].shape[1]
    bspec = pl.BlockSpec((tm, bw), lambda i, j, n: (i, 0))
    return pl.pallas_call(
        _merge_kernel,
        grid=(t // tm, d // tn, 4),
        in_specs=[pl.BlockSpec((tm, d), lambda i, j, n: (i, 0)),
                  pl.BlockSpec((1, d, tn), lambda i, j, n: (n, 0, j)),
                  bspec, bspec, bspec, bspec,
                  pl.BlockSpec((1, bw, tn), lambda i, j, n: (n, 0, j))],
        out_specs=pl.BlockSpec((tm, tn), lambda i, j, n: (i, j)),
        out_shape=jax.ShapeDtypeStruct((t, d), BF16),
        scratch_shapes=[pltpu.VMEM((tm, tn), F32)],
        compiler_params=_params(("parallel", "parallel", "arbitrary")),
        name="gated_merge",
    )(xb, wg, *branches, wb)


def _ln_kernel(x_ref, y_ref, g_ref, b_ref, o_ref, ob_ref, *, alpha):
    r = alpha * x_ref[...] + y_ref[...]
    mu = jnp.mean(r, axis=-1, keepdims=True)
    c = r - mu
    var = jnp.mean(c * c, axis=-1, keepdims=True)
    out = c * lax.rsqrt(var + LN_EPS) * g_ref[...] + b_ref[...]
    o_ref[...] = out
    ob_ref[...] = out.astype(BF16)


def _residual_layernorm(x, y, gain, bias, alpha, tm=256):
    t, d = x.shape
    tm = min(tm, t)
    row = pl.BlockSpec((tm, d), lambda i: (i, 0))
    vec = pl.BlockSpec((1, d), lambda i: (0, 0))
    return pl.pallas_call(
        functools.partial(_ln_kernel, alpha=alpha),
        grid=(t // tm,),
        in_specs=[row, row, vec, vec],
        out_specs=[row, row],
        out_shape=[jax.ShapeDtypeStruct((t, d), F32), jax.ShapeDtypeStruct((t, d), BF16)],
        compiler_params=_params(("parallel",)),
        name="residual_layernorm",
    )(x, y, gain.reshape(1, d), bias.reshape(1, d))


def _forget_cumsum_kernel(f_ref, bias_ref, c_ref):
    rows, s = f_ref.shape
    lane = lax.broadcasted_iota(jnp.int32, (rows, LANES), 1)
    carry = jnp.zeros((rows, 1), F32)
    for c in range(s // LANES):
        x = jax.nn.log_sigmoid(f_ref[:, c * LANES:(c + 1) * LANES] + bias_ref[...])
        shift = 1
        while shift < LANES:
            x = x + jnp.where(lane >= shift, pltpu.roll(x, shift, 1), 0.0)
            shift *= 2
        x = x + carry
        c_ref[:, c * LANES:(c + 1) * LANES] = x
        carry = x[:, LANES - 1:LANES]


def _forget_cumsum(f_rows, bias_rows):
    rows, s = f_rows.shape
    full = pl.BlockSpec((rows, s), lambda: (0, 0))
    return pl.pallas_call(
        _forget_cumsum_kernel,
        in_specs=[full, pl.BlockSpec((rows, 1), lambda: (0, 0))],
        out_specs=full,
        out_shape=jax.ShapeDtypeStruct((rows, s), F32),
        name="forget_cumsum",
    )(f_rows, bias_rows)


def _flash_kernel(*refs, scale, fox, tq, tk):
    if fox:
        q_ref, k_ref, v_ref, cq_ref, ck_ref, z_ref, o_ref, m_sc, l_sc, acc_sc = refs
    else:
        q_ref, k_ref, v_ref, z_ref, o_ref, m_sc, l_sc, acc_sc = refs
    i = pl.program_id(2)
    j = pl.program_id(3)

    @pl.when(j == 0)
    def _():
        m_sc[...] = jnp.full(m_sc.shape, NEG, F32)
        l_sc[...] = jnp.zeros(l_sc.shape, F32)
        acc_sc[...] = jnp.zeros(acc_sc.shape, F32)

    @pl.when(j <= i)
    def _():
        s = _dot_nt(q_ref[...], k_ref[...]) * scale
        if fox:
            s = s + (cq_ref[0, 0] - ck_ref[0, 0])
        row = i * tq + lax.broadcasted_iota(jnp.int32, (tq, tk), 0)
        col = j * tk + lax.broadcasted_iota(jnp.int32, (tq, tk), 1)
        s = jnp.where(row >= col, s, NEG)
        m_prev = m_sc[:, 0:1]
        m_new = jnp.maximum(m_prev, jnp.max(s, axis=1, keepdims=True))
        alpha = jnp.exp(m_prev - m_new)
        p = jnp.exp(s - m_new)
        l_sc[...] = jnp.broadcast_to(alpha * l_sc[:, 0:1] + jnp.sum(p, axis=1, keepdims=True), l_sc.shape)
        acc_sc[...] = alpha * acc_sc[...] + _dot(p.astype(BF16), v_ref[...])
        m_sc[...] = jnp.broadcast_to(m_new, m_sc.shape)

    @pl.when(j == i)
    def _():
        z = z_ref[...].astype(F32)
        o_ref[...] = (acc_sc[...] / l_sc[:, 0:1] * jax.nn.silu(z)).astype(o_ref.dtype)


def _flash_attention(q_arr, q_blk0, k_arr, k_blk0, v_arr, v_blk0, z_arr, z_blk0, *, batch, seq, heads,
                     dqk, dv, scale, cum=None, tile=512):
    tq = tk = min(tile, seq)
    nq = seq // tq
    fox = cum is not None
    in_specs = [
        pl.BlockSpec((tq, dqk), lambda b, h, i, j: (b * nq + i, q_blk0 + h)),
        pl.BlockSpec((tk, dqk), lambda b, h, i, j: (b * nq + jnp.minimum(j, i), k_blk0 + h)),
        pl.BlockSpec((tk, dv), lambda b, h, i, j: (b * nq + jnp.minimum(j, i), v_blk0 + h)),
    ]
    args = [q_arr, k_arr, v_arr]
    if fox:
        in_specs += [pl.BlockSpec((1, 1, tq, 1), lambda b, h, i, j: (b, h, i, 0)),
                     pl.BlockSpec((1, 1, 1, tk), lambda b, h, i, j: (b, h, 0, jnp.minimum(j, i)))]
        args += [cum.reshape(batch, heads, seq, 1), cum.reshape(batch, heads, 1, seq)]
    in_specs.append(pl.BlockSpec((tq, dv), lambda b, h, i, j: (b * nq + i, z_blk0 + h)))
    args.append(z_arr)
    return pl.pallas_call(
        functools.partial(_flash_kernel, scale=scale, fox=fox, tq=tq, tk=tk),
        grid=(batch, heads, nq, nq),
        in_specs=in_specs,
        out_specs=pl.BlockSpec((tq, dv), lambda b, h, i, j: (b * nq + i, h)),
        out_shape=jax.ShapeDtypeStruct((batch * seq, heads * dv), BF16),
        scratch_shapes=[pltpu.VMEM((tq, LANES), F32), pltpu.VMEM((tq, LANES), F32), pltpu.VMEM((tq, dv), F32)],
        compiler_params=_params(("parallel", "parallel", "parallel", "arbitrary")),
        name="flash_fox" if fox else "flash_latent",
    )(*args)


def _swa_kernel(q_ref, kvc_ref, kvp_ref, z_ref, sink_ref, o_ref, *, slopes, scale):
    i = pl.program_id(1)
    w = WINDOW
    kv = jnp.concatenate([kvp_ref[...], kvc_ref[...]], axis=0).astype(F32)
    lane = lax.broadcasted_iota(jnp.int32, (2 * w, LANES), 1)
    low = lane < C_DIM

    def variants(t):
        g0_lo = jnp.where(low, t, 0.0)
        g1_hi = jnp.where(low, 0.0, t)
        g0_hi = pltpu.roll(g0_lo, C_DIM, 1)
        g1_lo = pltpu.roll(g1_hi, C_DIM, 1)
        return ((g0_lo.astype(BF16), g0_hi.astype(BF16)), (g1_lo.astype(BF16), g1_hi.astype(BF16)))

    k_var = variants(kv[:, :LANES])
    v_var = variants(kv[:, LANES:])
    qi = lax.broadcasted_iota(jnp.int32, (w, 2 * w), 0)
    kj = lax.broadcasted_iota(jnp.int32, (w, 2 * w), 1)
    dist = qi + w - kj
    first_key = jnp.where(i > 0, 0, w)
    mask = (dist >= 0) & (dist < w) & (kj >= first_key)
    distf = dist.astype(F32)
    for pair in range(C_HEADS // 2):
        g = (2 * pair) // (C_HEADS // C_KV_HEADS)
        qp = q_ref[:, pair * LANES:(pair + 1) * LANES]
        out = None
        for half in range(2):
            h = 2 * pair + half
            s = _dot_nt(qp, k_var[g][half]) * scale - slopes[h] * distf
            s = jnp.where(mask, s, NEG)
            sink = sink_ref[h]
            m = jnp.maximum(jnp.max(s, axis=1, keepdims=True), sink)
            p = jnp.exp(s - m)
            denom = jnp.sum(p, axis=1, keepdims=True) + jnp.exp(sink - m)
            contrib = _dot((p / denom).astype(BF16), v_var[g][half])
            out = contrib if out is None else out + contrib
        z = z_ref[:, pair * LANES:(pair + 1) * LANES].astype(F32)
        o_ref[:, pair * LANES:(pair + 1) * LANES] = (out * jax.nn.silu(z)).astype(o_ref.dtype)


def _sliding_window(p_arr, sinks, batch, seq):
    w = WINDOW
    nb = seq // w
    kv_blk = P_OFF['c_k'] // (2 * LANES)
    return pl.pallas_call(
        functools.partial(_swa_kernel, slopes=_alibi_slopes(C_HEADS), scale=C_DIM ** -0.5),
        grid=(batch, nb),
        in_specs=[pl.BlockSpec((w, BRANCH_WIDTH), lambda b, i: (b * nb + i, P_OFF['c_q'] // BRANCH_WIDTH)),
                  pl.BlockSpec((w, 2 * LANES), lambda b, i: (b * nb + i, kv_blk)),
                  pl.BlockSpec((w, 2 * LANES), lambda b, i: (b * nb + jnp.maximum(i - 1, 0), kv_blk)),
                  pl.BlockSpec((w, BRANCH_WIDTH), lambda b, i: (b * nb + i, P_OFF['c_z'] // BRANCH_WIDTH)),
                  pl.BlockSpec(memory_space=pltpu.SMEM)],
        out_specs=pl.BlockSpec((w, BRANCH_WIDTH), lambda b, i: (b * nb + i, 0)),
        out_shape=jax.ShapeDtypeStruct((batch * seq, BRANCH_WIDTH), BF16),
        compiler_params=_params(("parallel", "parallel")),
        name="sliding_window",
    )(p_arr, p_arr, p_arr, p_arr, sinks.astype(F32))


def _latent_prep_kernel(cq_ref, ckv_ref, kr_ref, tab_ref, qg_ref, kvg_ref, qup_ref, kup_ref, vup_ref,
                        q_ref, k_ref, v_ref):
    def rms(x, g):
        return x * lax.rsqrt(jnp.mean(x * x, axis=-1, keepdims=True) + RMS_EPS) * g

    tab = tab_ref[...]
    lane = lax.broadcasted_iota(jnp.int32, tab.shape, 1)

    def rotate(t):
        r = t * tab
        return r + pltpu.roll(r, D_ROPE, 1)

    q = _dot(rms(cq_ref[...].astype(F32), qg_ref[...]).astype(BF16), qup_ref[...])
    ckv = rms(ckv_ref[...].astype(F32), kvg_ref[...]).astype(BF16)
    k_nope = _dot(ckv, kup_ref[...])
    v_ref[...] = _dot(ckv, vup_ref[...]).astype(v_ref.dtype)
    k_rot = jnp.where(lane < D_ROPE, rotate(kr_ref[...].astype(F32)), 0.0).astype(k_ref.dtype)
    hw = 2 * LANES
    for h in range(D_HEADS):
        q_ref[:, h * hw:h * hw + LANES] = q[:, h * hw:h * hw + LANES].astype(q_ref.dtype)
        q_ref[:, h * hw + LANES:(h + 1) * hw] = rotate(q[:, h * hw + LANES:(h + 1) * hw]).astype(q_ref.dtype)
        k_ref[:, h * hw:h * hw + LANES] = k_nope[:, h * LANES:(h + 1) * LANES].astype(k_ref.dtype)
        k_ref[:, h * hw + LANES:(h + 1) * hw] = k_rot


def _latent_prep(p_arr, tab, q_gain, kv_gain, q_up, k_up, v_up, tm=512):
    t = p_arr.shape[0]
    tm = min(tm, t)
    hw = 2 * LANES

    def const(shape):
        return pl.BlockSpec(shape, lambda i: (0, 0))

    return pl.pallas_call(
        _latent_prep_kernel,
        grid=(t // tm,),
        in_specs=[pl.BlockSpec((tm, D_Q_LORA), lambda i: (i, P_OFF['d_cq'] // D_Q_LORA)),
                  pl.BlockSpec((tm, D_KV_LORA), lambda i: (i, P_OFF['d_ckv'] // D_KV_LORA)),
                  pl.BlockSpec((tm, LANES), lambda i: (i, P_OFF['d_kr'] // LANES)),
                  pl.BlockSpec((tm, LANES), lambda i: (i, 0)),
                  const((1, D_Q_LORA)), const((1, D_KV_LORA)),
                  const((D_Q_LORA, D_HEADS * hw)), const((D_KV_LORA, D_HEADS * LANES)),
                  const((D_KV_LORA, D_HEADS * LANES))],
        out_specs=[pl.BlockSpec((tm, D_HEADS * hw), lambda i: (i, 0)),
                   pl.BlockSpec((tm, D_HEADS * hw), lambda i: (i, 0)),
                   pl.BlockSpec((tm, D_HEADS * LANES), lambda i: (i, 0))],
        out_shape=[jax.ShapeDtypeStruct((t, D_HEADS * hw), BF16),
                   jax.ShapeDtypeStruct((t, D_HEADS * hw), BF16),
                   jax.ShapeDtypeStruct((t, D_HEADS * LANES), BF16)],
        compiler_params=_params(("parallel",)),
        name="latent_prep",
    )(p_arr, p_arr, p_arr, tab, q_gain.reshape(1, -1), kv_gain.reshape(1, -1), q_up, k_up, v_up)


def _sparse_kernel(q_ref, iq_ref, z_ref, ik_ref, k_ref, vt_ref, wt_ref, o_ref,
                   keys_ref, jcut_ref, m_ref, l_ref, acc_ref, *, topk, ck, slopes, scale, idx_bits):
    i = pl.program_id(1)
    tq = LANES
    nch = ((i + 1) * tq + ck - 1) // ck
    q_pos = i * tq + lax.broadcasted_iota(jnp.int32, (ck, tq), 1)
    s_iota = lax.broadcasted_iota(jnp.int32, (ck, tq), 0)
    lane = lax.broadcasted_iota(jnp.int32, (ck, LANES), 1)
    low = lane < IDX_DIM
    w_all = wt_ref[0] * (IDX_HEADS ** -0.5 * IDX_DIM ** -0.5)

    def score_chunk(c, carry):
        start = pl.multiple_of(c * ck, ck)
        ikc = ik_ref[pl.ds(start, ck), :]
        ik_lo = jnp.where(low, ikc, jnp.zeros_like(ikc))
        ik_hi = jnp.where(low, jnp.zeros_like(ikc), ikc)
        acc = jnp.zeros((ck, tq), F32)
        for pair in range(IDX_HEADS // 2):
            iqp = iq_ref[:, pair * LANES:(pair + 1) * LANES]
            acc = acc + w_all[2 * pair:2 * pair + 1, :] * jnp.maximum(_dot_nt(ik_lo, iqp), 0.0)
            acc = acc + w_all[2 * pair + 1:2 * pair + 2, :] * jnp.maximum(_dot_nt(ik_hi, iqp), 0.0)
        acc = jnp.where(acc == 0.0, 0.0, acc)
        bits = lax.bitcast_convert_type(acc, jnp.int32)
        key = bits ^ ((bits >> 31) & 0x7FFFFFFF)
        key = jnp.where(start + s_iota <= q_pos, key, INT_MIN)
        keys_ref[pl.ds(start, ck), :] = key
        return carry

    lax.fori_loop(0, nch, score_chunk, 0)

    def count(pred_fn):
        def body(c, cnt):
            start = pl.multiple_of(c * ck, ck)
            kc = keys_ref[pl.ds(start, ck), :]
            return cnt + jnp.sum(pred_fn(kc, start + s_iota).astype(F32), axis=0, keepdims=True)
        return lax.fori_loop(0, nch, body, jnp.zeros((1, tq), F32))

    kf = float(topk)

    def thr_step(it, thr):
        cand = thr ^ (jnp.int32(1) << (31 - it))
        cnt = count(lambda kc, pos: kc >= cand)
        return jnp.where(cnt >= kf, cand, thr)

    thr = lax.fori_loop(0, 32, thr_step, jnp.full((1, tq), INT_MIN, jnp.int32))

    n_ge = count(lambda kc, pos: kc >= thr)
    n_gt = count(lambda kc, pos: kc > thr)
    need = kf - n_gt
    tied = (n_ge > kf) & (thr != INT_MIN)
    jcut_ref[...] = jnp.full(jcut_ref.shape, 2 ** 30, jnp.int32)

    @pl.when(jnp.max(tied.astype(F32)) > 0.0)
    def _():
        def cut_step(it, x):
            cand = x + (jnp.int32(1) << (idx_bits - 1 - it))
            cnt = count(lambda kc, pos: (kc == thr) & (pos < cand))
            return jnp.where(cnt < need, cand, x)
        x = lax.fori_loop(0, idx_bits, cut_step, jnp.zeros((1, tq), jnp.int32))
        jcut_ref[...] = jnp.broadcast_to(jnp.where(tied, x, 2 ** 30), jcut_ref.shape)

    jcut = jcut_ref[0:1, :]

    m_ref[...] = jnp.full(m_ref.shape, NEG, F32)
    l_ref[...] = jnp.zeros(l_ref.shape, F32)
    acc_ref[...] = jnp.zeros(acc_ref.shape, F32)
    rep = A_HEADS // A_KV_HEADS
    q_groups = [jnp.concatenate([q_ref[:, (g * rep + r) * A_DIM:(g * rep + r + 1) * A_DIM] for r in range(rep)],
                                axis=0) for g in range(A_KV_HEADS)]

    def attend_chunk(c, carry):
        start = pl.multiple_of(c * ck, ck)
        kc = keys_ref[pl.ds(start, ck), :]
        pos = start + s_iota
        sel = ((kc > thr) | ((kc == thr) & (pos <= jcut))) & (pos <= q_pos)
        dist = (q_pos - pos).astype(F32)
        for g in range(A_KV_HEADS):
            kg = k_ref[pl.ds(start, ck), g * A_DIM:(g + 1) * A_DIM]
            vtg = vt_ref[0, c, g * A_DIM:(g + 1) * A_DIM, :]
            logits = _dot_nt(kg, q_groups[g])
            for r in range(rep):
                h = g * rep + r
                s = logits[:, r * tq:(r + 1) * tq] * scale - slopes[h] * dist
                s = jnp.where(sel, s, NEG)
                m_prev = m_ref[h:h + 1, :]
                m_new = jnp.maximum(m_prev, jnp.max(s, axis=0, keepdims=True))
                alpha = jnp.exp(m_prev - m_new)
                p = jnp.where(sel, jnp.exp(s - m_new), 0.0)
                l_ref[h:h + 1, :] = alpha * l_ref[h:h + 1, :] + jnp.sum(p, axis=0, keepdims=True)
                acc_ref[h * A_DIM:(h + 1) * A_DIM, :] = (alpha * acc_ref[h * A_DIM:(h + 1) * A_DIM, :]
                                                         + _dot(vtg, p.astype(BF16)))
                m_ref[h:h + 1, :] = m_new
        return carry

    lax.fori_loop(0, nch, attend_chunk, 0)

    for h in range(A_HEADS):
        out_t = acc_ref[h * A_DIM:(h + 1) * A_DIM, :] / l_ref[h:h + 1, :]
        z = z_ref[:, h * A_DIM:(h + 1) * A_DIM].astype(F32)
        o_ref[:, h * A_DIM:(h + 1) * A_DIM] = (out_t.T * jax.nn.silu(z)).astype(o_ref.dtype)


def _sparse_attention(p_arr, vt, wt, batch, seq):
    tq = LANES
    nq = seq // tq
    ck = min(512, seq)
    topk = min(IDX_TOPK_MAX, seq // 4)
    idx_bits = int(seq).bit_length()
    return pl.pallas_call(
        functools.partial(_sparse_kernel, topk=topk, ck=ck, slopes=_alibi_slopes(A_HEADS),
                          scale=A_DIM ** -0.5, idx_bits=idx_bits),
        grid=(batch, nq),
        in_specs=[pl.BlockSpec((tq, BRANCH_WIDTH), lambda b, i: (b * nq + i, P_OFF['a_q'] // BRANCH_WIDTH)),
                  pl.BlockSpec((tq, BRANCH_WIDTH), lambda b, i: (b * nq + i, P_OFF['a_iq'] // BRANCH_WIDTH)),
                  pl.BlockSpec((tq, BRANCH_WIDTH), lambda b, i: (b * nq + i, P_OFF['a_z'] // BRANCH_WIDTH)),
                  pl.BlockSpec((seq, LANES), lambda b, i: (b, P_OFF['a_ik'] // LANES)),
                  pl.BlockSpec((seq, 2 * LANES), lambda b, i: (b, P_OFF['a_k'] // (2 * LANES))),
                  pl.BlockSpec((1, seq // ck, 2 * LANES, ck), lambda b, i: (b, 0, 0, 0)),
                  pl.BlockSpec((1, IDX_HEADS, tq), lambda b, i: (b, 0, i))],
        out_specs=pl.BlockSpec((tq, BRANCH_WIDTH), lambda b, i: (b * nq + i, 0)),
        out_shape=jax.ShapeDtypeStruct((batch * seq, BRANCH_WIDTH), BF16),
        scratch_shapes=[pltpu.VMEM((seq, tq), jnp.int32), pltpu.VMEM((8, tq), jnp.int32),
                        pltpu.VMEM((A_HEADS, tq), F32), pltpu.VMEM((A_HEADS, tq), F32),
                        pltpu.VMEM((A_HEADS * A_DIM, tq), F32)],
        compiler_params=_params(("parallel", "arbitrary")),
        name="sparse_attention",
    )(p_arr, p_arr, p_arr, p_arr, p_arr, vt, wt)


def _swap_halves(w):
    half = w.shape[-1] // 2
    return jnp.concatenate([-w[..., half:], w[..., :half]], axis=-1)


def _prepare_in_proj(w_in):
    sizes = [n for _, n in IN_SEGMENTS]
    offs = np.concatenate([[0], np.cumsum(sizes)])
    seg = {name: w_in[:, int(offs[k]):int(offs[k + 1])] for k, (name, _) in enumerate(IN_SEGMENTS)}
    cols = []
    for name, _ in P_LAYOUT:
        if name == 'a_ik':
            cols += [seg['a_ik'], seg['a_ik']]
        elif name == 'd_kr':
            cols += [seg['d_kr'], _swap_halves(seg['d_kr'])]
        else:
            cols.append(seg[name])
    w_main = jnp.concatenate(cols, axis=1).astype(BF16)
    pad = jnp.zeros((w_in.shape[0], LANES - IDX_HEADS - B_HEADS), w_in.dtype)
    w_small = jnp.concatenate([seg['a_iw'], seg['b_f'], pad], axis=1).astype(BF16)
    return w_main, w_small


def _prepare_latent(dq_up, dkv_up):
    q = dq_up.reshape(D_Q_LORA, D_HEADS, D_NOPE + D_ROPE)
    rope = q[..., D_NOPE:]
    q_up = jnp.concatenate([q[..., :D_NOPE], rope, _swap_halves(rope)], axis=-1)
    kv = dkv_up.reshape(D_KV_LORA, D_HEADS, D_NOPE + D_VDIM)
    return (q_up.reshape(D_Q_LORA, -1).astype(BF16),
            kv[..., :D_NOPE].reshape(D_KV_LORA, -1).astype(BF16),
            kv[..., D_NOPE:].reshape(D_KV_LORA, -1).astype(BF16))


def _rope_table(positions):
    half = D_ROPE // 2
    inv_freq = ROPE_THETA ** (-jnp.arange(half, dtype=F32) / half)
    ang = positions.astype(F32)[..., None] * inv_freq
    cos, sin = jnp.cos(ang), jnp.sin(ang)
    tab = jnp.concatenate([cos, cos, sin, sin], axis=-1)
    return tab.reshape(-1, 4 * half)


def _layer(x, xb, tab, batch, seq, w_in, w_gate, w_branch, w_out, dq_gain, dq_up, dkv_gain, dkv_up,
           f_bias, sinks, ln_gain, ln_bias, alpha):
    w_main, w_small = _prepare_in_proj(w_in)
    proj = _matmul(xb, w_main, BF16)
    small = _matmul(xb, w_small, F32, tn=LANES)

    ck = min(512, seq)
    wt = small[:, :IDX_HEADS].reshape(batch, seq, IDX_HEADS).transpose(0, 2, 1)
    a_v = proj[:, P_OFF['a_v']:P_OFF['a_v'] + 2 * LANES]
    vt = a_v.reshape(batch, seq // ck, ck, 2 * LANES).transpose(0, 1, 3, 2)
    o_a = _sparse_attention(proj, vt, wt, batch, seq)

    f_rows = small[:, IDX_HEADS:IDX_HEADS + B_HEADS].reshape(batch, seq, B_HEADS).transpose(0, 2, 1)
    cum = _forget_cumsum(f_rows.reshape(batch * B_HEADS, seq),
                         jnp.tile(f_bias.astype(F32), batch).reshape(batch * B_HEADS, 1))
    cum = cum.reshape(batch, B_HEADS, seq)
    o_b = _flash_attention(proj, P_OFF['b_q'] // B_DIM, proj, P_OFF['b_k'] // B_DIM, proj, P_OFF['b_v'] // B_DIM,
                           proj, P_OFF['b_z'] // B_DIM, batch=batch, seq=seq, heads=B_HEADS,
                           dqk=B_DIM, dv=B_DIM, scale=B_DIM ** -0.5, cum=cum)

    o_c = _sliding_window(proj, sinks, batch, seq)

    q_up, k_up, v_up = _prepare_latent(dq_up, dkv_up)
    q_d, k_d, v_d = _latent_prep(proj, tab, dq_gain, dkv_gain, q_up, k_up, v_up)
    o_d = _flash_attention(q_d, 0, k_d, 0, v_d, 0, proj, P_OFF['d_z'] // D_VDIM, batch=batch, seq=seq,
                           heads=D_HEADS, dqk=2 * LANES, dv=D_VDIM, scale=(D_NOPE + D_ROPE) ** -0.5)

    merged = _gated_merge(xb, w_gate.astype(BF16), (o_a, o_b, o_c, o_d), w_branch.astype(BF16))
    y = _matmul(merged, w_out.astype(BF16), F32)
    return _residual_layernorm(x, y, ln_gain, ln_bias, alpha)


def kernel(x, positions, w_in, w_gate, w_branch, w_out, dq_gain, dq_up, dkv_gain, dkv_up, f_bias, sinks,
           ln_gain, ln_bias):
    batch, seq, d = x.shape
    depth = w_in.shape[0]
    alpha = (2 * depth) ** 0.25
    tab = _rope_table(positions)
    xf = x.reshape(batch * seq, d)
    xb = xf.astype(BF16)
    for l in range(depth):
        xf, xb = _layer(xf, xb, tab, batch, seq, w_in[l], w_gate[l], w_branch[l], w_out[l], dq_gain[l],
                        dq_up[l], dkv_gain[l], dkv_up[l], f_bias[l], sinks[l], ln_gain[l], ln_bias[l], alpha)
    return xf.reshape(batch, seq, d)
```

```python
import functools

import numpy as np
import jax
import jax.numpy as jnp
from jax import lax
from jax.experimental import pallas as pl
from jax.experimental.pallas import tpu as pltpu

BRANCH_WIDTH = 1024
A_HEADS, A_KV_HEADS, A_DIM = 8, 2, 128
IDX_HEADS, IDX_DIM, IDX_TOPK_MAX = 16, 64, 256
B_HEADS, B_DIM = 8, 128
C_HEADS, C_KV_HEADS, C_DIM, WINDOW = 16, 2, 64, 128
D_HEADS, D_Q_LORA, D_KV_LORA, D_NOPE, D_ROPE, D_VDIM = 8, 768, 256, 128, 64, 128
ROPE_THETA = 10000.0
RMS_EPS = 1e-6
LN_EPS = 1e-5

IN_SEGMENTS = (
    ('a_q', 1024), ('a_k', 256), ('a_v', 256), ('a_iq', 1024), ('a_ik', 64), ('a_iw', 16), ('a_z', 1024),
    ('b_q', 1024), ('b_k', 1024), ('b_v', 1024), ('b_f', 8), ('b_z', 1024),
    ('c_q', 1024), ('c_k', 128), ('c_v', 128), ('c_z', 1024),
    ('d_cq', 768), ('d_ckv', 256), ('d_kr', 64), ('d_z', 1024),
)

P_LAYOUT = (
    ('d_cq', 768), ('d_ckv', 256), ('a_q', 1024), ('a_iq', 1024), ('a_z', 1024),
    ('b_q', 1024), ('b_k', 1024), ('b_v', 1024), ('b_z', 1024), ('c_q', 1024), ('c_z', 1024),
    ('d_z', 1024), ('a_k', 256), ('a_v', 256), ('a_ik', 128), ('d_kr', 128), ('c_k', 128), ('c_v', 128),
)
P_OFF = {}
_o = 0
for _n, _w in P_LAYOUT:
    assert _o % _w == 0
    P_OFF[_n] = _o
    _o += _w
P_WIDTH = _o

LANES = 128
SUBLANES = 8
NEG = -1e30
MASKED = -3e38
LOG2E = 1.4426950408889634
INT_MIN = -2 ** 31
VMEM_LIMIT = 56 * 1024 * 1024

F32 = jnp.float32
BF16 = jnp.bfloat16


def _params(sem, vmem=VMEM_LIMIT):
    return pltpu.CompilerParams(dimension_semantics=sem, vmem_limit_bytes=vmem)


def _dot(a, b):
    return jnp.dot(a, b, preferred_element_type=F32)


def _dot_nt(a, b):
    return lax.dot_general(a, b, (((1,), (1,)), ((), ())), preferred_element_type=F32)


def _alibi_slopes(n_heads):
    return [float(np.float32(2.0 ** (-8.0 * (h + 1) / n_heads))) for h in range(n_heads)]


def _mm_kernel(x_ref, w_ref, o_ref):
    o_ref[...] = _dot(x_ref[...], w_ref[...]).astype(o_ref.dtype)


def _matmul(x, w, out_dtype, tm=1024, tn=1024):
    m, k = x.shape
    n = w.shape[1]
    tm, tn = min(tm, m), min(tn, n)
    return pl.pallas_call(
        _mm_kernel,
        grid=(m // tm, n // tn),
        in_specs=[pl.BlockSpec((tm, k), lambda i, j: (i, 0)),
                  pl.BlockSpec((k, tn), lambda i, j: (0, j))],
        out_specs=pl.BlockSpec((tm, tn), lambda i, j: (i, j)),
        out_shape=jax.ShapeDtypeStruct((m, n), out_dtype),
        compiler_params=_params(("parallel", "parallel")),
        name="dense_matmul",
    )(x, w)


def _merge_kernel(x_ref, wg_ref, b0_ref, b1_ref, b2_ref, b3_ref, wb_ref, o_ref, acc_ref):
    n = pl.program_id(2)
    gate = jax.nn.sigmoid(_dot(x_ref[...], wg_ref[0]))

    @pl.when(n == 0)
    def _():
        acc_ref[...] = gate * _dot(b0_ref[...], wb_ref[0])

    for idx, b_ref in ((1, b1_ref), (2, b2_ref), (3, b3_ref)):
        @pl.when(n == idx)
        def _(b_ref=b_ref):
            acc_ref[...] += gate * _dot(b_ref[...], wb_ref[0])

    @pl.when(n == 3)
    def _():
        o_ref[...] = acc_ref[...].astype(o_ref.dtype)


def _gated_merge(xb, wg, branches, wb, tm=1024, tn=512):
    t, d = xb.shape
    tm, tn = min(tm, t), min(tn, d)
    bw = branches[0].shape[1]
    bspec = pl.BlockSpec((tm, bw), lambda i, j, n: (i, 0))
    return pl.pallas_call(
        _merge_kernel,
        grid=(t // tm, d // tn, 4),
        in_specs=[pl.BlockSpec((tm, d), lambda i, j, n: (i, 0)),
                  pl.BlockSpec((1, d, tn), lambda i, j, n: (n, 0, j)),
                  bspec, bspec, bspec, bspec,
                  pl.BlockSpec((1, bw, tn), lambda i, j, n: (n, 0, j))],
        out_specs=pl.BlockSpec((tm, tn), lambda i, j, n: (i, j)),
        out_shape=jax.ShapeDtypeStruct((t, d), BF16),
        scratch_shapes=[pltpu.VMEM((tm, tn), F32)],
        compiler_params=_params(("parallel", "parallel", "arbitrary")),
        name="gated_merge",
    )(xb, wg, *branches, wb)


def _ln_kernel(x_ref, y_ref, g_ref, b_ref, o_ref, ob_ref, *, alpha):
    r = alpha * x_ref[...] + y_ref[...]
    mu = jnp.mean(r, axis=-1, keepdims=True)
    c = r - mu
    var = jnp.mean(c * c, axis=-1, keepdims=True)
    out = c * lax.rsqrt(var + LN_EPS) * g_ref[...] + b_ref[...]
    o_ref[...] = out
    ob_ref[...] = out.astype(BF16)


def _residual_layernorm(x, y, gain, bias, alpha, tm=256):
    t, d = x.shape
    tm = min(tm, t)
    row = pl.BlockSpec((tm, d), lambda i: (i, 0))
    vec = pl.BlockSpec((1, d), lambda i: (0, 0))
    return pl.pallas_call(
        functools.partial(_ln_kernel, alpha=alpha),
        grid=(t // tm,),
        in_specs=[row, row, vec, vec],
        out_specs=[row, row],
        out_shape=[jax.ShapeDtypeStruct((t, d), F32), jax.ShapeDtypeStruct((t, d), BF16)],
        compiler_params=_params(("parallel",)),
        name="residual_layernorm",
    )(x, y, gain.reshape(1, d), bias.reshape(1, d))


def _forget_cumsum_kernel(f_ref, bias_ref, c_ref):
    rows, s = f_ref.shape
    lane = lax.broadcasted_iota(jnp.int32, (rows, LANES), 1)
    carry = jnp.zeros((rows, 1), F32)
    for c in range(s // LANES):
        x = jax.nn.log_sigmoid(f_ref[:, c * LANES:(c + 1) * LANES] + bias_ref[...])
        shift = 1
        while shift < LANES:
            x = x + jnp.where(lane >= shift, pltpu.roll(x, shift, 1), 0.0)
            shift *= 2
        x = x + carry
        c_ref[:, c * LANES:(c + 1) * LANES] = x * LOG2E
        carry = x[:, LANES - 1:LANES]


def _forget_cumsum(f_rows, bias_rows):
    rows, s = f_rows.shape
    full = pl.BlockSpec((rows, s), lambda: (0, 0))
    return pl.pallas_call(
        _forget_cumsum_kernel,
        in_specs=[full, pl.BlockSpec((rows, 1), lambda: (0, 0))],
        out_specs=full,
        out_shape=jax.ShapeDtypeStruct((rows, s), F32),
        name="forget_cumsum",
    )(f_rows, bias_rows)


def _tree(parts, op):
    parts = list(parts)
    while len(parts) > 1:
        nxt = [op(parts[a], parts[a + 1]) for a in range(0, len(parts) - 1, 2)]
        if len(parts) % 2:
            nxt.append(parts[-1])
        parts = nxt
    return parts[0]


def _col_reduce(x, op, reduce_fn):
    rows = x.shape[0]
    part = _tree([x[r:r + SUBLANES, :] for r in range(0, rows, SUBLANES)], op)
    return reduce_fn(part, axis=0, keepdims=True)


def _flash_kernel(i_tab, j_tab, *refs, fox, heads, dqk, dv, tile):
    if fox:
        q_ref, k_ref, vt_ref, ck_ref, z_ref, o_ref, m_sc, l_sc, acc_sc = refs
    else:
        q_ref, k_ref, vt_ref, z_ref, o_ref, m_sc, l_sc, acc_sc = refs
    pair = pl.program_id(1)
    i = i_tab[pair]
    j = j_tab[pair]

    @pl.when(j == 0)
    def _():
        m_sc[...] = jnp.full(m_sc.shape, NEG, F32)
        l_sc[...] = jnp.zeros(l_sc.shape, F32)
        acc_sc[...] = jnp.zeros(acc_sc.shape, F32)

    def step(masked):
        if masked:
            causal = (lax.broadcasted_iota(jnp.int32, (tile, tile), 0)
                      <= lax.broadcasted_iota(jnp.int32, (tile, tile), 1))
        def logits(h):
            return _dot_nt(k_ref[:, h * dqk:(h + 1) * dqk], q_ref[:, h * dqk:(h + 1) * dqk])

        s_next = logits(0)
        for h in range(heads):
            s = s_next
            if h + 1 < heads:
                s_next = logits(h + 1)
            if fox:
                s = s - ck_ref[:, h:h + 1]
            if masked:
                s = jnp.where(causal, s, NEG)
            m_prev = m_sc[h:h + 1, :]
            m_new = jnp.maximum(m_prev, _col_reduce(s, jnp.maximum, jnp.max))
            alpha = jnp.exp2(m_prev - m_new)
            p = jnp.exp2(s - m_new)
            l_sc[h:h + 1, :] = alpha * l_sc[h:h + 1, :] + _col_reduce(p, jnp.add, jnp.sum)
            acc_sc[h * dv:(h + 1) * dv, :] = (alpha * acc_sc[h * dv:(h + 1) * dv, :]
                                              + _dot(vt_ref[h * dv:(h + 1) * dv, :], p.astype(BF16)))
            m_sc[h:h + 1, :] = m_new

    @pl.when(j < i)
    def _():
        step(False)

    @pl.when(j == i)
    def _():
        step(True)
        for h in range(heads):
            out_t = acc_sc[h * dv:(h + 1) * dv, :] / l_sc[h:h + 1, :]
            z = z_ref[:, h * dv:(h + 1) * dv].astype(F32)
            o_ref[:, h * dv:(h + 1) * dv] = (out_t.T * jax.nn.silu(z)).astype(o_ref.dtype)


def _flash_attention(q_arr, q_blk, k_arr, k_blk, vt_arr, z_arr, z_blk, *, batch, seq, heads,
                     dqk, dv, cum=None, tile=512):
    tile = min(tile, seq)
    nq = seq // tile
    fox = cum is not None
    pairs = [(i, j) for i in range(nq) for j in range(i + 1)]
    i_tab = jnp.asarray([p[0] for p in pairs], jnp.int32)
    j_tab = jnp.asarray([p[1] for p in pairs], jnp.int32)
    in_specs = [
        pl.BlockSpec((tile, heads * dqk), lambda b, p, it, jt: (b * nq + it[p], q_blk)),
        pl.BlockSpec((tile, heads * dqk), lambda b, p, it, jt: (b * nq + jt[p], k_blk)),
        pl.BlockSpec((heads * dv, tile), lambda b, p, it, jt: (0, b * nq + jt[p])),
    ]
    args = [q_arr, k_arr, vt_arr]
    if fox:
        in_specs.append(pl.BlockSpec((tile, LANES), lambda b, p, it, jt: (b * nq + jt[p], 0)))
        args.append(cum)
    in_specs.append(pl.BlockSpec((tile, heads * dv), lambda b, p, it, jt: (b * nq + it[p], z_blk)))
    args.append(z_arr)
    return pl.pallas_call(
        functools.partial(_flash_kernel, fox=fox, heads=heads, dqk=dqk, dv=dv, tile=tile),
        grid_spec=pltpu.PrefetchScalarGridSpec(
            num_scalar_prefetch=2,
            grid=(batch, len(pairs)),
            in_specs=in_specs,
            out_specs=pl.BlockSpec((tile, heads * dv), lambda b, p, it, jt: (b * nq + it[p], 0)),
            scratch_shapes=[pltpu.VMEM((heads, tile), F32), pltpu.VMEM((heads, tile), F32),
                            pltpu.VMEM((heads * dv, tile), F32)]),
        out_shape=jax.ShapeDtypeStruct((batch * seq, heads * dv), BF16),
        compiler_params=_params(("parallel", "arbitrary")),
        name="flash_fox" if fox else "flash_latent",
    )(i_tab, j_tab, *args)


def _swa_kernel(q_ref, kvc_ref, kvp_ref, z_ref, sink_ref, o_ref, *, slopes, scale):
    i = pl.program_id(1)
    w = WINDOW
    kv = jnp.concatenate([kvp_ref[...], kvc_ref[...]], axis=0).astype(F32)
    lane = lax.broadcasted_iota(jnp.int32, (2 * w, LANES), 1)
    low = lane < C_DIM

    def variants(t):
        g0_lo = jnp.where(low, t, 0.0)
        g1_hi = jnp.where(low, 0.0, t)
        g0_hi = pltpu.roll(g0_lo, C_DIM, 1)
        g1_lo = pltpu.roll(g1_hi, C_DIM, 1)
        return ((g0_lo.astype(BF16), g0_hi.astype(BF16)), (g1_lo.astype(BF16), g1_hi.astype(BF16)))

    k_var = variants(kv[:, :LANES])
    v_var = variants(kv[:, LANES:])
    qi = lax.broadcasted_iota(jnp.int32, (w, 2 * w), 0)
    kj = lax.broadcasted_iota(jnp.int32, (w, 2 * w), 1)
    dist = qi + w - kj
    first_key = jnp.where(i > 0, 0, w)
    mask = (dist >= 0) & (dist < w) & (kj >= first_key)
    distf = dist.astype(F32)
    for pair in range(C_HEADS // 2):
        g = (2 * pair) // (C_HEADS // C_KV_HEADS)
        qp = q_ref[:, pair * LANES:(pair + 1) * LANES]
        out = None
        for half in range(2):
            h = 2 * pair + half
            s = _dot_nt(qp, k_var[g][half]) * scale - slopes[h] * distf
            s = jnp.where(mask, s, NEG)
            sink = sink_ref[h]
            m = jnp.maximum(jnp.max(s, axis=1, keepdims=True), sink)
            p = jnp.exp(s - m)
            denom = jnp.sum(p, axis=1, keepdims=True) + jnp.exp(sink - m)
            contrib = _dot((p / denom).astype(BF16), v_var[g][half])
            out = contrib if out is None else out + contrib
        z = z_ref[:, pair * LANES:(pair + 1) * LANES].astype(F32)
        o_ref[:, pair * LANES:(pair + 1) * LANES] = (out * jax.nn.silu(z)).astype(o_ref.dtype)


def _sliding_window(p_arr, sinks, batch, seq):
    w = WINDOW
    nb = seq // w
    kv_blk = P_OFF['c_k'] // (2 * LANES)
    return pl.pallas_call(
        functools.partial(_swa_kernel, slopes=_alibi_slopes(C_HEADS), scale=C_DIM ** -0.5),
        grid=(batch, nb),
        in_specs=[pl.BlockSpec((w, BRANCH_WIDTH), lambda b, i: (b * nb + i, P_OFF['c_q'] // BRANCH_WIDTH)),
                  pl.BlockSpec((w, 2 * LANES), lambda b, i: (b * nb + i, kv_blk)),
                  pl.BlockSpec((w, 2 * LANES), lambda b, i: (b * nb + jnp.maximum(i - 1, 0), kv_blk)),
                  pl.BlockSpec((w, BRANCH_WIDTH), lambda b, i: (b * nb + i, P_OFF['c_z'] // BRANCH_WIDTH)),
                  pl.BlockSpec(memory_space=pltpu.SMEM)],
        out_specs=pl.BlockSpec((w, BRANCH_WIDTH), lambda b, i: (b * nb + i, 0)),
        out_shape=jax.ShapeDtypeStruct((batch * seq, BRANCH_WIDTH), BF16),
        compiler_params=_params(("parallel", "parallel")),
        name="sliding_window",
    )(p_arr, p_arr, p_arr, p_arr, sinks.astype(F32))


def _latent_prep_kernel(cq_ref, ckv_ref, kr_ref, tab_ref, qg_ref, kvg_ref, qup_ref, kup_ref, vupt_ref,
                        q_ref, k_ref, vt_ref):
    def rms(x, g):
        return x * lax.rsqrt(jnp.mean(x * x, axis=-1, keepdims=True) + RMS_EPS) * g

    tab = tab_ref[...]
    lane = lax.broadcasted_iota(jnp.int32, tab.shape, 1)

    def rotate(t):
        r = t * tab
        return r + pltpu.roll(r, D_ROPE, 1)

    q = _dot(rms(cq_ref[...].astype(F32), qg_ref[...]).astype(BF16), qup_ref[...])
    ckv = rms(ckv_ref[...].astype(F32), kvg_ref[...]).astype(BF16)
    k_nope = _dot(ckv, kup_ref[...])
    vt_ref[...] = _dot_nt(vupt_ref[...], ckv).astype(vt_ref.dtype)
    k_rot = jnp.where(lane < D_ROPE, rotate(kr_ref[...].astype(F32)), 0.0).astype(k_ref.dtype)
    hw = 2 * LANES
    for h in range(D_HEADS):
        q_ref[:, h * hw:h * hw + LANES] = q[:, h * hw:h * hw + LANES].astype(q_ref.dtype)
        q_ref[:, h * hw + LANES:(h + 1) * hw] = rotate(q[:, h * hw + LANES:(h + 1) * hw]).astype(q_ref.dtype)
        k_ref[:, h * hw:h * hw + LANES] = k_nope[:, h * LANES:(h + 1) * LANES].astype(k_ref.dtype)
        k_ref[:, h * hw + LANES:(h + 1) * hw] = k_rot


def _latent_prep(p_arr, tab, q_gain, kv_gain, q_up, k_up, v_up_t, tm=512):
    t = p_arr.shape[0]
    tm = min(tm, t)
    hw = 2 * LANES

    def const(shape):
        return pl.BlockSpec(shape, lambda i: (0, 0))

    return pl.pallas_call(
        _latent_prep_kernel,
        grid=(t // tm,),
        in_specs=[pl.BlockSpec((tm, D_Q_LORA), lambda i: (i, P_OFF['d_cq'] // D_Q_LORA)),
                  pl.BlockSpec((tm, D_KV_LORA), lambda i: (i, P_OFF['d_ckv'] // D_KV_LORA)),
                  pl.BlockSpec((tm, LANES), lambda i: (i, P_OFF['d_kr'] // LANES)),
                  pl.BlockSpec((tm, LANES), lambda i: (i, 0)),
                  const((1, D_Q_LORA)), const((1, D_KV_LORA)),
                  const((D_Q_LORA, D_HEADS * hw)), const((D_KV_LORA, D_HEADS * LANES)),
                  const((D_HEADS * D_VDIM, D_KV_LORA))],
        out_specs=[pl.BlockSpec((tm, D_HEADS * hw), lambda i: (i, 0)),
                   pl.BlockSpec((tm, D_HEADS * hw), lambda i: (i, 0)),
                   pl.BlockSpec((D_HEADS * D_VDIM, tm), lambda i: (0, i))],
        out_shape=[jax.ShapeDtypeStruct((t, D_HEADS * hw), BF16),
                   jax.ShapeDtypeStruct((t, D_HEADS * hw), BF16),
                   jax.ShapeDtypeStruct((D_HEADS * D_VDIM, t), BF16)],
        compiler_params=_params(("parallel",)),
        name="latent_prep",
    )(p_arr, p_arr, p_arr, tab, q_gain.reshape(1, -1), kv_gain.reshape(1, -1), q_up, k_up, v_up_t)


def _sparse_kernel(q_ref, iq_ref, z_ref, ik_ref, k_ref, vt_ref, wt_ref, o_ref,
                   keys_ref, jcut_ref, m_ref, l_ref, acc_ref, *, topk, ck, slopes, idx_bits):
    i = pl.program_id(1)
    tq = LANES
    nch = ((i + 1) * tq + ck - 1) // ck
    q_pos = i * tq + lax.broadcasted_iota(jnp.int32, (ck, tq), 1)
    s_iota = lax.broadcasted_iota(jnp.int32, (ck, tq), 0)
    lane = lax.broadcasted_iota(jnp.int32, (ck, LANES), 1)
    low = lane < IDX_DIM
    w_all = wt_ref[0] * (IDX_HEADS ** -0.5 * IDX_DIM ** -0.5)

    def score_chunk(c, carry):
        start = pl.multiple_of(c * ck, ck)
        ikc = ik_ref[pl.ds(start, ck), :]
        ik_lo = jnp.where(low, ikc, jnp.zeros_like(ikc))
        ik_hi = jnp.where(low, jnp.zeros_like(ikc), ikc)
        acc = jnp.zeros((ck, tq), F32)
        for pair in range(IDX_HEADS // 2):
            iqp = iq_ref[:, pair * LANES:(pair + 1) * LANES]
            acc = acc + w_all[2 * pair:2 * pair + 1, :] * jnp.maximum(_dot_nt(ik_lo, iqp), 0.0)
            acc = acc + w_all[2 * pair + 1:2 * pair + 2, :] * jnp.maximum(_dot_nt(ik_hi, iqp), 0.0)
        acc = jnp.where(acc == 0.0, 0.0, acc)
        bits = lax.bitcast_convert_type(acc, jnp.int32)
        key = bits ^ ((bits >> 31) & 0x7FFFFFFF)
        key = jnp.where(start + s_iota <= q_pos, key, INT_MIN)
        keys_ref[pl.ds(start, ck), :] = key
        return carry

    lax.fori_loop(0, nch, score_chunk, 0)

    def count(pred_fn):
        def body(c, cnt):
            start = pl.multiple_of(c * ck, ck)
            hit = jnp.where(pred_fn(keys_ref[pl.ds(start, ck), :], start + s_iota), 1.0, 0.0)
            return cnt + _tree([hit[r:r + SUBLANES, :] for r in range(0, ck, SUBLANES)], jnp.add)
        part = lax.fori_loop(0, nch, body, jnp.zeros((SUBLANES, tq), F32))
        return jnp.sum(part, axis=0, keepdims=True)

    kf = float(topk)

    def thr_step(it, thr):
        cand = thr ^ (jnp.int32(1) << (31 - it))
        cnt = count(lambda kc, pos: kc >= cand)
        return jnp.where(cnt >= kf, cand, thr)

    thr = lax.fori_loop(0, 32, thr_step, jnp.full((1, tq), INT_MIN, jnp.int32))

    n_ge = count(lambda kc, pos: kc >= thr)
    n_gt = count(lambda kc, pos: kc > thr)
    need = kf - n_gt
    tied = (n_ge > kf) & (thr != INT_MIN)
    jcut_ref[...] = jnp.full(jcut_ref.shape, 2 ** 30, jnp.int32)

    @pl.when(jnp.max(tied.astype(F32)) > 0.0)
    def _():
        def cut_step(it, x):
            cand = x + (jnp.int32(1) << (idx_bits - 1 - it))
            cnt = count(lambda kc, pos: (kc == thr) & (pos < cand))
            return jnp.where(cnt < need, cand, x)
        x = lax.fori_loop(0, idx_bits, cut_step, jnp.zeros((1, tq), jnp.int32))
        jcut_ref[...] = jnp.broadcast_to(jnp.where(tied, x, 2 ** 30), jcut_ref.shape)

    jcut = jcut_ref[0:1, :]

    m_ref[...] = jnp.full(m_ref.shape, NEG, F32)
    l_ref[...] = jnp.zeros(l_ref.shape, F32)
    acc_ref[...] = jnp.zeros(acc_ref.shape, F32)
    rep = A_HEADS // A_KV_HEADS
    q_groups = [jnp.concatenate([q_ref[:, (g * rep + r) * A_DIM:(g * rep + r + 1) * A_DIM] for r in range(rep)],
                                axis=0) for g in range(A_KV_HEADS)]

    def attend_chunk(c, carry):
        start = pl.multiple_of(c * ck, ck)
        kc = keys_ref[pl.ds(start, ck), :]
        pos = start + s_iota
        sel = ((kc > thr) | ((kc == thr) & (pos <= jcut))) & (pos <= q_pos)
        dist = (q_pos - pos).astype(F32)
        logits_g = [_dot_nt(k_ref[pl.ds(start, ck), g * A_DIM:(g + 1) * A_DIM], q_groups[g])
                    for g in range(A_KV_HEADS)]
        for g in range(A_KV_HEADS):
            vtg = vt_ref[0, c, g * A_DIM:(g + 1) * A_DIM, :]
            for r in range(rep):
                h = g * rep + r
                s = logits_g[g][:, r * tq:(r + 1) * tq] - (slopes[h] * LOG2E) * dist
                s = jnp.where(sel, s, MASKED)
                m_prev = m_ref[h:h + 1, :]
                m_new = jnp.maximum(m_prev, _col_reduce(s, jnp.maximum, jnp.max))
                alpha = jnp.exp2(m_prev - m_new)
                p = jnp.exp2(s - m_new)
                l_ref[h:h + 1, :] = alpha * l_ref[h:h + 1, :] + _col_reduce(p, jnp.add, jnp.sum)
                acc_ref[h * A_DIM:(h + 1) * A_DIM, :] = (alpha * acc_ref[h * A_DIM:(h + 1) * A_DIM, :]
                                                         + _dot(vtg, p.astype(BF16)))
                m_ref[h:h + 1, :] = m_new
        return carry

    lax.fori_loop(0, nch, attend_chunk, 0)

    for h in range(A_HEADS):
        out_t = acc_ref[h * A_DIM:(h + 1) * A_DIM, :] / l_ref[h:h + 1, :]
        z = z_ref[:, h * A_DIM:(h + 1) * A_DIM].astype(F32)
        o_ref[:, h * A_DIM:(h + 1) * A_DIM] = (out_t.T * jax.nn.silu(z)).astype(o_ref.dtype)


def _sparse_attention(p_arr, vt, wt, batch, seq):
    tq = LANES
    nq = seq // tq
    ck = min(512, seq)
    topk = min(IDX_TOPK_MAX, seq // 4)
    idx_bits = int(seq).bit_length()
    return pl.pallas_call(
        functools.partial(_sparse_kernel, topk=topk, ck=ck, slopes=_alibi_slopes(A_HEADS), idx_bits=idx_bits),
        grid=(batch, nq),
        in_specs=[pl.BlockSpec((tq, BRANCH_WIDTH), lambda b, i: (b * nq + i, P_OFF['a_q'] // BRANCH_WIDTH)),
                  pl.BlockSpec((tq, BRANCH_WIDTH), lambda b, i: (b * nq + i, P_OFF['a_iq'] // BRANCH_WIDTH)),
                  pl.BlockSpec((tq, BRANCH_WIDTH), lambda b, i: (b * nq + i, P_OFF['a_z'] // BRANCH_WIDTH)),
                  pl.BlockSpec((seq, LANES), lambda b, i: (b, P_OFF['a_ik'] // LANES)),
                  pl.BlockSpec((seq, 2 * LANES), lambda b, i: (b, P_OFF['a_k'] // (2 * LANES))),
                  pl.BlockSpec((1, seq // ck, 2 * LANES, ck), lambda b, i: (b, 0, 0, 0)),
                  pl.BlockSpec((1, IDX_HEADS, tq), lambda b, i: (b, 0, i))],
        out_specs=pl.BlockSpec((tq, BRANCH_WIDTH), lambda b, i: (b * nq + i, 0)),
        out_shape=jax.ShapeDtypeStruct((batch * seq, BRANCH_WIDTH), BF16),
        scratch_shapes=[pltpu.VMEM((seq, tq), jnp.int32), pltpu.VMEM((8, tq), jnp.int32),
                        pltpu.VMEM((A_HEADS, tq), F32), pltpu.VMEM((A_HEADS, tq), F32),
                        pltpu.VMEM((A_HEADS * A_DIM, tq), F32)],
        compiler_params=_params(("parallel", "arbitrary")),
        name="sparse_attention",
    )(p_arr, p_arr, p_arr, p_arr, p_arr, vt, wt)


def _swap_halves(w):
    half = w.shape[-1] // 2
    return jnp.concatenate([-w[..., half:], w[..., :half]], axis=-1)


def _prepare_in_proj(w_in):
    sizes = [n for _, n in IN_SEGMENTS]
    offs = np.concatenate([[0], np.cumsum(sizes)])
    seg = {name: w_in[:, int(offs[k]):int(offs[k + 1])] for k, (name, _) in enumerate(IN_SEGMENTS)}
    cols = []
    for name, _ in P_LAYOUT:
        if name == 'a_ik':
            cols += [seg['a_ik'], seg['a_ik']]
        elif name == 'd_kr':
            cols += [seg['d_kr'], _swap_halves(seg['d_kr'])]
        elif name == 'a_q':
            cols.append(seg['a_q'] * (A_DIM ** -0.5 * LOG2E))
        elif name == 'b_q':
            cols.append(seg['b_q'] * (B_DIM ** -0.5 * LOG2E))
        else:
            cols.append(seg[name])
    w_main = jnp.concatenate(cols, axis=1).astype(BF16)
    pad = jnp.zeros((w_in.shape[0], LANES - IDX_HEADS - B_HEADS), w_in.dtype)
    w_small = jnp.concatenate([seg['a_iw'], seg['b_f'], pad], axis=1).astype(BF16)
    return w_main, w_small


def _prepare_latent(dq_up, dkv_up):
    q = dq_up.reshape(D_Q_LORA, D_HEADS, D_NOPE + D_ROPE) * ((D_NOPE + D_ROPE) ** -0.5 * LOG2E)
    rope = q[..., D_NOPE:]
    q_up = jnp.concatenate([q[..., :D_NOPE], rope, _swap_halves(rope)], axis=-1)
    kv = dkv_up.reshape(D_KV_LORA, D_HEADS, D_NOPE + D_VDIM)
    return (q_up.reshape(D_Q_LORA, -1).astype(BF16),
            kv[..., :D_NOPE].reshape(D_KV_LORA, -1).astype(BF16),
            kv[..., D_NOPE:].reshape(D_KV_LORA, -1).T.astype(BF16))


def _rope_table(positions):
    half = D_ROPE // 2
    inv_freq = ROPE_THETA ** (-jnp.arange(half, dtype=F32) / half)
    ang = positions.astype(F32)[..., None] * inv_freq
    cos, sin = jnp.cos(ang), jnp.sin(ang)
    tab = jnp.concatenate([cos, cos, sin, sin], axis=-1)
    return tab.reshape(-1, 4 * half)


def _layer(x, xb, tab, batch, seq, w_in, w_gate, w_branch, w_out, dq_gain, dq_up, dkv_gain, dkv_up,
           f_bias, sinks, ln_gain, ln_bias, alpha):
    w_main, w_small = _prepare_in_proj(w_in)
    proj = _matmul(xb, w_main, BF16)
    small = _matmul(xb, w_small, F32, tn=LANES)

    ck = min(512, seq)
    wt = small[:, :IDX_HEADS].reshape(batch, seq, IDX_HEADS).transpose(0, 2, 1)
    a_v = proj[:, P_OFF['a_v']:P_OFF['a_v'] + 2 * LANES]
    vt = a_v.reshape(batch, seq // ck, ck, 2 * LANES).transpose(0, 1, 3, 2)
    o_a = _sparse_attention(proj, vt, wt, batch, seq)

    f_rows = small[:, IDX_HEADS:IDX_HEADS + B_HEADS].reshape(batch, seq, B_HEADS).transpose(0, 2, 1)
    cum = _forget_cumsum(f_rows.reshape(batch * B_HEADS, seq),
                         jnp.tile(f_bias.astype(F32), batch).reshape(batch * B_HEADS, 1))
    cum = cum.reshape(batch, B_HEADS, seq).transpose(0, 2, 1).reshape(batch * seq, B_HEADS)
    cum = jnp.pad(cum, ((0, 0), (0, LANES - B_HEADS)))
    vt_b = proj[:, P_OFF['b_v']:P_OFF['b_v'] + BRANCH_WIDTH].T
    o_b = _flash_attention(proj, P_OFF['b_q'] // BRANCH_WIDTH, proj, P_OFF['b_k'] // BRANCH_WIDTH, vt_b,
                           proj, P_OFF['b_z'] // BRANCH_WIDTH, batch=batch, seq=seq, heads=B_HEADS,
                           dqk=B_DIM, dv=B_DIM, cum=cum)

    o_c = _sliding_window(proj, sinks, batch, seq)

    q_up, k_up, v_up_t = _prepare_latent(dq_up, dkv_up)
    q_d, k_d, vt_d = _latent_prep(proj, tab, dq_gain, dkv_gain, q_up, k_up, v_up_t)
    o_d = _flash_attention(q_d, 0, k_d, 0, vt_d, proj, P_OFF['d_z'] // BRANCH_WIDTH, batch=batch, seq=seq,
                           heads=D_HEADS, dqk=2 * LANES, dv=D_VDIM)

    merged = _gated_merge(xb, w_gate.astype(BF16), (o_a, o_b, o_c, o_d), w_branch.astype(BF16))
    y = _matmul(merged, w_out.astype(BF16), F32)
    return _residual_layernorm(x, y, ln_gain, ln_bias, alpha)


def kernel(x, positions, w_in, w_gate, w_branch, w_out, dq_gain, dq_up, dkv_gain, dkv_up, f_bias, sinks,
           ln_gain, ln_bias):
    batch, seq, d = x.shape
    depth = w_in.shape[0]
    alpha = (2 * depth) ** 0.25
    tab = _rope_table(positions)
    xf = x.reshape(batch * seq, d)
    xb = xf.astype(BF16)
    for l in range(depth):
        xf, xb = _layer(xf, xb, tab, batch, seq, w_in[l], w_gate[l], w_branch[l], w_out[l], dq_gain[l],
                        dq_up[l], dkv_gain[l], dkv_up[l], f_bias[l], sinks[l], ln_gain[l], ln_bias[l], alpha)
    return xf.reshape(batch, seq, d)
```

```python
import functools

import numpy as np
import jax
import jax.numpy as jnp
from jax import lax
from jax.experimental import pallas as pl
from jax.experimental.pallas import tpu as pltpu

BRANCH_WIDTH = 1024
A_HEADS, A_KV_HEADS, A_DIM = 8, 2, 128
IDX_HEADS, IDX_DIM, IDX_TOPK_MAX = 16, 64, 256
B_HEADS, B_DIM = 8, 128
C_HEADS, C_KV_HEADS, C_DIM, WINDOW = 16, 2, 64, 128
D_HEADS, D_Q_LORA, D_KV_LORA, D_NOPE, D_ROPE, D_VDIM = 8, 768, 256, 128, 64, 128
ROPE_THETA = 10000.0
RMS_EPS = 1e-6
LN_EPS = 1e-5

IN_SEGMENTS = (
    ('a_q', 1024), ('a_k', 256), ('a_v', 256), ('a_iq', 1024), ('a_ik', 64), ('a_iw', 16), ('a_z', 1024),
    ('b_q', 1024), ('b_k', 1024), ('b_v', 1024), ('b_f', 8), ('b_z', 1024),
    ('c_q', 1024), ('c_k', 128), ('c_v', 128), ('c_z', 1024),
    ('d_cq', 768), ('d_ckv', 256), ('d_kr', 64), ('d_z', 1024),
)

P_LAYOUT = (
    ('d_cq', 768), ('d_ckv', 256), ('a_q', 1024), ('a_iq', 1024), ('a_z', 1024),
    ('b_q', 1024), ('b_k', 1024), ('b_v', 1024), ('b_z', 1024), ('c_q', 1024), ('c_z', 1024),
    ('d_z', 1024), ('a_k', 256), ('a_v', 256), ('a_ik', 128), ('d_kr', 128), ('c_k', 128), ('c_v', 128),
)
P_OFF = {}
_o = 0
for _n, _w in P_LAYOUT:
    assert _o % _w == 0
    P_OFF[_n] = _o
    _o += _w
P_WIDTH = _o

LANES = 128
SUBLANES = 8
NEG = -1e30
MASKED = -3e38
LOG2E = 1.4426950408889634
INT_MIN = -2 ** 31
VMEM_LIMIT = 56 * 1024 * 1024

F32 = jnp.float32
BF16 = jnp.bfloat16


def _params(sem, vmem=VMEM_LIMIT):
    return pltpu.CompilerParams(dimension_semantics=sem, vmem_limit_bytes=vmem)


def _dot(a, b):
    return jnp.dot(a, b, preferred_element_type=F32)


def _dot_nt(a, b):
    return lax.dot_general(a, b, (((1,), (1,)), ((), ())), preferred_element_type=F32)


def _alibi_slopes(n_heads):
    return [float(np.float32(2.0 ** (-8.0 * (h + 1) / n_heads))) for h in range(n_heads)]


def _mm_kernel(x_ref, w_ref, o_ref):
    o_ref[...] = _dot(x_ref[...], w_ref[...]).astype(o_ref.dtype)


def _mm_layer_kernel(x_ref, w_ref, o_ref):
    o_ref[...] = _dot(x_ref[...], w_ref[0]).astype(o_ref.dtype)


def _matmul(x, w, out_dtype, layer=None, tm=1024, tn=1024):
    m, k = x.shape
    n = w.shape[-1]
    tm, tn = min(tm, m), min(tn, n)
    if layer is None:
        body, w_spec = _mm_kernel, pl.BlockSpec((k, tn), lambda i, j: (0, j))
    else:
        body, w_spec = _mm_layer_kernel, pl.BlockSpec((1, k, tn), lambda i, j: (layer, 0, j))
    return pl.pallas_call(
        body,
        grid=(m // tm, n // tn),
        in_specs=[pl.BlockSpec((tm, k), lambda i, j: (i, 0)), w_spec],
        out_specs=pl.BlockSpec((tm, tn), lambda i, j: (i, j)),
        out_shape=jax.ShapeDtypeStruct((m, n), out_dtype),
        compiler_params=_params(("parallel", "parallel")),
        name="dense_matmul",
    )(x, w)


def _cast_kernel(x_ref, o_ref):
    o_ref[...] = x_ref[...].astype(o_ref.dtype)


def _cast_bf16(x, tm=512):
    t, d = x.shape
    tm = min(tm, t)
    row = pl.BlockSpec((tm, d), lambda i: (i, 0))
    return pl.pallas_call(
        _cast_kernel, grid=(t // tm,), in_specs=[row], out_specs=row,
        out_shape=jax.ShapeDtypeStruct((t, d), BF16),
        compiler_params=_params(("parallel",)),
        name="cast_bf16",
    )(x)


def _merge_kernel(x_ref, wg_ref, b0_ref, b1_ref, b2_ref, b3_ref, wb_ref, o_ref, acc_ref, *, sub):
    n = pl.program_id(2)

    @pl.when((pl.program_id(0) == 0) & (pl.program_id(1) == 0) & (n == 0))
    def _():
        acc_ref[...] = jnp.zeros(acc_ref.shape, F32)

    branch = jnp.where(n == 0, b0_ref[...], jnp.where(n == 1, b1_ref[...],
                                                      jnp.where(n == 2, b2_ref[...], b3_ref[...])))
    x = x_ref[...]
    for c in range(o_ref.shape[1] // sub):
        cols = slice(c * sub, (c + 1) * sub)
        gate = jax.nn.sigmoid(_dot(x, wg_ref[0, 0, :, cols]))
        contrib = gate * _dot(branch, wb_ref[0, 0, :, cols])
        acc = jnp.where(n == 0, 0.0, acc_ref[:, cols]) + contrib
        acc_ref[:, cols] = acc
        o_ref[:, cols] = acc.astype(o_ref.dtype)


def _gated_merge(xb, wg, branches, wb, layer, tm=512, tn=1024, sub=256):
    t, d = xb.shape
    tm, tn = min(tm, t), min(tn, d)
    sub = min(sub, tn)
    bw = branches[0].shape[1]
    bspec = pl.BlockSpec((tm, bw), lambda i, j, n: (i, 0))
    return pl.pallas_call(
        functools.partial(_merge_kernel, sub=sub),
        grid=(t // tm, d // tn, 4),
        in_specs=[pl.BlockSpec((tm, d), lambda i, j, n: (i, 0)),
                  pl.BlockSpec((1, 1, d, tn), lambda i, j, n: (layer, n, 0, j)),
                  bspec, bspec, bspec, bspec,
                  pl.BlockSpec((1, 1, bw, tn), lambda i, j, n: (layer, n, 0, j))],
        out_specs=pl.BlockSpec((tm, tn), lambda i, j, n: (i, j)),
        out_shape=jax.ShapeDtypeStruct((t, d), BF16),
        scratch_shapes=[pltpu.VMEM((tm, tn), F32)],
        compiler_params=_params(("arbitrary", "arbitrary", "arbitrary")),
        name="gated_merge",
    )(xb, wg, *branches, wb)


def _ln_kernel(x_ref, y_ref, g_ref, b_ref, o_ref, ob_ref, *, alpha):
    r = alpha * x_ref[...] + y_ref[...]
    mu = jnp.mean(r, axis=-1, keepdims=True)
    c = r - mu
    var = jnp.mean(c * c, axis=-1, keepdims=True)
    out = c * lax.rsqrt(var + LN_EPS) * g_ref[...] + b_ref[...]
    o_ref[...] = out
    ob_ref[...] = out.astype(BF16)


def _residual_layernorm(x, y, gain, bias, alpha, tm=256):
    t, d = x.shape
    tm = min(tm, t)
    row = pl.BlockSpec((tm, d), lambda i: (i, 0))
    vec = pl.BlockSpec((1, d), lambda i: (0, 0))
    return pl.pallas_call(
        functools.partial(_ln_kernel, alpha=alpha),
        grid=(t // tm,),
        in_specs=[row, row, vec, vec],
        out_specs=[row, row],
        out_shape=[jax.ShapeDtypeStruct((t, d), F32), jax.ShapeDtypeStruct((t, d), BF16)],
        compiler_params=_params(("parallel",)),
        name="residual_layernorm",
    )(x, y, gain.reshape(1, d), bias.reshape(1, d))


def _forget_cumsum_kernel(f_ref, bias_ref, c_ref):
    rows, s = f_ref.shape
    lane = lax.broadcasted_iota(jnp.int32, (rows, LANES), 1)
    carry = jnp.zeros((rows, 1), F32)
    for c in range(s // LANES):
        x = jax.nn.log_sigmoid(f_ref[:, c * LANES:(c + 1) * LANES] + bias_ref[...])
        shift = 1
        while shift < LANES:
            x = x + jnp.where(lane >= shift, pltpu.roll(x, shift, 1), 0.0)
            shift *= 2
        x = x + carry
        c_ref[:, c * LANES:(c + 1) * LANES] = x * LOG2E
        carry = x[:, LANES - 1:LANES]


def _forget_cumsum(f_rows, bias_rows):
    rows, s = f_rows.shape
    full = pl.BlockSpec((rows, s), lambda: (0, 0))
    return pl.pallas_call(
        _forget_cumsum_kernel,
        in_specs=[full, pl.BlockSpec((rows, 1), lambda: (0, 0))],
        out_specs=full,
        out_shape=jax.ShapeDtypeStruct((rows, s), F32),
        name="forget_cumsum",
    )(f_rows, bias_rows)


def _tree(parts, op):
    parts = list(parts)
    while len(parts) > 1:
        nxt = [op(parts[a], parts[a + 1]) for a in range(0, len(parts) - 1, 2)]
        if len(parts) % 2:
            nxt.append(parts[-1])
        parts = nxt
    return parts[0]


def _col_reduce(x, op, reduce_fn):
    rows = x.shape[0]
    part = _tree([x[r:r + SUBLANES, :] for r in range(0, rows, SUBLANES)], op)
    return reduce_fn(part, axis=0, keepdims=True)


def _flash_kernel(i_tab, j_tab, *refs, fox, heads, dqk, dv, tile):
    if fox:
        q_ref, k_ref, vt_ref, ck_ref, z_ref, o_ref, m_sc, l_sc, acc_sc = refs
    else:
        q_ref, k_ref, vt_ref, z_ref, o_ref, m_sc, l_sc, acc_sc = refs
    pair = pl.program_id(1)
    i = i_tab[pair]
    j = j_tab[pair]

    @pl.when(j == 0)
    def _():
        m_sc[...] = jnp.full(m_sc.shape, NEG, F32)
        l_sc[...] = jnp.zeros(l_sc.shape, F32)
        acc_sc[...] = jnp.zeros(acc_sc.shape, F32)

    def step(masked):
        if masked:
            causal = (lax.broadcasted_iota(jnp.int32, (tile, tile), 0)
                      <= lax.broadcasted_iota(jnp.int32, (tile, tile), 1))
        def logits(h):
            return _dot_nt(k_ref[:, h * dqk:(h + 1) * dqk], q_ref[:, h * dqk:(h + 1) * dqk])

        s_next = logits(0)
        for h in range(heads):
            s = s_next
            if h + 1 < heads:
                s_next = logits(h + 1)
            if fox:
                s = s - ck_ref[:, h:h + 1]
            if masked:
                s = jnp.where(causal, s, NEG)
            m_prev = m_sc[h:h + 1, :]
            m_new = jnp.maximum(m_prev, _col_reduce(s, jnp.maximum, jnp.max))
            alpha = jnp.exp2(m_prev - m_new)
            p = jnp.exp2(s - m_new)
            l_sc[h:h + 1, :] = alpha * l_sc[h:h + 1, :] + _col_reduce(p, jnp.add, jnp.sum)
            acc_sc[h * dv:(h + 1) * dv, :] = (alpha * acc_sc[h * dv:(h + 1) * dv, :]
                                              + _dot(vt_ref[h * dv:(h + 1) * dv, :], p.astype(BF16)))
            m_sc[h:h + 1, :] = m_new

    @pl.when(j < i)
    def _():
        step(False)

    @pl.when(j == i)
    def _():
        step(True)
        for h in range(heads):
            out_t = acc_sc[h * dv:(h + 1) * dv, :] / l_sc[h:h + 1, :]
            z = z_ref[:, h * dv:(h + 1) * dv].astype(F32)
            o_ref[:, h * dv:(h + 1) * dv] = (out_t.T * jax.nn.silu(z)).astype(o_ref.dtype)


def _flash_attention(q_arr, q_blk, k_arr, k_blk, vt_arr, z_arr, z_blk, *, batch, seq, heads,
                     dqk, dv, cum=None, tile=512):
    tile = min(tile, seq)
    nq = seq // tile
    fox = cum is not None
    pairs = [(i, j) for i in range(nq) for j in range(i + 1)]
    i_tab = jnp.asarray([p[0] for p in pairs], jnp.int32)
    j_tab = jnp.asarray([p[1] for p in pairs], jnp.int32)
    in_specs = [
        pl.BlockSpec((tile, heads * dqk), lambda b, p, it, jt: (b * nq + it[p], q_blk)),
        pl.BlockSpec((tile, heads * dqk), lambda b, p, it, jt: (b * nq + jt[p], k_blk)),
        pl.BlockSpec((heads * dv, tile), lambda b, p, it, jt: (0, b * nq + jt[p])),
    ]
    args = [q_arr, k_arr, vt_arr]
    if fox:
        in_specs.append(pl.BlockSpec((tile, LANES), lambda b, p, it, jt: (b * nq + jt[p], 0)))
        args.append(cum)
    in_specs.append(pl.BlockSpec((tile, heads * dv), lambda b, p, it, jt: (b * nq + it[p], z_blk)))
    args.append(z_arr)
    return pl.pallas_call(
        functools.partial(_flash_kernel, fox=fox, heads=heads, dqk=dqk, dv=dv, tile=tile),
        grid_spec=pltpu.PrefetchScalarGridSpec(
            num_scalar_prefetch=2,
            grid=(batch, len(pairs)),
            in_specs=in_specs,
            out_specs=pl.BlockSpec((tile, heads * dv), lambda b, p, it, jt: (b * nq + it[p], 0)),
            scratch_shapes=[pltpu.VMEM((heads, tile), F32), pltpu.VMEM((heads, tile), F32),
                            pltpu.VMEM((heads * dv, tile), F32)]),
        out_shape=jax.ShapeDtypeStruct((batch * seq, heads * dv), BF16),
        compiler_params=_params(("parallel", "arbitrary")),
        name="flash_fox" if fox else "flash_latent",
    )(i_tab, j_tab, *args)


def _swa_kernel(q_ref, kvc_ref, kvp_ref, z_ref, sink_ref, o_ref, *, slopes, scale):
    i = pl.program_id(1)
    w = WINDOW
    kv = jnp.concatenate([kvp_ref[...], kvc_ref[...]], axis=0).astype(F32)
    lane = lax.broadcasted_iota(jnp.int32, (2 * w, LANES), 1)
    low = lane < C_DIM

    def variants(t):
        g0_lo = jnp.where(low, t, 0.0)
        g1_hi = jnp.where(low, 0.0, t)
        g0_hi = pltpu.roll(g0_lo, C_DIM, 1)
        g1_lo = pltpu.roll(g1_hi, C_DIM, 1)
        return ((g0_lo.astype(BF16), g0_hi.astype(BF16)), (g1_lo.astype(BF16), g1_hi.astype(BF16)))

    k_var = variants(kv[:, :LANES])
    v_var = variants(kv[:, LANES:])
    qi = lax.broadcasted_iota(jnp.int32, (w, 2 * w), 0)
    kj = lax.broadcasted_iota(jnp.int32, (w, 2 * w), 1)
    dist = qi + w - kj
    first_key = jnp.where(i > 0, 0, w)
    mask = (dist >= 0) & (dist < w) & (kj >= first_key)
    distf = dist.astype(F32)
    for pair in range(C_HEADS // 2):
        g = (2 * pair) // (C_HEADS // C_KV_HEADS)
        qp = q_ref[:, pair * LANES:(pair + 1) * LANES]
        out = None
        for half in range(2):
            h = 2 * pair + half
            s = _dot_nt(qp, k_var[g][half]) * scale - slopes[h] * distf
            s = jnp.where(mask, s, NEG)
            sink = sink_ref[h]
            m = jnp.maximum(jnp.max(s, axis=1, keepdims=True), sink)
            p = jnp.exp(s - m)
            denom = jnp.sum(p, axis=1, keepdims=True) + jnp.exp(sink - m)
            contrib = _dot((p / denom).astype(BF16), v_var[g][half])
            out = contrib if out is None else out + contrib
        z = z_ref[:, pair * LANES:(pair + 1) * LANES].astype(F32)
        o_ref[:, pair * LANES:(pair + 1) * LANES] = (out * jax.nn.silu(z)).astype(o_ref.dtype)


def _sliding_window(p_arr, sinks, batch, seq):
    w = WINDOW
    nb = seq // w
    kv_blk = P_OFF['c_k'] // (2 * LANES)
    return pl.pallas_call(
        functools.partial(_swa_kernel, slopes=_alibi_slopes(C_HEADS), scale=C_DIM ** -0.5),
        grid=(batch, nb),
        in_specs=[pl.BlockSpec((w, BRANCH_WIDTH), lambda b, i: (b * nb + i, P_OFF['c_q'] // BRANCH_WIDTH)),
                  pl.BlockSpec((w, 2 * LANES), lambda b, i: (b * nb + i, kv_blk)),
                  pl.BlockSpec((w, 2 * LANES), lambda b, i: (b * nb + jnp.maximum(i - 1, 0), kv_blk)),
                  pl.BlockSpec((w, BRANCH_WIDTH), lambda b, i: (b * nb + i, P_OFF['c_z'] // BRANCH_WIDTH)),
                  pl.BlockSpec(memory_space=pltpu.SMEM)],
        out_specs=pl.BlockSpec((w, BRANCH_WIDTH), lambda b, i: (b * nb + i, 0)),
        out_shape=jax.ShapeDtypeStruct((batch * seq, BRANCH_WIDTH), BF16),
        compiler_params=_params(("parallel", "parallel")),
        name="sliding_window",
    )(p_arr, p_arr, p_arr, p_arr, sinks.astype(F32))


def _latent_prep_kernel(cq_ref, ckv_ref, kr_ref, tab_ref, qg_ref, kvg_ref, qup_ref, kup_ref, vupt_ref,
                        q_ref, k_ref, vt_ref):
    def rms(x, g):
        return x * lax.rsqrt(jnp.mean(x * x, axis=-1, keepdims=True) + RMS_EPS) * g

    tab = tab_ref[...]
    lane = lax.broadcasted_iota(jnp.int32, tab.shape, 1)

    def rotate(t):
        r = t * tab
        return r + pltpu.roll(r, D_ROPE, 1)

    q = _dot(rms(cq_ref[...].astype(F32), qg_ref[...]).astype(BF16), qup_ref[...])
    ckv = rms(ckv_ref[...].astype(F32), kvg_ref[...]).astype(BF16)
    k_nope = _dot(ckv, kup_ref[...])
    vt_ref[...] = _dot_nt(vupt_ref[...], ckv).astype(vt_ref.dtype)
    k_rot = jnp.where(lane < D_ROPE, rotate(kr_ref[...].astype(F32)), 0.0).astype(k_ref.dtype)
    hw = 2 * LANES
    for h in range(D_HEADS):
        q_ref[:, h * hw:h * hw + LANES] = q[:, h * hw:h * hw + LANES].astype(q_ref.dtype)
        q_ref[:, h * hw + LANES:(h + 1) * hw] = rotate(q[:, h * hw + LANES:(h + 1) * hw]).astype(q_ref.dtype)
        k_ref[:, h * hw:h * hw + LANES] = k_nope[:, h * LANES:(h + 1) * LANES].astype(k_ref.dtype)
        k_ref[:, h * hw + LANES:(h + 1) * hw] = k_rot


def _latent_prep(p_arr, tab, q_gain, kv_gain, q_up, k_up, v_up_t, tm=512):
    t = p_arr.shape[0]
    tm = min(tm, t)
    hw = 2 * LANES

    def const(shape):
        return pl.BlockSpec(shape, lambda i: (0, 0))

    return pl.pallas_call(
        _latent_prep_kernel,
        grid=(t // tm,),
        in_specs=[pl.BlockSpec((tm, D_Q_LORA), lambda i: (i, P_OFF['d_cq'] // D_Q_LORA)),
                  pl.BlockSpec((tm, D_KV_LORA), lambda i: (i, P_OFF['d_ckv'] // D_KV_LORA)),
                  pl.BlockSpec((tm, LANES), lambda i: (i, P_OFF['d_kr'] // LANES)),
                  pl.BlockSpec((tm, LANES), lambda i: (i, 0)),
                  const((1, D_Q_LORA)), const((1, D_KV_LORA)),
                  const((D_Q_LORA, D_HEADS * hw)), const((D_KV_LORA, D_HEADS * LANES)),
                  const((D_HEADS * D_VDIM, D_KV_LORA))],
        out_specs=[pl.BlockSpec((tm, D_HEADS * hw), lambda i: (i, 0)),
                   pl.BlockSpec((tm, D_HEADS * hw), lambda i: (i, 0)),
                   pl.BlockSpec((D_HEADS * D_VDIM, tm), lambda i: (0, i))],
        out_shape=[jax.ShapeDtypeStruct((t, D_HEADS * hw), BF16),
                   jax.ShapeDtypeStruct((t, D_HEADS * hw), BF16),
                   jax.ShapeDtypeStruct((D_HEADS * D_VDIM, t), BF16)],
        compiler_params=_params(("parallel",)),
        name="latent_prep",
    )(p_arr, p_arr, p_arr, tab, q_gain.reshape(1, -1), kv_gain.reshape(1, -1), q_up, k_up, v_up_t)


def _sparse_kernel(q_ref, iq_ref, z_ref, ik_ref, k_ref, vt_ref, wt_ref, o_ref,
                   keys_ref, jcut_ref, m_ref, l_ref, acc_ref, *, topk, ck, slopes, idx_bits):
    i = pl.program_id(1)
    tq = LANES
    nch = ((i + 1) * tq + ck - 1) // ck
    q_pos = i * tq + lax.broadcasted_iota(jnp.int32, (ck, tq), 1)
    s_iota = lax.broadcasted_iota(jnp.int32, (ck, tq), 0)
    lane = lax.broadcasted_iota(jnp.int32, (ck, LANES), 1)
    low = lane < IDX_DIM
    w_all = wt_ref[0] * (IDX_HEADS ** -0.5 * IDX_DIM ** -0.5)

    def score_chunk(c, carry):
        start = pl.multiple_of(c * ck, ck)
        ikc = ik_ref[pl.ds(start, ck), :]
        ik_lo = jnp.where(low, ikc, jnp.zeros_like(ikc))
        ik_hi = jnp.where(low, jnp.zeros_like(ikc), ikc)
        acc = jnp.zeros((ck, tq), F32)
        for pair in range(IDX_HEADS // 2):
            iqp = iq_ref[:, pair * LANES:(pair + 1) * LANES]
            acc = acc + w_all[2 * pair:2 * pair + 1, :] * jnp.maximum(_dot_nt(ik_lo, iqp), 0.0)
            acc = acc + w_all[2 * pair + 1:2 * pair + 2, :] * jnp.maximum(_dot_nt(ik_hi, iqp), 0.0)
        acc = jnp.where(acc == 0.0, 0.0, acc)
        bits = lax.bitcast_convert_type(acc, jnp.int32)
        key = bits ^ ((bits >> 31) & 0x7FFFFFFF)
        key = jnp.where(start + s_iota <= q_pos, key, INT_MIN)
        keys_ref[pl.ds(start, ck), :] = key
        return carry

    lax.fori_loop(0, nch, score_chunk, 0)

    def count(pred_fn):
        def body(c, cnt):
            start = pl.multiple_of(c * ck, ck)
            hit = jnp.where(pred_fn(keys_ref[pl.ds(start, ck), :], start + s_iota), 1.0, 0.0)
            return cnt + _tree([hit[r:r + SUBLANES, :] for r in range(0, ck, SUBLANES)], jnp.add)
        part = lax.fori_loop(0, nch, body, jnp.zeros((SUBLANES, tq), F32))
        return jnp.sum(part, axis=0, keepdims=True)

    kf = float(topk)

    def thr_step(it, thr):
        cand = thr ^ (jnp.int32(1) << (31 - it))
        cnt = count(lambda kc, pos: kc >= cand)
        return jnp.where(cnt >= kf, cand, thr)

    thr = lax.fori_loop(0, 32, thr_step, jnp.full((1, tq), INT_MIN, jnp.int32))

    n_ge = count(lambda kc, pos: kc >= thr)
    n_gt = count(lambda kc, pos: kc > thr)
    need = kf - n_gt
    tied = (n_ge > kf) & (thr != INT_MIN)
    jcut_ref[...] = jnp.full(jcut_ref.shape, 2 ** 30, jnp.int32)

    @pl.when(jnp.max(tied.astype(F32)) > 0.0)
    def _():
        def cut_step(it, x):
            cand = x + (jnp.int32(1) << (idx_bits - 1 - it))
            cnt = count(lambda kc, pos: (kc == thr) & (pos < cand))
            return jnp.where(cnt < need, cand, x)
        x = lax.fori_loop(0, idx_bits, cut_step, jnp.zeros((1, tq), jnp.int32))
        jcut_ref[...] = jnp.broadcast_to(jnp.where(tied, x, 2 ** 30), jcut_ref.shape)

    jcut = jcut_ref[0:1, :]

    m_ref[...] = jnp.full(m_ref.shape, NEG, F32)
    l_ref[...] = jnp.zeros(l_ref.shape, F32)
    acc_ref[...] = jnp.zeros(acc_ref.shape, F32)
    rep = A_HEADS // A_KV_HEADS
    q_groups = [jnp.concatenate([q_ref[:, (g * rep + r) * A_DIM:(g * rep + r + 1) * A_DIM] for r in range(rep)],
                                axis=0) for g in range(A_KV_HEADS)]

    def attend_chunk(c, carry):
        start = pl.multiple_of(c * ck, ck)
        kc = keys_ref[pl.ds(start, ck), :]
        pos = start + s_iota
        sel = ((kc > thr) | ((kc == thr) & (pos <= jcut))) & (pos <= q_pos)
        dist = (q_pos - pos).astype(F32)
        logits_g = [_dot_nt(k_ref[pl.ds(start, ck), g * A_DIM:(g + 1) * A_DIM], q_groups[g])
                    for g in range(A_KV_HEADS)]
        for g in range(A_KV_HEADS):
            vtg = vt_ref[0, c, g * A_DIM:(g + 1) * A_DIM, :]
            for r in range(rep):
                h = g * rep + r
                s = logits_g[g][:, r * tq:(r + 1) * tq] - (slopes[h] * LOG2E) * dist
                s = jnp.where(sel, s, MASKED)
                m_prev = m_ref[h:h + 1, :]
                m_new = jnp.maximum(m_prev, _col_reduce(s, jnp.maximum, jnp.max))
                alpha = jnp.exp2(m_prev - m_new)
                p = jnp.exp2(s - m_new)
                l_ref[h:h + 1, :] = alpha * l_ref[h:h + 1, :] + _col_reduce(p, jnp.add, jnp.sum)
                acc_ref[h * A_DIM:(h + 1) * A_DIM, :] = (alpha * acc_ref[h * A_DIM:(h + 1) * A_DIM, :]
                                                         + _dot(vtg, p.astype(BF16)))
                m_ref[h:h + 1, :] = m_new
        return carry

    lax.fori_loop(0, nch, attend_chunk, 0)

    for h in range(A_HEADS):
        out_t = acc_ref[h * A_DIM:(h + 1) * A_DIM, :] / l_ref[h:h + 1, :]
        z = z_ref[:, h * A_DIM:(h + 1) * A_DIM].astype(F32)
        o_ref[:, h * A_DIM:(h + 1) * A_DIM] = (out_t.T * jax.nn.silu(z)).astype(o_ref.dtype)


def _sparse_attention(p_arr, vt, wt, batch, seq):
    tq = LANES
    nq = seq // tq
    ck = min(512, seq)
    topk = min(IDX_TOPK_MAX, seq // 4)
    idx_bits = int(seq).bit_length()
    return pl.pallas_call(
        functools.partial(_sparse_kernel, topk=topk, ck=ck, slopes=_alibi_slopes(A_HEADS), idx_bits=idx_bits),
        grid=(batch, nq),
        in_specs=[pl.BlockSpec((tq, BRANCH_WIDTH), lambda b, i: (b * nq + i, P_OFF['a_q'] // BRANCH_WIDTH)),
                  pl.BlockSpec((tq, BRANCH_WIDTH), lambda b, i: (b * nq + i, P_OFF['a_iq'] // BRANCH_WIDTH)),
                  pl.BlockSpec((tq, BRANCH_WIDTH), lambda b, i: (b * nq + i, P_OFF['a_z'] // BRANCH_WIDTH)),
                  pl.BlockSpec((seq, LANES), lambda b, i: (b, P_OFF['a_ik'] // LANES)),
                  pl.BlockSpec((seq, 2 * LANES), lambda b, i: (b, P_OFF['a_k'] // (2 * LANES))),
                  pl.BlockSpec((1, seq // ck, 2 * LANES, ck), lambda b, i: (b, 0, 0, 0)),
                  pl.BlockSpec((1, IDX_HEADS, tq), lambda b, i: (b, 0, i))],
        out_specs=pl.BlockSpec((tq, BRANCH_WIDTH), lambda b, i: (b * nq + i, 0)),
        out_shape=jax.ShapeDtypeStruct((batch * seq, BRANCH_WIDTH), BF16),
        scratch_shapes=[pltpu.VMEM((seq, tq), jnp.int32), pltpu.VMEM((8, tq), jnp.int32),
                        pltpu.VMEM((A_HEADS, tq), F32), pltpu.VMEM((A_HEADS, tq), F32),
                        pltpu.VMEM((A_HEADS * A_DIM, tq), F32)],
        compiler_params=_params(("parallel", "arbitrary")),
        name="sparse_attention",
    )(p_arr, p_arr, p_arr, p_arr, p_arr, vt, wt)


def _swap_halves(w):
    half = w.shape[-1] // 2
    return jnp.concatenate([-w[..., half:], w[..., :half]], axis=-1)


def _prepare_in_proj(w_in):
    sizes = [n for _, n in IN_SEGMENTS]
    offs = np.concatenate([[0], np.cumsum(sizes)])
    span = {name: (int(offs[k]), int(offs[k + 1])) for k, (name, _) in enumerate(IN_SEGMENTS)}
    w16 = w_in.astype(BF16)
    seg16 = lambda name: w16[:, :, span[name][0]:span[name][1]]
    seg32 = lambda name: w_in[:, :, span[name][0]:span[name][1]]
    cols = []
    for name, _ in P_LAYOUT:
        if name == 'a_ik':
            cols += [seg16('a_ik'), seg16('a_ik')]
        elif name == 'd_kr':
            cols += [seg16('d_kr'), _swap_halves(seg16('d_kr'))]
        elif name == 'a_q':
            cols.append((seg32('a_q') * (A_DIM ** -0.5 * LOG2E)).astype(BF16))
        elif name == 'b_q':
            cols.append((seg32('b_q') * (B_DIM ** -0.5 * LOG2E)).astype(BF16))
        else:
            cols.append(seg16(name))
    w_main = jnp.concatenate(cols, axis=2)
    pad = jnp.zeros(w_in.shape[:2] + (LANES - IDX_HEADS - B_HEADS,), BF16)
    w_small = jnp.concatenate([seg16('a_iw'), seg16('b_f'), pad], axis=2)
    return w_main, w_small


def _prepare_latent(dq_up, dkv_up):
    q = dq_up.reshape(D_Q_LORA, D_HEADS, D_NOPE + D_ROPE) * ((D_NOPE + D_ROPE) ** -0.5 * LOG2E)
    rope = q[..., D_NOPE:]
    q_up = jnp.concatenate([q[..., :D_NOPE], rope, _swap_halves(rope)], axis=-1)
    kv = dkv_up.reshape(D_KV_LORA, D_HEADS, D_NOPE + D_VDIM)
    return (q_up.reshape(D_Q_LORA, -1).astype(BF16),
            kv[..., :D_NOPE].reshape(D_KV_LORA, -1).astype(BF16),
            kv[..., D_NOPE:].reshape(D_KV_LORA, -1).T.astype(BF16))


def _rope_table(positions):
    half = D_ROPE // 2
    inv_freq = ROPE_THETA ** (-jnp.arange(half, dtype=F32) / half)
    ang = positions.astype(F32)[..., None] * inv_freq
    cos, sin = jnp.cos(ang), jnp.sin(ang)
    tab = jnp.concatenate([cos, cos, sin, sin], axis=-1)
    return tab.reshape(-1, 4 * half)


def _layer(layer, x, xb, tab, batch, seq, w_main, w_small, w_gate, w_branch, w_out, dq_gain, dq_up, dkv_gain,
           dkv_up, f_bias, sinks, ln_gain, ln_bias, alpha):
    proj = _matmul(xb, w_main, BF16, layer=layer)
    small = _matmul(xb, w_small, F32, layer=layer, tn=LANES)

    ck = min(512, seq)
    wt = small[:, :IDX_HEADS].reshape(batch, seq, IDX_HEADS).transpose(0, 2, 1)
    a_v = proj[:, P_OFF['a_v']:P_OFF['a_v'] + 2 * LANES]
    vt = a_v.reshape(batch, seq // ck, ck, 2 * LANES).transpose(0, 1, 3, 2)
    o_a = _sparse_attention(proj, vt, wt, batch, seq)

    f_rows = small[:, IDX_HEADS:IDX_HEADS + B_HEADS].reshape(batch, seq, B_HEADS).transpose(0, 2, 1)
    cum = _forget_cumsum(f_rows.reshape(batch * B_HEADS, seq),
                         jnp.tile(f_bias.astype(F32), batch).reshape(batch * B_HEADS, 1))
    cum = cum.reshape(batch, B_HEADS, seq).transpose(0, 2, 1).reshape(batch * seq, B_HEADS)
    cum = jnp.pad(cum, ((0, 0), (0, LANES - B_HEADS)))
    vt_b = proj[:, P_OFF['b_v']:P_OFF['b_v'] + BRANCH_WIDTH].T
    o_b = _flash_attention(proj, P_OFF['b_q'] // BRANCH_WIDTH, proj, P_OFF['b_k'] // BRANCH_WIDTH, vt_b,
                           proj, P_OFF['b_z'] // BRANCH_WIDTH, batch=batch, seq=seq, heads=B_HEADS,
                           dqk=B_DIM, dv=B_DIM, cum=cum)

    o_c = _sliding_window(proj, sinks, batch, seq)

    q_up, k_up, v_up_t = _prepare_latent(dq_up, dkv_up)
    q_d, k_d, vt_d = _latent_prep(proj, tab, dq_gain, dkv_gain, q_up, k_up, v_up_t)
    o_d = _flash_attention(q_d, 0, k_d, 0, vt_d, proj, P_OFF['d_z'] // BRANCH_WIDTH, batch=batch, seq=seq,
                           heads=D_HEADS, dqk=2 * LANES, dv=D_VDIM)

    merged = _gated_merge(xb, w_gate, (o_a, o_b, o_c, o_d), w_branch, layer)
    y = _matmul(merged, w_out, F32, layer=layer)
    return _residual_layernorm(x, y, ln_gain, ln_bias, alpha)


def kernel(x, positions, w_in, w_gate, w_branch, w_out, dq_gain, dq_up, dkv_gain, dkv_up, f_bias, sinks,
           ln_gain, ln_bias):
    batch, seq, d = x.shape
    depth = w_in.shape[0]
    alpha = (2 * depth) ** 0.25
    tab = _rope_table(positions)
    xf = x.reshape(batch * seq, d)
    xb = _cast_bf16(xf)
    w_main, w_small = _prepare_in_proj(w_in)
    w_gate, w_branch, w_out = w_gate.astype(BF16), w_branch.astype(BF16), w_out.astype(BF16)
    for l in range(depth):
        xf, xb = _layer(l, xf, xb, tab, batch, seq, w_main, w_small, w_gate, w_branch, w_out, dq_gain[l],
                        dq_up[l], dkv_gain[l], dkv_up[l], f_bias[l], sinks[l], ln_gain[l], ln_bias[l], alpha)
    return xf.reshape(batch, seq, d)
```

```python
import functools

import numpy as np
import jax
import jax.numpy as jnp
from jax import lax
from jax.experimental import pallas as pl
from jax.experimental.pallas import tpu as pltpu

BRANCH_WIDTH = 1024
A_HEADS, A_KV_HEADS, A_DIM = 8, 2, 128
IDX_HEADS, IDX_DIM, IDX_TOPK_MAX = 16, 64, 256
B_HEADS, B_DIM = 8, 128
C_HEADS, C_KV_HEADS, C_DIM, WINDOW = 16, 2, 64, 128
D_HEADS, D_Q_LORA, D_KV_LORA, D_NOPE, D_ROPE, D_VDIM = 8, 768, 256, 128, 64, 128
ROPE_THETA = 10000.0
RMS_EPS = 1e-6
LN_EPS = 1e-5

IN_SEGMENTS = (
    ('a_q', 1024), ('a_k', 256), ('a_v', 256), ('a_iq', 1024), ('a_ik', 64), ('a_iw', 16), ('a_z', 1024),
    ('b_q', 1024), ('b_k', 1024), ('b_v', 1024), ('b_f', 8), ('b_z', 1024),
    ('c_q', 1024), ('c_k', 128), ('c_v', 128), ('c_z', 1024),
    ('d_cq', 768), ('d_ckv', 256), ('d_kr', 64), ('d_z', 1024),
)

P_LAYOUT = (
    ('d_cq', 768), ('d_ckv', 256), ('a_q', 1024), ('a_iq', 1024), ('a_z', 1024),
    ('b_q', 1024), ('b_k', 1024), ('b_v', 1024), ('b_z', 1024), ('c_q', 1024), ('c_z', 1024),
    ('d_z', 1024), ('a_k', 256), ('a_v', 256), ('a_ik', 128), ('d_kr', 128), ('c_k', 128), ('c_v', 128),
)
P_OFF = {}
_o = 0
for _n, _w in P_LAYOUT:
    assert _o % _w == 0
    P_OFF[_n] = _o
    _o += _w
P_WIDTH = _o

LANES = 128
SUBLANES = 8
NEG = -1e30
MASKED = -3e38
LOG2E = 1.4426950408889634
INT_MIN = -2 ** 31
VMEM_LIMIT = 56 * 1024 * 1024

F32 = jnp.float32
BF16 = jnp.bfloat16


def _params(sem, vmem=VMEM_LIMIT):
    return pltpu.CompilerParams(dimension_semantics=sem, vmem_limit_bytes=vmem)


def _dot(a, b):
    return jnp.dot(a, b, preferred_element_type=F32)


def _dot_nt(a, b):
    return lax.dot_general(a, b, (((1,), (1,)), ((), ())), preferred_element_type=F32)


def _alibi_slopes(n_heads):
    return [float(np.float32(2.0 ** (-8.0 * (h + 1) / n_heads))) for h in range(n_heads)]


def _mm_kernel(x_ref, w_ref, o_ref):
    o_ref[...] = _dot(x_ref[...], w_ref[0]).astype(o_ref.dtype)


def _mm_nt_kernel(x_ref, wt_ref, o_ref):
    o_ref[...] = _dot_nt(x_ref[...], wt_ref[0]).astype(o_ref.dtype)


def _matmul(x, w, layer, out_dtype, transposed=False, tm=1024, tn=1024):
    m, k = x.shape
    n = w.shape[1] if transposed else w.shape[2]
    tm, tn = min(tm, m), min(tn, n)
    if transposed:
        body, w_spec = _mm_nt_kernel, pl.BlockSpec((1, tn, k), lambda i, j: (layer, j, 0))
    else:
        body, w_spec = _mm_kernel, pl.BlockSpec((1, k, tn), lambda i, j: (layer, 0, j))
    return pl.pallas_call(
        body,
        grid=(m // tm, n // tn),
        in_specs=[pl.BlockSpec((tm, k), lambda i, j: (i, 0)), w_spec],
        out_specs=pl.BlockSpec((tm, tn), lambda i, j: (i, j)),
        out_shape=jax.ShapeDtypeStruct((m, n), out_dtype),
        compiler_params=_params(("parallel", "parallel")),
        name="dense_matmul",
    )(x, w)


def _cast_kernel(x_ref, o_ref):
    o_ref[...] = x_ref[...].astype(o_ref.dtype)


def _cast_bf16(x, tm=512):
    t, d = x.shape
    tm = min(tm, t)
    row = pl.BlockSpec((tm, d), lambda i: (i, 0))
    return pl.pallas_call(
        _cast_kernel, grid=(t // tm,), in_specs=[row], out_specs=row,
        out_shape=jax.ShapeDtypeStruct((t, d), BF16),
        compiler_params=_params(("parallel",)),
        name="cast_bf16",
    )(x)


def _merge_kernel(x_ref, wg_ref, b0_ref, b1_ref, b2_ref, b3_ref, wb_ref, o_ref, acc_ref, *, sub):
    n = pl.program_id(2)

    @pl.when((pl.program_id(0) == 0) & (pl.program_id(1) == 0) & (n == 0))
    def _():
        acc_ref[...] = jnp.zeros(acc_ref.shape, F32)

    branch = jnp.where(n == 0, b0_ref[...], jnp.where(n == 1, b1_ref[...],
                                                      jnp.where(n == 2, b2_ref[...], b3_ref[...])))
    x = x_ref[...]
    for c in range(o_ref.shape[1] // sub):
        cols = slice(c * sub, (c + 1) * sub)
        gate = jax.nn.sigmoid(_dot(x, wg_ref[0, 0, :, cols]))
        contrib = gate * _dot(branch, wb_ref[0, 0, :, cols])
        acc = jnp.where(n == 0, 0.0, acc_ref[:, cols]) + contrib
        acc_ref[:, cols] = acc
        o_ref[:, cols] = acc.astype(o_ref.dtype)


def _gated_merge(xb, wg, branches, wb, layer, tm=512, tn=1024, sub=256):
    t, d = xb.shape
    tm, tn = min(tm, t), min(tn, d)
    sub = min(sub, tn)
    bw = branches[0].shape[1]
    bspec = pl.BlockSpec((tm, bw), lambda i, j, n: (i, 0))
    return pl.pallas_call(
        functools.partial(_merge_kernel, sub=sub),
        grid=(t // tm, d // tn, 4),
        in_specs=[pl.BlockSpec((tm, d), lambda i, j, n: (i, 0)),
                  pl.BlockSpec((1, 1, d, tn), lambda i, j, n: (layer, n, 0, j)),
                  bspec, bspec, bspec, bspec,
                  pl.BlockSpec((1, 1, bw, tn), lambda i, j, n: (layer, n, 0, j))],
        out_specs=pl.BlockSpec((tm, tn), lambda i, j, n: (i, j)),
        out_shape=jax.ShapeDtypeStruct((t, d), BF16),
        scratch_shapes=[pltpu.VMEM((tm, tn), F32)],
        compiler_params=_params(("arbitrary", "arbitrary", "arbitrary")),
        name="gated_merge",
    )(xb, wg, *branches, wb)


def _ln_kernel(x_ref, y_ref, g_ref, b_ref, o_ref, ob_ref, *, alpha):
    r = alpha * x_ref[...] + y_ref[...]
    mu = jnp.mean(r, axis=-1, keepdims=True)
    c = r - mu
    var = jnp.mean(c * c, axis=-1, keepdims=True)
    out = c * lax.rsqrt(var + LN_EPS) * g_ref[...] + b_ref[...]
    o_ref[...] = out
    ob_ref[...] = out.astype(BF16)


def _residual_layernorm(x, y, gain, bias, alpha, tm=256):
    t, d = x.shape
    tm = min(tm, t)
    row = pl.BlockSpec((tm, d), lambda i: (i, 0))
    vec = pl.BlockSpec((1, d), lambda i: (0, 0))
    return pl.pallas_call(
        functools.partial(_ln_kernel, alpha=alpha),
        grid=(t // tm,),
        in_specs=[row, row, vec, vec],
        out_specs=[row, row],
        out_shape=[jax.ShapeDtypeStruct((t, d), F32), jax.ShapeDtypeStruct((t, d), BF16)],
        compiler_params=_params(("parallel",)),
        name="residual_layernorm",
    )(x, y, gain.reshape(1, d), bias.reshape(1, d))


def _forget_cumsum_kernel(f_ref, bias_ref, c_ref):
    rows, s = f_ref.shape
    lane = lax.broadcasted_iota(jnp.int32, (rows, LANES), 1)
    carry = jnp.zeros((rows, 1), F32)
    for c in range(s // LANES):
        x = jax.nn.log_sigmoid(f_ref[:, c * LANES:(c + 1) * LANES] + bias_ref[...])
        shift = 1
        while shift < LANES:
            x = x + jnp.where(lane >= shift, pltpu.roll(x, shift, 1), 0.0)
            shift *= 2
        x = x + carry
        c_ref[:, c * LANES:(c + 1) * LANES] = x * LOG2E
        carry = x[:, LANES - 1:LANES]


def _forget_cumsum(f_rows, bias_rows):
    rows, s = f_rows.shape
    full = pl.BlockSpec((rows, s), lambda: (0, 0))
    return pl.pallas_call(
        _forget_cumsum_kernel,
        in_specs=[full, pl.BlockSpec((rows, 1), lambda: (0, 0))],
        out_specs=full,
        out_shape=jax.ShapeDtypeStruct((rows, s), F32),
        name="forget_cumsum",
    )(f_rows, bias_rows)


def _tree(parts, op):
    parts = list(parts)
    while len(parts) > 1:
        nxt = [op(parts[a], parts[a + 1]) for a in range(0, len(parts) - 1, 2)]
        if len(parts) % 2:
            nxt.append(parts[-1])
        parts = nxt
    return parts[0]


def _col_reduce(x, op, reduce_fn):
    rows = x.shape[0]
    part = _tree([x[r:r + SUBLANES, :] for r in range(0, rows, SUBLANES)], op)
    return reduce_fn(part, axis=0, keepdims=True)


def _flash_kernel(i_tab, j_tab, *refs, fox, heads, dqk, dv, tile):
    if fox:
        q_ref, k_ref, vt_ref, ck_ref, z_ref, o_ref, m_sc, l_sc, acc_sc = refs
    else:
        q_ref, k_ref, vt_ref, z_ref, o_ref, m_sc, l_sc, acc_sc = refs
    pair = pl.program_id(1)
    i = i_tab[pair]
    j = j_tab[pair]

    @pl.when(j == 0)
    def _():
        m_sc[...] = jnp.full(m_sc.shape, NEG, F32)
        l_sc[...] = jnp.zeros(l_sc.shape, F32)
        acc_sc[...] = jnp.zeros(acc_sc.shape, F32)

    def step(masked):
        if masked:
            causal = (lax.broadcasted_iota(jnp.int32, (tile, tile), 0)
                      <= lax.broadcasted_iota(jnp.int32, (tile, tile), 1))
        def logits(h):
            return _dot_nt(k_ref[:, h * dqk:(h + 1) * dqk], q_ref[:, h * dqk:(h + 1) * dqk])

        s_next = logits(0)
        for h in range(heads):
            s = s_next
            if h + 1 < heads:
                s_next = logits(h + 1)
            if fox:
                s = s - ck_ref[:, h:h + 1]
            if masked:
                s = jnp.where(causal, s, NEG)
            m_prev = m_sc[h:h + 1, :]
            m_new = jnp.maximum(m_prev, _col_reduce(s, jnp.maximum, jnp.max))
            alpha = jnp.exp2(m_prev - m_new)
            p = jnp.exp2(s - m_new)
            l_sc[h:h + 1, :] = alpha * l_sc[h:h + 1, :] + _col_reduce(p, jnp.add, jnp.sum)
            acc_sc[h * dv:(h + 1) * dv, :] = (alpha * acc_sc[h * dv:(h + 1) * dv, :]
                                              + _dot(vt_ref[h * dv:(h + 1) * dv, :], p.astype(BF16)))
            m_sc[h:h + 1, :] = m_new

    @pl.when(j < i)
    def _():
        step(False)

    @pl.when(j == i)
    def _():
        step(True)
        for h in range(heads):
            out_t = acc_sc[h * dv:(h + 1) * dv, :] / l_sc[h:h + 1, :]
            z = z_ref[:, h * dv:(h + 1) * dv].astype(F32)
            o_ref[:, h * dv:(h + 1) * dv] = (out_t.T * jax.nn.silu(z)).astype(o_ref.dtype)


def _flash_attention(q_arr, q_blk, k_arr, k_blk, vt_arr, z_arr, z_blk, *, batch, seq, heads,
                     dqk, dv, cum=None, tile=512):
    tile = min(tile, seq)
    nq = seq // tile
    fox = cum is not None
    pairs = [(i, j) for i in range(nq) for j in range(i + 1)]
    i_tab = jnp.asarray([p[0] for p in pairs], jnp.int32)
    j_tab = jnp.asarray([p[1] for p in pairs], jnp.int32)
    in_specs = [
        pl.BlockSpec((tile, heads * dqk), lambda b, p, it, jt: (b * nq + it[p], q_blk)),
        pl.BlockSpec((tile, heads * dqk), lambda b, p, it, jt: (b * nq + jt[p], k_blk)),
        pl.BlockSpec((heads * dv, tile), lambda b, p, it, jt: (0, b * nq + jt[p])),
    ]
    args = [q_arr, k_arr, vt_arr]
    if fox:
        in_specs.append(pl.BlockSpec((tile, LANES), lambda b, p, it, jt: (b * nq + jt[p], 0)))
        args.append(cum)
    in_specs.append(pl.BlockSpec((tile, heads * dv), lambda b, p, it, jt: (b * nq + it[p], z_blk)))
    args.append(z_arr)
    return pl.pallas_call(
        functools.partial(_flash_kernel, fox=fox, heads=heads, dqk=dqk, dv=dv, tile=tile),
        grid_spec=pltpu.PrefetchScalarGridSpec(
            num_scalar_prefetch=2,
            grid=(batch, len(pairs)),
            in_specs=in_specs,
            out_specs=pl.BlockSpec((tile, heads * dv), lambda b, p, it, jt: (b * nq + it[p], 0)),
            scratch_shapes=[pltpu.VMEM((heads, tile), F32), pltpu.VMEM((heads, tile), F32),
                            pltpu.VMEM((heads * dv, tile), F32)]),
        out_shape=jax.ShapeDtypeStruct((batch * seq, heads * dv), BF16),
        compiler_params=_params(("parallel", "arbitrary")),
        name="flash_fox" if fox else "flash_latent",
    )(i_tab, j_tab, *args)


def _swa_kernel(q_ref, kvc_ref, kvp_ref, z_ref, sink_ref, o_ref, *, slopes):
    i = pl.program_id(1)
    w = WINDOW
    pairs_per_group = C_HEADS // C_KV_HEADS // 2
    kv = jnp.concatenate([kvp_ref[...], kvc_ref[...]], axis=0).astype(F32)
    lane = lax.broadcasted_iota(jnp.int32, (2 * w, LANES), 1)
    low = lane < C_DIM

    def halves(t):
        g0_lo = jnp.where(low, t, 0.0)
        g1_hi = jnp.where(low, 0.0, t)
        return ((g0_lo, pltpu.roll(g0_lo, C_DIM, 1)), (pltpu.roll(g1_hi, C_DIM, 1), g1_hi))

    k_half = halves(kv[:, :LANES])
    v_half = halves(kv[:, LANES:])
    key = lax.broadcasted_iota(jnp.int32, (2 * w, w), 0)
    qry = lax.broadcasted_iota(jnp.int32, (2 * w, w), 1)
    dist = qry + w - key
    first_key = jnp.where(i > 0, 0, w)
    valid = (dist >= 0) & (dist < w) & (key >= first_key)
    dist_m = jnp.where(valid, dist.astype(F32), -NEG)

    q_rows = [jnp.concatenate([q_ref[:, (g * pairs_per_group + jj) * LANES:(g * pairs_per_group + jj + 1) * LANES]
                               for jj in range(pairs_per_group)], axis=0) for g in range(C_KV_HEADS)]
    logits = [[_dot_nt(k_half[g][half].astype(BF16), q_rows[g]) for half in range(2)] for g in range(C_KV_HEADS)]
    for g in range(C_KV_HEADS):
        out_t = None
        for half in range(2):
            probs, inv = [], []
            for jj in range(pairs_per_group):
                h = 2 * (g * pairs_per_group + jj) + half
                s = logits[g][half][:, jj * w:(jj + 1) * w] - (slopes[h] * LOG2E) * dist_m
                sink = jnp.full((1, w), sink_ref[h], F32) * LOG2E
                m = jnp.maximum(_col_reduce(s, jnp.maximum, jnp.max), sink)
                p = jnp.exp2(s - m)
                inv.append(1.0 / (_col_reduce(p, jnp.add, jnp.sum) + jnp.exp2(sink - m)))
                probs.append(p.astype(BF16))
            vt = v_half[g][half].T.astype(BF16)
            contrib = _dot(vt, jnp.concatenate(probs, axis=1)) * jnp.concatenate(inv, axis=1)
            out_t = contrib if out_t is None else out_t + contrib
        for jj in range(pairs_per_group):
            cols = slice((g * pairs_per_group + jj) * LANES, (g * pairs_per_group + jj + 1) * LANES)
            z = z_ref[:, cols].astype(F32)
            o_ref[:, cols] = (out_t[:, jj * w:(jj + 1) * w].T * jax.nn.silu(z)).astype(o_ref.dtype)


def _sliding_window(p_arr, sinks, batch, seq):
    w = WINDOW
    nb = seq // w
    kv_blk = P_OFF['c_k'] // (2 * LANES)
    return pl.pallas_call(
        functools.partial(_swa_kernel, slopes=_alibi_slopes(C_HEADS)),
        grid=(batch, nb),
        in_specs=[pl.BlockSpec((w, BRANCH_WIDTH), lambda b, i: (b * nb + i, P_OFF['c_q'] // BRANCH_WIDTH)),
                  pl.BlockSpec((w, 2 * LANES), lambda b, i: (b * nb + i, kv_blk)),
                  pl.BlockSpec((w, 2 * LANES), lambda b, i: (b * nb + jnp.maximum(i - 1, 0), kv_blk)),
                  pl.BlockSpec((w, BRANCH_WIDTH), lambda b, i: (b * nb + i, P_OFF['c_z'] // BRANCH_WIDTH)),
                  pl.BlockSpec(memory_space=pltpu.SMEM)],
        out_specs=pl.BlockSpec((w, BRANCH_WIDTH), lambda b, i: (b * nb + i, 0)),
        out_shape=jax.ShapeDtypeStruct((batch * seq, BRANCH_WIDTH), BF16),
        compiler_params=_params(("parallel", "parallel")),
        name="sliding_window",
    )(p_arr, p_arr, p_arr, p_arr, sinks.astype(F32))


def _latent_prep_kernel(cq_ref, ckv_ref, kr_ref, tab_ref, qg_ref, kvg_ref, qup_ref, kup_ref, vupt_ref,
                        q_ref, k_ref, vt_ref):
    def rms(x, g):
        return x * lax.rsqrt(jnp.mean(x * x, axis=-1, keepdims=True) + RMS_EPS) * g

    tab = tab_ref[...]
    lane = lax.broadcasted_iota(jnp.int32, tab.shape, 1)

    def rotate(t):
        r = t * tab
        return r + pltpu.roll(r, D_ROPE, 1)

    q = _dot(rms(cq_ref[...].astype(F32), qg_ref[...]).astype(BF16), qup_ref[...])
    ckv = rms(ckv_ref[...].astype(F32), kvg_ref[...]).astype(BF16)
    k_nope = _dot(ckv, kup_ref[...])
    vt_ref[...] = _dot_nt(vupt_ref[...], ckv).astype(vt_ref.dtype)
    k_rot = jnp.where(lane < D_ROPE, rotate(kr_ref[...].astype(F32)), 0.0).astype(k_ref.dtype)
    hw = 2 * LANES
    for h in range(D_HEADS):
        q_ref[:, h * hw:h * hw + LANES] = q[:, h * hw:h * hw + LANES].astype(q_ref.dtype)
        q_ref[:, h * hw + LANES:(h + 1) * hw] = rotate(q[:, h * hw + LANES:(h + 1) * hw]).astype(q_ref.dtype)
        k_ref[:, h * hw:h * hw + LANES] = k_nope[:, h * LANES:(h + 1) * LANES].astype(k_ref.dtype)
        k_ref[:, h * hw + LANES:(h + 1) * hw] = k_rot


def _latent_prep(p_arr, tab, q_gain, kv_gain, q_up, k_up, v_up_t, tm=512):
    t = p_arr.shape[0]
    tm = min(tm, t)
    hw = 2 * LANES

    def const(shape):
        return pl.BlockSpec(shape, lambda i: (0, 0))

    return pl.pallas_call(
        _latent_prep_kernel,
        grid=(t // tm,),
        in_specs=[pl.BlockSpec((tm, D_Q_LORA), lambda i: (i, P_OFF['d_cq'] // D_Q_LORA)),
                  pl.BlockSpec((tm, D_KV_LORA), lambda i: (i, P_OFF['d_ckv'] // D_KV_LORA)),
                  pl.BlockSpec((tm, LANES), lambda i: (i, P_OFF['d_kr'] // LANES)),
                  pl.BlockSpec((tm, LANES), lambda i: (i, 0)),
                  const((1, D_Q_LORA)), const((1, D_KV_LORA)),
                  const((D_Q_LORA, D_HEADS * hw)), const((D_KV_LORA, D_HEADS * LANES)),
                  const((D_HEADS * D_VDIM, D_KV_LORA))],
        out_specs=[pl.BlockSpec((tm, D_HEADS * hw), lambda i: (i, 0)),
                   pl.BlockSpec((tm, D_HEADS * hw), lambda i: (i, 0)),
                   pl.BlockSpec((D_HEADS * D_VDIM, tm), lambda i: (0, i))],
        out_shape=[jax.ShapeDtypeStruct((t, D_HEADS * hw), BF16),
                   jax.ShapeDtypeStruct((t, D_HEADS * hw), BF16),
                   jax.ShapeDtypeStruct((D_HEADS * D_VDIM, t), BF16)],
        compiler_params=_params(("parallel",)),
        name="latent_prep",
    )(p_arr, p_arr, p_arr, tab, q_gain.reshape(1, -1), kv_gain.reshape(1, -1), q_up, k_up, v_up_t)


def _sparse_kernel(q_ref, iq_ref, z_ref, ik_ref, k_ref, vt_ref, wt_ref, o_ref,
                   keys_ref, jcut_ref, m_ref, l_ref, acc_ref, *, topk, ck, slopes, idx_bits):
    i = pl.program_id(1)
    tq = LANES
    nch = ((i + 1) * tq + ck - 1) // ck
    q_pos = i * tq + lax.broadcasted_iota(jnp.int32, (ck, tq), 1)
    s_iota = lax.broadcasted_iota(jnp.int32, (ck, tq), 0)
    lane = lax.broadcasted_iota(jnp.int32, (ck, LANES), 1)
    low = lane < IDX_DIM
    w_all = wt_ref[0] * (IDX_HEADS ** -0.5 * IDX_DIM ** -0.5)

    def score_chunk(c, carry):
        start = pl.multiple_of(c * ck, ck)
        ikc = ik_ref[pl.ds(start, ck), :]
        ik_lo = jnp.where(low, ikc, jnp.zeros_like(ikc))
        ik_hi = jnp.where(low, jnp.zeros_like(ikc), ikc)
        acc = jnp.zeros((ck, tq), F32)
        for pair in range(IDX_HEADS // 2):
            iqp = iq_ref[:, pair * LANES:(pair + 1) * LANES]
            acc = acc + w_all[2 * pair:2 * pair + 1, :] * jnp.maximum(_dot_nt(ik_lo, iqp), 0.0)
            acc = acc + w_all[2 * pair + 1:2 * pair + 2, :] * jnp.maximum(_dot_nt(ik_hi, iqp), 0.0)
        acc = jnp.where(acc == 0.0, 0.0, acc)
        bits = lax.bitcast_convert_type(acc, jnp.int32)
        key = bits ^ ((bits >> 31) & 0x7FFFFFFF)
        key = jnp.where(start + s_iota <= q_pos, key, INT_MIN)
        keys_ref[pl.ds(start, ck), :] = key
        return carry

    lax.fori_loop(0, nch, score_chunk, 0)

    def count(pred_fn):
        def body(c, cnt):
            start = pl.multiple_of(c * ck, ck)
            hit = jnp.where(pred_fn(keys_ref[pl.ds(start, ck), :], start + s_iota), 1.0, 0.0)
            return cnt + _tree([hit[r:r + SUBLANES, :] for r in range(0, ck, SUBLANES)], jnp.add)
        part = lax.fori_loop(0, nch, body, jnp.zeros((SUBLANES, tq), F32))
        return jnp.sum(part, axis=0, keepdims=True)

    kf = float(topk)

    def thr_step(it, thr):
        cand = thr ^ (jnp.int32(1) << (31 - it))
        cnt = count(lambda kc, pos: kc >= cand)
        return jnp.where(cnt >= kf, cand, thr)

    thr = lax.fori_loop(0, 32, thr_step, jnp.full((1, tq), INT_MIN, jnp.int32))

    n_ge = count(lambda kc, pos: kc >= thr)
    n_gt = count(lambda kc, pos: kc > thr)
    need = kf - n_gt
    tied = (n_ge > kf) & (thr != INT_MIN)
    jcut_ref[...] = jnp.full(jcut_ref.shape, 2 ** 30, jnp.int32)

    @pl.when(jnp.max(tied.astype(F32)) > 0.0)
    def _():
        def cut_step(it, x):
            cand = x + (jnp.int32(1) << (idx_bits - 1 - it))
            cnt = count(lambda kc, pos: (kc == thr) & (pos < cand))
            return jnp.where(cnt < need, cand, x)
        x = lax.fori_loop(0, idx_bits, cut_step, jnp.zeros((1, tq), jnp.int32))
        jcut_ref[...] = jnp.broadcast_to(jnp.where(tied, x, 2 ** 30), jcut_ref.shape)

    jcut = jcut_ref[0:1, :]

    m_ref[...] = jnp.full(m_ref.shape, NEG, F32)
    l_ref[...] = jnp.zeros(l_ref.shape, F32)
    acc_ref[...] = jnp.zeros(acc_ref.shape, F32)
    rep = A_HEADS // A_KV_HEADS
    q_groups = [jnp.concatenate([q_ref[:, (g * rep + r) * A_DIM:(g * rep + r + 1) * A_DIM] for r in range(rep)],
                                axis=0) for g in range(A_KV_HEADS)]

    def attend_chunk(c, carry):
        start = pl.multiple_of(c * ck, ck)
        kc = keys_ref[pl.ds(start, ck), :]
        pos = start + s_iota
        sel = ((kc > thr) | ((kc == thr) & (pos <= jcut))) & (pos <= q_pos)
        dist = (q_pos - pos).astype(F32)
        logits_g = [_dot_nt(k_ref[pl.ds(start, ck), g * A_DIM:(g + 1) * A_DIM], q_groups[g])
                    for g in range(A_KV_HEADS)]
        for g in range(A_KV_HEADS):
            vtg = vt_ref[0, c, g * A_DIM:(g + 1) * A_DIM, :]
            for r in range(rep):
                h = g * rep + r
                s = logits_g[g][:, r * tq:(r + 1) * tq] - (slopes[h] * LOG2E) * dist
                s = jnp.where(sel, s, MASKED)
                m_prev = m_ref[h:h + 1, :]
                m_new = jnp.maximum(m_prev, _col_reduce(s, jnp.maximum, jnp.max))
                alpha = jnp.exp2(m_prev - m_new)
                p = jnp.exp2(s - m_new)
                l_ref[h:h + 1, :] = alpha * l_ref[h:h + 1, :] + _col_reduce(p, jnp.add, jnp.sum)
                acc_ref[h * A_DIM:(h + 1) * A_DIM, :] = (alpha * acc_ref[h * A_DIM:(h + 1) * A_DIM, :]
                                                         + _dot(vtg, p.astype(BF16)))
                m_ref[h:h + 1, :] = m_new
        return carry

    lax.fori_loop(0, nch, attend_chunk, 0)

    for h in range(A_HEADS):
        out_t = acc_ref[h * A_DIM:(h + 1) * A_DIM, :] / l_ref[h:h + 1, :]
        z = z_ref[:, h * A_DIM:(h + 1) * A_DIM].astype(F32)
        o_ref[:, h * A_DIM:(h + 1) * A_DIM] = (out_t.T * jax.nn.silu(z)).astype(o_ref.dtype)


def _sparse_attention(p_arr, vt, wt, batch, seq):
    tq = LANES
    nq = seq // tq
    ck = min(512, seq)
    topk = min(IDX_TOPK_MAX, seq // 4)
    idx_bits = int(seq).bit_length()
    return pl.pallas_call(
        functools.partial(_sparse_kernel, topk=topk, ck=ck, slopes=_alibi_slopes(A_HEADS), idx_bits=idx_bits),
        grid=(batch, nq),
        in_specs=[pl.BlockSpec((tq, BRANCH_WIDTH), lambda b, i: (b * nq + i, P_OFF['a_q'] // BRANCH_WIDTH)),
                  pl.BlockSpec((tq, BRANCH_WIDTH), lambda b, i: (b * nq + i, P_OFF['a_iq'] // BRANCH_WIDTH)),
                  pl.BlockSpec((tq, BRANCH_WIDTH), lambda b, i: (b * nq + i, P_OFF['a_z'] // BRANCH_WIDTH)),
                  pl.BlockSpec((seq, LANES), lambda b, i: (b, P_OFF['a_ik'] // LANES)),
                  pl.BlockSpec((seq, 2 * LANES), lambda b, i: (b, P_OFF['a_k'] // (2 * LANES))),
                  pl.BlockSpec((1, seq // ck, 2 * LANES, ck), lambda b, i: (b, 0, 0, 0)),
                  pl.BlockSpec((1, IDX_HEADS, tq), lambda b, i: (b, 0, i))],
        out_specs=pl.BlockSpec((tq, BRANCH_WIDTH), lambda b, i: (b * nq + i, 0)),
        out_shape=jax.ShapeDtypeStruct((batch * seq, BRANCH_WIDTH), BF16),
        scratch_shapes=[pltpu.VMEM((seq, tq), jnp.int32), pltpu.VMEM((8, tq), jnp.int32),
                        pltpu.VMEM((A_HEADS, tq), F32), pltpu.VMEM((A_HEADS, tq), F32),
                        pltpu.VMEM((A_HEADS * A_DIM, tq), F32)],
        compiler_params=_params(("parallel", "arbitrary")),
        name="sparse_attention",
    )(p_arr, p_arr, p_arr, p_arr, p_arr, vt, wt)


def _swap_halves(w):
    half = w.shape[-1] // 2
    return jnp.concatenate([-w[..., half:], w[..., :half]], axis=-1)


def _prepare_in_proj(w_in):
    sizes = [n for _, n in IN_SEGMENTS]
    offs = np.concatenate([[0], np.cumsum(sizes)])
    span = {name: (int(offs[k]), int(offs[k + 1])) for k, (name, _) in enumerate(IN_SEGMENTS)}
    wt32 = jnp.swapaxes(w_in, 1, 2)
    wt16 = wt32.astype(BF16)
    seg16 = lambda name: wt16[:, span[name][0]:span[name][1], :]
    seg32 = lambda name: wt32[:, span[name][0]:span[name][1], :]
    rows = []
    for name, _ in P_LAYOUT:
        if name == 'a_ik':
            rows += [seg16('a_ik'), seg16('a_ik')]
        elif name == 'd_kr':
            kr = seg16('d_kr')
            rows += [kr, -kr[:, D_ROPE // 2:, :], kr[:, :D_ROPE // 2, :]]
        elif name == 'a_q':
            rows.append((seg32('a_q') * (A_DIM ** -0.5 * LOG2E)).astype(BF16))
        elif name == 'b_q':
            rows.append((seg32('b_q') * (B_DIM ** -0.5 * LOG2E)).astype(BF16))
        elif name == 'c_q':
            rows.append((seg32('c_q') * (C_DIM ** -0.5 * LOG2E)).astype(BF16))
        else:
            rows.append(seg16(name))
    w_main_t = jnp.concatenate(rows, axis=1)
    pad = jnp.zeros((w_in.shape[0], LANES - IDX_HEADS - B_HEADS, w_in.shape[1]), BF16)
    w_small_t = jnp.concatenate([seg16('a_iw'), seg16('b_f'), pad], axis=1)
    return w_main_t, w_small_t


def _prepare_latent(dq_up, dkv_up):
    q = dq_up.reshape(D_Q_LORA, D_HEADS, D_NOPE + D_ROPE) * ((D_NOPE + D_ROPE) ** -0.5 * LOG2E)
    rope = q[..., D_NOPE:]
    q_up = jnp.concatenate([q[..., :D_NOPE], rope, _swap_halves(rope)], axis=-1)
    kv = dkv_up.reshape(D_KV_LORA, D_HEADS, D_NOPE + D_VDIM)
    return (q_up.reshape(D_Q_LORA, -1).astype(BF16),
            kv[..., :D_NOPE].reshape(D_KV_LORA, -1).astype(BF16),
            kv[..., D_NOPE:].reshape(D_KV_LORA, -1).T.astype(BF16))


def _rope_table(positions):
    half = D_ROPE // 2
    inv_freq = ROPE_THETA ** (-jnp.arange(half, dtype=F32) / half)
    ang = positions.astype(F32)[..., None] * inv_freq
    cos, sin = jnp.cos(ang), jnp.sin(ang)
    tab = jnp.concatenate([cos, cos, sin, sin], axis=-1)
    return tab.reshape(-1, 4 * half)


def _layer(layer, x, xb, tab, batch, seq, w_main, w_small, w_gate, w_branch, w_out, dq_gain, dq_up, dkv_gain,
           dkv_up, f_bias, sinks, ln_gain, ln_bias, alpha):
    proj = _matmul(xb, w_main, layer, BF16, transposed=True)
    small = _matmul(xb, w_small, layer, F32, transposed=True, tn=LANES)

    ck = min(512, seq)
    wt = small[:, :IDX_HEADS].reshape(batch, seq, IDX_HEADS).transpose(0, 2, 1)
    a_v = proj[:, P_OFF['a_v']:P_OFF['a_v'] + 2 * LANES]
    vt = a_v.reshape(batch, seq // ck, ck, 2 * LANES).transpose(0, 1, 3, 2)
    o_a = _sparse_attention(proj, vt, wt, batch, seq)

    f_rows = small[:, IDX_HEADS:IDX_HEADS + B_HEADS].reshape(batch, seq, B_HEADS).transpose(0, 2, 1)
    cum = _forget_cumsum(f_rows.reshape(batch * B_HEADS, seq),
                         jnp.tile(f_bias.astype(F32), batch).reshape(batch * B_HEADS, 1))
    cum = cum.reshape(batch, B_HEADS, seq).transpose(0, 2, 1).reshape(batch * seq, B_HEADS)
    cum = jnp.pad(cum, ((0, 0), (0, LANES - B_HEADS)))
    vt_b = proj[:, P_OFF['b_v']:P_OFF['b_v'] + BRANCH_WIDTH].T
    o_b = _flash_attention(proj, P_OFF['b_q'] // BRANCH_WIDTH, proj, P_OFF['b_k'] // BRANCH_WIDTH, vt_b,
                           proj, P_OFF['b_z'] // BRANCH_WIDTH, batch=batch, seq=seq, heads=B_HEADS,
                           dqk=B_DIM, dv=B_DIM, cum=cum)

    o_c = _sliding_window(proj, sinks, batch, seq)

    q_up, k_up, v_up_t = _prepare_latent(dq_up, dkv_up)
    q_d, k_d, vt_d = _latent_prep(proj, tab, dq_gain, dkv_gain, q_up, k_up, v_up_t)
    o_d = _flash_attention(q_d, 0, k_d, 0, vt_d, proj, P_OFF['d_z'] // BRANCH_WIDTH, batch=batch, seq=seq,
                           heads=D_HEADS, dqk=2 * LANES, dv=D_VDIM)

    merged = _gated_merge(xb, w_gate, (o_a, o_b, o_c, o_d), w_branch, layer)
    y = _matmul(merged, w_out, layer, F32)
    return _residual_layernorm(x, y, ln_gain, ln_bias, alpha)


def kernel(x, positions, w_in, w_gate, w_branch, w_out, dq_gain, dq_up, dkv_gain, dkv_up, f_bias, sinks,
           ln_gain, ln_bias):
    batch, seq, d = x.shape
    depth = w_in.shape[0]
    alpha = (2 * depth) ** 0.25
    tab = _rope_table(positions)
    xf = x.reshape(batch * seq, d)
    xb = _cast_bf16(xf)
    w_main, w_small = _prepare_in_proj(w_in)
    w_gate, w_branch, w_out = w_gate.astype(BF16), w_branch.astype(BF16), w_out.astype(BF16)
    for l in range(depth):
        xf, xb = _layer(l, xf, xb, tab, batch, seq, w_main, w_small, w_gate, w_branch, w_out, dq_gain[l],
                        dq_up[l], dkv_gain[l], dkv_up[l], f_bias[l], sinks[l], ln_gain[l], ln_bias[l], alpha)
    return xf.reshape(batch, seq, d)
```

```python
import functools

import numpy as np
import jax
import jax.numpy as jnp
from jax import lax
from jax.experimental import pallas as pl
from jax.experimental.pallas import tpu as pltpu

BRANCH_WIDTH = 1024
A_HEADS, A_KV_HEADS, A_DIM = 8, 2, 128
IDX_HEADS, IDX_DIM, IDX_TOPK_MAX = 16, 64, 256
B_HEADS, B_DIM = 8, 128
C_HEADS, C_KV_HEADS, C_DIM, WINDOW = 16, 2, 64, 128
D_HEADS, D_Q_LORA, D_KV_LORA, D_NOPE, D_ROPE, D_VDIM = 8, 768, 256, 128, 64, 128
ROPE_THETA = 10000.0
RMS_EPS = 1e-6
LN_EPS = 1e-5

IN_SEGMENTS = (
    ('a_q', 1024), ('a_k', 256), ('a_v', 256), ('a_iq', 1024), ('a_ik', 64), ('a_iw', 16), ('a_z', 1024),
    ('b_q', 1024), ('b_k', 1024), ('b_v', 1024), ('b_f', 8), ('b_z', 1024),
    ('c_q', 1024), ('c_k', 128), ('c_v', 128), ('c_z', 1024),
    ('d_cq', 768), ('d_ckv', 256), ('d_kr', 64), ('d_z', 1024),
)

P_LAYOUT = (
    ('d_cq', 768), ('d_ckv', 256), ('a_q', 1024), ('a_iq', 1024), ('a_z', 1024),
    ('b_q', 1024), ('b_k', 1024), ('b_v', 1024), ('b_z', 1024), ('c_q', 1024), ('c_z', 1024),
    ('d_z', 1024), ('a_k', 256), ('a_v', 256), ('a_ik', 128), ('d_kr', 128), ('c_k', 128), ('c_v', 128),
)
P_OFF = {}
_o = 0
for _n, _w in P_LAYOUT:
    assert _o % _w == 0
    P_OFF[_n] = _o
    _o += _w
P_WIDTH = _o

LANES = 128
SUBLANES = 8
NEG = -1e30
MASKED = -3e38
LOG2E = 1.4426950408889634
INT_MIN = -2 ** 31
VMEM_LIMIT = 56 * 1024 * 1024

F32 = jnp.float32
BF16 = jnp.bfloat16


def _params(sem, vmem=VMEM_LIMIT):
    return pltpu.CompilerParams(dimension_semantics=sem, vmem_limit_bytes=vmem)


def _dot(a, b):
    return jnp.dot(a, b, preferred_element_type=F32)


def _dot_nt(a, b):
    return lax.dot_general(a, b, (((1,), (1,)), ((), ())), preferred_element_type=F32)


def _alibi_slopes(n_heads):
    return [float(np.float32(2.0 ** (-8.0 * (h + 1) / n_heads))) for h in range(n_heads)]


def _mm_kernel(x_ref, w_ref, o_ref):
    o_ref[...] = _dot(x_ref[...], w_ref[0]).astype(o_ref.dtype)


def _mm_nt_kernel(x_ref, wt_ref, o_ref):
    o_ref[...] = _dot_nt(x_ref[...], wt_ref[0]).astype(o_ref.dtype)


def _matmul(x, w, layer, out_dtype, transposed=False, tm=1024, tn=1024):
    m, k = x.shape
    n = w.shape[1] if transposed else w.shape[2]
    tm, tn = min(tm, m), min(tn, n)
    if transposed:
        body, w_spec = _mm_nt_kernel, pl.BlockSpec((1, tn, k), lambda i, j: (layer, j, 0))
    else:
        body, w_spec = _mm_kernel, pl.BlockSpec((1, k, tn), lambda i, j: (layer, 0, j))
    return pl.pallas_call(
        body,
        grid=(m // tm, n // tn),
        in_specs=[pl.BlockSpec((tm, k), lambda i, j: (i, 0)), w_spec],
        out_specs=pl.BlockSpec((tm, tn), lambda i, j: (i, j)),
        out_shape=jax.ShapeDtypeStruct((m, n), out_dtype),
        compiler_params=_params(("parallel", "parallel")),
        name="dense_matmul",
    )(x, w)


def _cast_kernel(x_ref, o_ref):
    o_ref[...] = x_ref[...].astype(o_ref.dtype)


def _cast_bf16(x, tm=512):
    t, d = x.shape
    tm = min(tm, t)
    row = pl.BlockSpec((tm, d), lambda i: (i, 0))
    return pl.pallas_call(
        _cast_kernel, grid=(t // tm,), in_specs=[row], out_specs=row,
        out_shape=jax.ShapeDtypeStruct((t, d), BF16),
        compiler_params=_params(("parallel",)),
        name="cast_bf16",
    )(x)


def _merge_kernel(x_ref, wg_ref, b0_ref, b1_ref, b2_ref, b3_ref, wb_ref, o_ref, acc_ref, *, sub):
    n = pl.program_id(2)

    @pl.when((pl.program_id(0) == 0) & (pl.program_id(1) == 0) & (n == 0))
    def _():
        acc_ref[...] = jnp.zeros(acc_ref.shape, F32)

    branch = jnp.where(n == 0, b0_ref[...], jnp.where(n == 1, b1_ref[...],
                                                      jnp.where(n == 2, b2_ref[...], b3_ref[...])))
    x = x_ref[...]
    for c in range(o_ref.shape[1] // sub):
        cols = slice(c * sub, (c + 1) * sub)
        gate = jax.nn.sigmoid(_dot(x, wg_ref[0, 0, :, cols]))
        contrib = gate * _dot(branch, wb_ref[0, 0, :, cols])
        acc = jnp.where(n == 0, 0.0, acc_ref[:, cols]) + contrib
        acc_ref[:, cols] = acc
        o_ref[:, cols] = acc.astype(o_ref.dtype)


def _gated_merge(xb, wg, branches, wb, layer, tm=512, tn=1024, sub=256):
    t, d = xb.shape
    tm, tn = min(tm, t), min(tn, d)
    sub = min(sub, tn)
    bw = branches[0].shape[1]
    bspec = pl.BlockSpec((tm, bw), lambda i, j, n: (i, 0))
    return pl.pallas_call(
        functools.partial(_merge_kernel, sub=sub),
        grid=(t // tm, d // tn, 4),
        in_specs=[pl.BlockSpec((tm, d), lambda i, j, n: (i, 0)),
                  pl.BlockSpec((1, 1, d, tn), lambda i, j, n: (layer, n, 0, j)),
                  bspec, bspec, bspec, bspec,
                  pl.BlockSpec((1, 1, bw, tn), lambda i, j, n: (layer, n, 0, j))],
        out_specs=pl.BlockSpec((tm, tn), lambda i, j, n: (i, j)),
        out_shape=jax.ShapeDtypeStruct((t, d), BF16),
        scratch_shapes=[pltpu.VMEM((tm, tn), F32)],
        compiler_params=_params(("arbitrary", "arbitrary", "arbitrary")),
        name="gated_merge",
    )(xb, wg, *branches, wb)


def _ln_kernel(x_ref, y_ref, g_ref, b_ref, o_ref, ob_ref, *, alpha):
    r = alpha * x_ref[...] + y_ref[...]
    mu = jnp.mean(r, axis=-1, keepdims=True)
    c = r - mu
    var = jnp.mean(c * c, axis=-1, keepdims=True)
    out = c * lax.rsqrt(var + LN_EPS) * g_ref[...] + b_ref[...]
    o_ref[...] = out
    ob_ref[...] = out.astype(BF16)


def _residual_layernorm(x, y, gain, bias, alpha, tm=256):
    t, d = x.shape
    tm = min(tm, t)
    row = pl.BlockSpec((tm, d), lambda i: (i, 0))
    vec = pl.BlockSpec((1, d), lambda i: (0, 0))
    return pl.pallas_call(
        functools.partial(_ln_kernel, alpha=alpha),
        grid=(t // tm,),
        in_specs=[row, row, vec, vec],
        out_specs=[row, row],
        out_shape=[jax.ShapeDtypeStruct((t, d), F32), jax.ShapeDtypeStruct((t, d), BF16)],
        compiler_params=_params(("parallel",)),
        name="residual_layernorm",
    )(x, y, gain.reshape(1, d), bias.reshape(1, d))


def _forget_cumsum_kernel(f_ref, bias_ref, c_ref):
    rows, s = f_ref.shape
    lane = lax.broadcasted_iota(jnp.int32, (rows, LANES), 1)
    carry = jnp.zeros((rows, 1), F32)
    for c in range(s // LANES):
        x = jax.nn.log_sigmoid(f_ref[:, c * LANES:(c + 1) * LANES] + bias_ref[...])
        shift = 1
        while shift < LANES:
            x = x + jnp.where(lane >= shift, pltpu.roll(x, shift, 1), 0.0)
            shift *= 2
        x = x + carry
        c_ref[:, c * LANES:(c + 1) * LANES] = x * LOG2E
        carry = x[:, LANES - 1:LANES]


def _forget_cumsum(f_rows, bias_rows):
    rows, s = f_rows.shape
    full = pl.BlockSpec((rows, s), lambda: (0, 0))
    return pl.pallas_call(
        _forget_cumsum_kernel,
        in_specs=[full, pl.BlockSpec((rows, 1), lambda: (0, 0))],
        out_specs=full,
        out_shape=jax.ShapeDtypeStruct((rows, s), F32),
        name="forget_cumsum",
    )(f_rows, bias_rows)


def _tree(parts, op, ways=8):
    parts = list(parts)
    if len(parts) > ways:
        accs = parts[:ways]
        for a in range(ways, len(parts)):
            accs[a % ways] = op(accs[a % ways], parts[a])
        parts = accs
    while len(parts) > 1:
        nxt = [op(parts[a], parts[a + 1]) for a in range(0, len(parts) - 1, 2)]
        if len(parts) % 2:
            nxt.append(parts[-1])
        parts = nxt
    return parts[0]


def _col_reduce(x, op, reduce_fn):
    rows = x.shape[0]
    part = _tree([x[r:r + SUBLANES, :] for r in range(0, rows, SUBLANES)], op)
    return reduce_fn(part, axis=0, keepdims=True)


def _flash_kernel(i_tab, j_tab, *refs, fox, heads, dqk, dv, tile):
    if fox:
        q_ref, k_ref, vt_ref, ck_ref, z_ref, o_ref, m_sc, l_sc, acc_sc = refs
    else:
        q_ref, k_ref, vt_ref, z_ref, o_ref, m_sc, l_sc, acc_sc = refs
    pair = pl.program_id(1)
    i = i_tab[pair]
    j = j_tab[pair]

    @pl.when(j == 0)
    def _():
        m_sc[...] = jnp.full(m_sc.shape, NEG, F32)
        l_sc[...] = jnp.zeros(l_sc.shape, F32)
        acc_sc[...] = jnp.zeros(acc_sc.shape, F32)

    def step(masked):
        if masked:
            causal = (lax.broadcasted_iota(jnp.int32, (tile, tile), 0)
                      <= lax.broadcasted_iota(jnp.int32, (tile, tile), 1))
        def logits(h):
            return _dot_nt(k_ref[:, h * dqk:(h + 1) * dqk], q_ref[:, h * dqk:(h + 1) * dqk])

        s_next = logits(0)
        for h in range(heads):
            s = s_next
            if h + 1 < heads:
                s_next = logits(h + 1)
            if fox:
                s = s - ck_ref[:, h:h + 1]
            if masked:
                s = jnp.where(causal, s, NEG)
            m_prev = m_sc[h:h + 1, :]
            m_new = jnp.maximum(m_prev, _col_reduce(s, jnp.maximum, jnp.max))
            alpha = jnp.exp2(m_prev - m_new)
            p = jnp.exp2(s - m_new)
            l_sc[h:h + 1, :] = alpha * l_sc[h:h + 1, :] + _col_reduce(p, jnp.add, jnp.sum)
            acc_sc[h * dv:(h + 1) * dv, :] = (alpha * acc_sc[h * dv:(h + 1) * dv, :]
                                              + _dot(vt_ref[h * dv:(h + 1) * dv, :], p.astype(BF16)))
            m_sc[h:h + 1, :] = m_new

    @pl.when(j < i)
    def _():
        step(False)

    @pl.when(j == i)
    def _():
        step(True)
        for h in range(heads):
            out_t = acc_sc[h * dv:(h + 1) * dv, :] / l_sc[h:h + 1, :]
            z = z_ref[:, h * dv:(h + 1) * dv].astype(F32)
            o_ref[:, h * dv:(h + 1) * dv] = (out_t.T * jax.nn.silu(z)).astype(o_ref.dtype)


def _flash_attention(q_arr, q_blk, k_arr, k_blk, vt_arr, z_arr, z_blk, *, batch, seq, heads,
                     dqk, dv, cum=None, tile=512):
    tile = min(tile, seq)
    nq = seq // tile
    fox = cum is not None
    pairs = [(i, j) for i in range(nq) for j in range(i + 1)]
    i_tab = jnp.asarray([p[0] for p in pairs], jnp.int32)
    j_tab = jnp.asarray([p[1] for p in pairs], jnp.int32)
    in_specs = [
        pl.BlockSpec((tile, heads * dqk), lambda b, p, it, jt: (b * nq + it[p], q_blk)),
        pl.BlockSpec((tile, heads * dqk), lambda b, p, it, jt: (b * nq + jt[p], k_blk)),
        pl.BlockSpec((heads * dv, tile), lambda b, p, it, jt: (0, b * nq + jt[p])),
    ]
    args = [q_arr, k_arr, vt_arr]
    if fox:
        in_specs.append(pl.BlockSpec((tile, LANES), lambda b, p, it, jt: (b * nq + jt[p], 0)))
        args.append(cum)
    in_specs.append(pl.BlockSpec((tile, heads * dv), lambda b, p, it, jt: (b * nq + it[p], z_blk)))
    args.append(z_arr)
    return pl.pallas_call(
        functools.partial(_flash_kernel, fox=fox, heads=heads, dqk=dqk, dv=dv, tile=tile),
        grid_spec=pltpu.PrefetchScalarGridSpec(
            num_scalar_prefetch=2,
            grid=(batch, len(pairs)),
            in_specs=in_specs,
            out_specs=pl.BlockSpec((tile, heads * dv), lambda b, p, it, jt: (b * nq + it[p], 0)),
            scratch_shapes=[pltpu.VMEM((heads, tile), F32), pltpu.VMEM((heads, tile), F32),
                            pltpu.VMEM((heads * dv, tile), F32)]),
        out_shape=jax.ShapeDtypeStruct((batch * seq, heads * dv), BF16),
        compiler_params=_params(("parallel", "arbitrary")),
        name="flash_fox" if fox else "flash_latent",
    )(i_tab, j_tab, *args)


def _swa_kernel(q_ref, kvc_ref, kvp_ref, z_ref, sink_ref, o_ref, *, slopes):
    i = pl.program_id(1)
    w = WINDOW
    pairs_per_group = C_HEADS // C_KV_HEADS // 2
    kv = jnp.concatenate([kvp_ref[...], kvc_ref[...]], axis=0).astype(F32)
    lane = lax.broadcasted_iota(jnp.int32, (2 * w, LANES), 1)
    low = lane < C_DIM

    def halves(t):
        g0_lo = jnp.where(low, t, 0.0)
        g1_hi = jnp.where(low, 0.0, t)
        return ((g0_lo, pltpu.roll(g0_lo, C_DIM, 1)), (pltpu.roll(g1_hi, C_DIM, 1), g1_hi))

    k_half = halves(kv[:, :LANES])
    v_half = halves(kv[:, LANES:])
    key = lax.broadcasted_iota(jnp.int32, (2 * w, w), 0)
    qry = lax.broadcasted_iota(jnp.int32, (2 * w, w), 1)
    dist = qry + w - key
    first_key = jnp.where(i > 0, 0, w)
    valid = (dist >= 0) & (dist < w) & (key >= first_key)
    dist_m = jnp.where(valid, dist.astype(F32), -NEG)

    q_rows = [jnp.concatenate([q_ref[:, (g * pairs_per_group + jj) * LANES:(g * pairs_per_group + jj + 1) * LANES]
                               for jj in range(pairs_per_group)], axis=0) for g in range(C_KV_HEADS)]
    logits = [[_dot_nt(k_half[g][half].astype(BF16), q_rows[g]) for half in range(2)] for g in range(C_KV_HEADS)]
    for g in range(C_KV_HEADS):
        out_t = None
        for half in range(2):
            probs, inv = [], []
            for jj in range(pairs_per_group):
                h = 2 * (g * pairs_per_group + jj) + half
                s = logits[g][half][:, jj * w:(jj + 1) * w] - (slopes[h] * LOG2E) * dist_m
                sink = jnp.full((1, w), sink_ref[h], F32) * LOG2E
                m = jnp.maximum(_col_reduce(s, jnp.maximum, jnp.max), sink)
                p = jnp.exp2(s - m)
                inv.append(1.0 / (_col_reduce(p, jnp.add, jnp.sum) + jnp.exp2(sink - m)))
                probs.append(p.astype(BF16))
            vt = v_half[g][half].T.astype(BF16)
            contrib = _dot(vt, jnp.concatenate(probs, axis=1)) * jnp.concatenate(inv, axis=1)
            out_t = contrib if out_t is None else out_t + contrib
        for jj in range(pairs_per_group):
            cols = slice((g * pairs_per_group + jj) * LANES, (g * pairs_per_group + jj + 1) * LANES)
            z = z_ref[:, cols].astype(F32)
            o_ref[:, cols] = (out_t[:, jj * w:(jj + 1) * w].T * jax.nn.silu(z)).astype(o_ref.dtype)


def _sliding_window(p_arr, sinks, batch, seq):
    w = WINDOW
    nb = seq // w
    kv_blk = P_OFF['c_k'] // (2 * LANES)
    return pl.pallas_call(
        functools.partial(_swa_kernel, slopes=_alibi_slopes(C_HEADS)),
        grid=(batch, nb),
        in_specs=[pl.BlockSpec((w, BRANCH_WIDTH), lambda b, i: (b * nb + i, P_OFF['c_q'] // BRANCH_WIDTH)),
                  pl.BlockSpec((w, 2 * LANES), lambda b, i: (b * nb + i, kv_blk)),
                  pl.BlockSpec((w, 2 * LANES), lambda b, i: (b * nb + jnp.maximum(i - 1, 0), kv_blk)),
                  pl.BlockSpec((w, BRANCH_WIDTH), lambda b, i: (b * nb + i, P_OFF['c_z'] // BRANCH_WIDTH)),
                  pl.BlockSpec(memory_space=pltpu.SMEM)],
        out_specs=pl.BlockSpec((w, BRANCH_WIDTH), lambda b, i: (b * nb + i, 0)),
        out_shape=jax.ShapeDtypeStruct((batch * seq, BRANCH_WIDTH), BF16),
        compiler_params=_params(("parallel", "parallel")),
        name="sliding_window",
    )(p_arr, p_arr, p_arr, p_arr, sinks.astype(F32))


def _latent_prep_kernel(cq_ref, ckv_ref, kr_ref, tab_ref, qg_ref, kvg_ref, qup_ref, kup_ref, vupt_ref,
                        q_ref, k_ref, vt_ref):
    def rms(x, g):
        return x * lax.rsqrt(jnp.mean(x * x, axis=-1, keepdims=True) + RMS_EPS) * g

    tab = tab_ref[...]
    lane = lax.broadcasted_iota(jnp.int32, tab.shape, 1)

    def rotate(t):
        r = t * tab
        return r + pltpu.roll(r, D_ROPE, 1)

    q = _dot(rms(cq_ref[...].astype(F32), qg_ref[...]).astype(BF16), qup_ref[...])
    ckv = rms(ckv_ref[...].astype(F32), kvg_ref[...]).astype(BF16)
    k_nope = _dot(ckv, kup_ref[...])
    vt_ref[...] = _dot_nt(vupt_ref[...], ckv).astype(vt_ref.dtype)
    k_rot = jnp.where(lane < D_ROPE, rotate(kr_ref[...].astype(F32)), 0.0).astype(k_ref.dtype)
    hw = 2 * LANES
    for h in range(D_HEADS):
        q_ref[:, h * hw:h * hw + LANES] = q[:, h * hw:h * hw + LANES].astype(q_ref.dtype)
        q_ref[:, h * hw + LANES:(h + 1) * hw] = rotate(q[:, h * hw + LANES:(h + 1) * hw]).astype(q_ref.dtype)
        k_ref[:, h * hw:h * hw + LANES] = k_nope[:, h * LANES:(h + 1) * LANES].astype(k_ref.dtype)
        k_ref[:, h * hw + LANES:(h + 1) * hw] = k_rot


def _latent_prep(p_arr, tab, q_gain, kv_gain, q_up, k_up, v_up_t, tm=512):
    t = p_arr.shape[0]
    tm = min(tm, t)
    hw = 2 * LANES

    def const(shape):
        return pl.BlockSpec(shape, lambda i: (0, 0))

    return pl.pallas_call(
        _latent_prep_kernel,
        grid=(t // tm,),
        in_specs=[pl.BlockSpec((tm, D_Q_LORA), lambda i: (i, P_OFF['d_cq'] // D_Q_LORA)),
                  pl.BlockSpec((tm, D_KV_LORA), lambda i: (i, P_OFF['d_ckv'] // D_KV_LORA)),
                  pl.BlockSpec((tm, LANES), lambda i: (i, P_OFF['d_kr'] // LANES)),
                  pl.BlockSpec((tm, LANES), lambda i: (i, 0)),
                  const((1, D_Q_LORA)), const((1, D_KV_LORA)),
                  const((D_Q_LORA, D_HEADS * hw)), const((D_KV_LORA, D_HEADS * LANES)),
                  const((D_HEADS * D_VDIM, D_KV_LORA))],
        out_specs=[pl.BlockSpec((tm, D_HEADS * hw), lambda i: (i, 0)),
                   pl.BlockSpec((tm, D_HEADS * hw), lambda i: (i, 0)),
                   pl.BlockSpec((D_HEADS * D_VDIM, tm), lambda i: (0, i))],
        out_shape=[jax.ShapeDtypeStruct((t, D_HEADS * hw), BF16),
                   jax.ShapeDtypeStruct((t, D_HEADS * hw), BF16),
                   jax.ShapeDtypeStruct((D_HEADS * D_VDIM, t), BF16)],
        compiler_params=_params(("parallel",)),
        name="latent_prep",
    )(p_arr, p_arr, p_arr, tab, q_gain.reshape(1, -1), kv_gain.reshape(1, -1), q_up, k_up, v_up_t)


def _sparse_kernel(q_ref, iq_ref, z_ref, ik_ref, k_ref, vt_ref, wt_ref, o_ref,
                   keys_ref, hi_ref, lo_ref, jcut_ref, m_ref, l_ref, acc_ref, *, topk, ck, slopes, idx_bits):
    i = pl.program_id(1)
    tq = LANES
    nch = ((i + 1) * tq + ck - 1) // ck
    q_pos = i * tq + lax.broadcasted_iota(jnp.int32, (ck, tq), 1)
    s_iota = lax.broadcasted_iota(jnp.int32, (ck, tq), 0)
    lane = lax.broadcasted_iota(jnp.int32, (ck, LANES), 1)
    low = lane < IDX_DIM
    w_all = wt_ref[0] * (IDX_HEADS ** -0.5 * IDX_DIM ** -0.5)

    def score_chunk(c, carry):
        start = pl.multiple_of(c * ck, ck)
        ikc = ik_ref[pl.ds(start, ck), :]
        ik_lo = jnp.where(low, ikc, jnp.zeros_like(ikc))
        ik_hi = jnp.where(low, jnp.zeros_like(ikc), ikc)
        acc = jnp.zeros((ck, tq), F32)
        for pair in range(IDX_HEADS // 2):
            iqp = iq_ref[:, pair * LANES:(pair + 1) * LANES]
            acc = acc + w_all[2 * pair:2 * pair + 1, :] * jnp.maximum(_dot_nt(ik_lo, iqp), 0.0)
            acc = acc + w_all[2 * pair + 1:2 * pair + 2, :] * jnp.maximum(_dot_nt(ik_hi, iqp), 0.0)
        acc = jnp.where(acc == 0.0, 0.0, acc)
        bits = lax.bitcast_convert_type(acc, jnp.int32)
        key = bits ^ ((bits >> 31) & 0x7FFFFFFF)
        key = jnp.where(start + s_iota <= q_pos, key, INT_MIN)
        keys_ref[pl.ds(start, ck), :] = key
        hi_ref[pl.ds(start, ck), :] = (key >> 16).astype(jnp.int16)
        return carry

    lax.fori_loop(0, nch, score_chunk, 0)

    half = 1 << 15
    rows16 = 2 * SUBLANES

    def count16(ref, cand):
        cand16 = cand.astype(jnp.int16)

        def body(c, cnt):
            start = pl.multiple_of(c * ck, ck)
            hit = jnp.where(ref[pl.ds(start, ck), :] >= cand16, jnp.int16(1), jnp.int16(0))
            return cnt + _tree([hit[r:r + rows16, :] for r in range(0, ck, rows16)], jnp.add)
        part = lax.fori_loop(0, nch, body, jnp.zeros((rows16, tq), jnp.int16))
        return jnp.sum(part.astype(jnp.int32).astype(F32), axis=0, keepdims=True)

    def digit_search(ref, target, count_all):
        def step(it, state):
            u, cnt_u = state
            cand = u | (jnp.int32(1) << (15 - it))
            cnt = count16(ref, cand - half)
            ok = cnt >= target
            return jnp.where(ok, cand, u), jnp.where(ok, cnt, cnt_u)
        return lax.fori_loop(0, 16, step, (jnp.zeros((1, tq), jnp.int32), count_all))

    kf = float(topk)
    total = jnp.full((1, tq), 1.0, F32) * (nch * ck).astype(F32)
    hi_u, n_ge_hi = digit_search(hi_ref, kf, total)
    hi_digit = hi_u - half
    n_gt_hi = jnp.where(hi_digit == half - 1, 0.0, count16(hi_ref, jnp.minimum(hi_digit + 1, half - 1)))

    def low_chunk(c, carry):
        start = pl.multiple_of(c * ck, ck)
        key = keys_ref[pl.ds(start, ck), :]
        low16 = (key & 0xFFFF) - half
        lo_ref[pl.ds(start, ck), :] = jnp.where((key >> 16) == hi_digit, low16, -half).astype(jnp.int16)
        return carry

    lax.fori_loop(0, nch, low_chunk, 0)
    lo_u, n_ge_lo = digit_search(lo_ref, kf - n_gt_hi, n_ge_hi - n_gt_hi)
    thr = hi_digit * (1 << 16) + lo_u
    n_ge = n_gt_hi + n_ge_lo
    tied = (n_ge > kf) & (thr != INT_MIN)
    jcut_ref[...] = jnp.full(jcut_ref.shape, 2 ** 30, jnp.int32)

    @pl.when(jnp.max(tied.astype(F32)) > 0.0)
    def _():
        def count(pred_fn):
            def body(c, cnt):
                start = pl.multiple_of(c * ck, ck)
                hit = jnp.where(pred_fn(keys_ref[pl.ds(start, ck), :], start + s_iota), 1.0, 0.0)
                return cnt + _tree([hit[r:r + SUBLANES, :] for r in range(0, ck, SUBLANES)], jnp.add)
            part = lax.fori_loop(0, nch, body, jnp.zeros((SUBLANES, tq), F32))
            return jnp.sum(part, axis=0, keepdims=True)

        need = kf - count(lambda kc, pos: kc > thr)

        def cut_step(it, x):
            cand = x + (jnp.int32(1) << (idx_bits - 1 - it))
            cnt = count(lambda kc, pos: (kc == thr) & (pos < cand))
            return jnp.where(cnt < need, cand, x)
        x = lax.fori_loop(0, idx_bits, cut_step, jnp.zeros((1, tq), jnp.int32))
        jcut_ref[...] = jnp.broadcast_to(jnp.where(tied, x, 2 ** 30), jcut_ref.shape)

    jcut = jcut_ref[0:1, :]

    m_ref[...] = jnp.full(m_ref.shape, NEG, F32)
    l_ref[...] = jnp.zeros(l_ref.shape, F32)
    acc_ref[...] = jnp.zeros(acc_ref.shape, F32)
    rep = A_HEADS // A_KV_HEADS
    q_groups = [jnp.concatenate([q_ref[:, (g * rep + r) * A_DIM:(g * rep + r + 1) * A_DIM] for r in range(rep)],
                                axis=0) for g in range(A_KV_HEADS)]

    def attend_chunk(c, carry):
        start = pl.multiple_of(c * ck, ck)
        kc = keys_ref[pl.ds(start, ck), :]
        pos = start + s_iota
        sel = ((kc > thr) | ((kc == thr) & (pos <= jcut))) & (pos <= q_pos)
        dist = (q_pos - pos).astype(F32)
        logits_g = [_dot_nt(k_ref[pl.ds(start, ck), g * A_DIM:(g + 1) * A_DIM], q_groups[g])
                    for g in range(A_KV_HEADS)]
        for g in range(A_KV_HEADS):
            vtg = vt_ref[0, c, g * A_DIM:(g + 1) * A_DIM, :]
            for r in range(rep):
                h = g * rep + r
                s = logits_g[g][:, r * tq:(r + 1) * tq] - (slopes[h] * LOG2E) * dist
                s = jnp.where(sel, s, MASKED)
                m_prev = m_ref[h:h + 1, :]
                m_new = jnp.maximum(m_prev, _col_reduce(s, jnp.maximum, jnp.max))
                alpha = jnp.exp2(m_prev - m_new)
                p = jnp.exp2(s - m_new)
                l_ref[h:h + 1, :] = alpha * l_ref[h:h + 1, :] + _col_reduce(p, jnp.add, jnp.sum)
                acc_ref[h * A_DIM:(h + 1) * A_DIM, :] = (alpha * acc_ref[h * A_DIM:(h + 1) * A_DIM, :]
                                                         + _dot(vtg, p.astype(BF16)))
                m_ref[h:h + 1, :] = m_new
        return carry

    lax.fori_loop(0, nch, attend_chunk, 0)

    for h in range(A_HEADS):
        out_t = acc_ref[h * A_DIM:(h + 1) * A_DIM, :] / l_ref[h:h + 1, :]
        z = z_ref[:, h * A_DIM:(h + 1) * A_DIM].astype(F32)
        o_ref[:, h * A_DIM:(h + 1) * A_DIM] = (out_t.T * jax.nn.silu(z)).astype(o_ref.dtype)


def _sparse_attention(p_arr, vt, wt, batch, seq):
    tq = LANES
    nq = seq // tq
    ck = min(512, seq)
    topk = min(IDX_TOPK_MAX, seq // 4)
    idx_bits = int(seq).bit_length()
    return pl.pallas_call(
        functools.partial(_sparse_kernel, topk=topk, ck=ck, slopes=_alibi_slopes(A_HEADS), idx_bits=idx_bits),
        grid=(batch, nq),
        in_specs=[pl.BlockSpec((tq, BRANCH_WIDTH), lambda b, i: (b * nq + i, P_OFF['a_q'] // BRANCH_WIDTH)),
                  pl.BlockSpec((tq, BRANCH_WIDTH), lambda b, i: (b * nq + i, P_OFF['a_iq'] // BRANCH_WIDTH)),
                  pl.BlockSpec((tq, BRANCH_WIDTH), lambda b, i: (b * nq + i, P_OFF['a_z'] // BRANCH_WIDTH)),
                  pl.BlockSpec((seq, LANES), lambda b, i: (b, P_OFF['a_ik'] // LANES)),
                  pl.BlockSpec((seq, 2 * LANES), lambda b, i: (b, P_OFF['a_k'] // (2 * LANES))),
                  pl.BlockSpec((1, seq // ck, 2 * LANES, ck), lambda b, i: (b, 0, 0, 0)),
                  pl.BlockSpec((1, IDX_HEADS, tq), lambda b, i: (b, 0, i))],
        out_specs=pl.BlockSpec((tq, BRANCH_WIDTH), lambda b, i: (b * nq + i, 0)),
        out_shape=jax.ShapeDtypeStruct((batch * seq, BRANCH_WIDTH), BF16),
        scratch_shapes=[pltpu.VMEM((seq, tq), jnp.int32), pltpu.VMEM((seq, tq), jnp.int16),
                        pltpu.VMEM((seq, tq), jnp.int16), pltpu.VMEM((8, tq), jnp.int32),
                        pltpu.VMEM((A_HEADS, tq), F32), pltpu.VMEM((A_HEADS, tq), F32),
                        pltpu.VMEM((A_HEADS * A_DIM, tq), F32)],
        compiler_params=_params(("parallel", "arbitrary")),
        name="sparse_attention",
    )(p_arr, p_arr, p_arr, p_arr, p_arr, vt, wt)


def _swap_halves(w):
    half = w.shape[-1] // 2
    return jnp.concatenate([-w[..., half:], w[..., :half]], axis=-1)


def _prepare_in_proj(w_in):
    sizes = [n for _, n in IN_SEGMENTS]
    offs = np.concatenate([[0], np.cumsum(sizes)])
    span = {name: (int(offs[k]), int(offs[k + 1])) for k, (name, _) in enumerate(IN_SEGMENTS)}
    wt32 = jnp.swapaxes(w_in, 1, 2)
    wt16 = wt32.astype(BF16)
    seg16 = lambda name: wt16[:, span[name][0]:span[name][1], :]
    seg32 = lambda name: wt32[:, span[name][0]:span[name][1], :]
    rows = []
    for name, _ in P_LAYOUT:
        if name == 'a_ik':
            rows += [seg16('a_ik'), seg16('a_ik')]
        elif name == 'd_kr':
            kr = seg16('d_kr')
            rows += [kr, -kr[:, D_ROPE // 2:, :], kr[:, :D_ROPE // 2, :]]
        elif name == 'a_q':
            rows.append((seg32('a_q') * (A_DIM ** -0.5 * LOG2E)).astype(BF16))
        elif name == 'b_q':
            rows.append((seg32('b_q') * (B_DIM ** -0.5 * LOG2E)).astype(BF16))
        elif name == 'c_q':
            rows.append((seg32('c_q') * (C_DIM ** -0.5 * LOG2E)).astype(BF16))
        else:
            rows.append(seg16(name))
    w_main_t = jnp.concatenate(rows, axis=1)
    pad = jnp.zeros((w_in.shape[0], LANES - IDX_HEADS - B_HEADS, w_in.shape[1]), BF16)
    w_small_t = jnp.concatenate([seg16('a_iw'), seg16('b_f'), pad], axis=1)
    return w_main_t, w_small_t


def _prepare_latent(dq_up, dkv_up):
    q = dq_up.reshape(D_Q_LORA, D_HEADS, D_NOPE + D_ROPE) * ((D_NOPE + D_ROPE) ** -0.5 * LOG2E)
    rope = q[..., D_NOPE:]
    q_up = jnp.concatenate([q[..., :D_NOPE], rope, _swap_halves(rope)], axis=-1)
    kv = dkv_up.reshape(D_KV_LORA, D_HEADS, D_NOPE + D_VDIM)
    return (q_up.reshape(D_Q_LORA, -1).astype(BF16),
            kv[..., :D_NOPE].reshape(D_KV_LORA, -1).astype(BF16),
            kv[..., D_NOPE:].reshape(D_KV_LORA, -1).T.astype(BF16))


def _rope_table(positions):
    half = D_ROPE // 2
    inv_freq = ROPE_THETA ** (-jnp.arange(half, dtype=F32) / half)
    ang = positions.astype(F32)[..., None] * inv_freq
    cos, sin = jnp.cos(ang), jnp.sin(ang)
    tab = jnp.concatenate([cos, cos, sin, sin], axis=-1)
    return tab.reshape(-1, 4 * half)


def _layer(layer, x, xb, tab, batch, seq, w_main, w_small, w_gate, w_branch, w_out, dq_gain, dq_up, dkv_gain,
           dkv_up, f_bias, sinks, ln_gain, ln_bias, alpha):
    proj = _matmul(xb, w_main, layer, BF16, transposed=True)
    small = _matmul(xb, w_small, layer, F32, transposed=True, tn=LANES)

    ck = min(512, seq)
    wt = small[:, :IDX_HEADS].reshape(batch, seq, IDX_HEADS).transpose(0, 2, 1)
    a_v = proj[:, P_OFF['a_v']:P_OFF['a_v'] + 2 * LANES]
    vt = a_v.reshape(batch, seq // ck, ck, 2 * LANES).transpose(0, 1, 3, 2)
    o_a = _sparse_attention(proj, vt, wt, batch, seq)

    f_rows = small[:, IDX_HEADS:IDX_HEADS + B_HEADS].reshape(batch, seq, B_HEADS).transpose(0, 2, 1)
    cum = _forget_cumsum(f_rows.reshape(batch * B_HEADS, seq),
                         jnp.tile(f_bias.astype(F32), batch).reshape(batch * B_HEADS, 1))
    cum = cum.reshape(batch, B_HEADS, seq).transpose(0, 2, 1).reshape(batch * seq, B_HEADS)
    cum = jnp.pad(cum, ((0, 0), (0, LANES - B_HEADS)))
    vt_b = proj[:, P_OFF['b_v']:P_OFF['b_v'] + BRANCH_WIDTH].T
    o_b = _flash_attention(proj, P_OFF['b_q'] // BRANCH_WIDTH, proj, P_OFF['b_k'] // BRANCH_WIDTH, vt_b,
                           proj, P_OFF['b_z'] // BRANCH_WIDTH, batch=batch, seq=seq, heads=B_HEADS,
                           dqk=B_DIM, dv=B_DIM, cum=cum)

    o_c = _sliding_window(proj, sinks, batch, seq)

    q_up, k_up, v_up_t = _prepare_latent(dq_up, dkv_up)
    q_d, k_d, vt_d = _latent_prep(proj, tab, dq_gain, dkv_gain, q_up, k_up, v_up_t)
    o_d = _flash_attention(q_d, 0, k_d, 0, vt_d, proj, P_OFF['d_z'] // BRANCH_WIDTH, batch=batch, seq=seq,
                           heads=D_HEADS, dqk=2 * LANES, dv=D_VDIM)

    merged = _gated_merge(xb, w_gate, (o_a, o_b, o_c, o_d), w_branch, layer)
    y = _matmul(merged, w_out, layer, F32)
    return _residual_layernorm(x, y, ln_gain, ln_bias, alpha)


def kernel(x, positions, w_in, w_gate, w_branch, w_out, dq_gain, dq_up, dkv_gain, dkv_up, f_bias, sinks,
           ln_gain, ln_bias):
    batch, seq, d = x.shape
    depth = w_in.shape[0]
    alpha = (2 * depth) ** 0.25
    tab = _rope_table(positions)
    xf = x.reshape(batch * seq, d)
    xb = _cast_bf16(xf)
    w_main, w_small = _prepare_in_proj(w_in)
    w_gate, w_branch, w_out = w_gate.astype(BF16), w_branch.astype(BF16), w_out.astype(BF16)
    for l in range(depth):
        xf, xb = _layer(l, xf, xb, tab, batch, seq, w_main, w_small, w_gate, w_branch, w_out, dq_gain[l],
                        dq_up[l], dkv_gain[l], dkv_up[l], f_bias[l], sinks[l], ln_gain[l], ln_bias[l], alpha)
    return xf.reshape(batch, seq, d)
```

```python
import functools

import numpy as np
import jax
import jax.numpy as jnp
from jax import lax
from jax.experimental import pallas as pl
from jax.experimental.pallas import tpu as pltpu

BRANCH_WIDTH = 1024
A_HEADS, A_KV_HEADS, A_DIM = 8, 2, 128
IDX_HEADS, IDX_DIM, IDX_TOPK_MAX = 16, 64, 256
B_HEADS, B_DIM = 8, 128
C_HEADS, C_KV_HEADS, C_DIM, WINDOW = 16, 2, 64, 128
D_HEADS, D_Q_LORA, D_KV_LORA, D_NOPE, D_ROPE, D_VDIM = 8, 768, 256, 128, 64, 128
ROPE_THETA = 10000.0
RMS_EPS = 1e-6
LN_EPS = 1e-5

IN_SEGMENTS = (
    ('a_q', 1024), ('a_k', 256), ('a_v', 256), ('a_iq', 1024), ('a_ik', 64), ('a_iw', 16), ('a_z', 1024),
    ('b_q', 1024), ('b_k', 1024), ('b_v', 1024), ('b_f', 8), ('b_z', 1024),
    ('c_q', 1024), ('c_k', 128), ('c_v', 128), ('c_z', 1024),
    ('d_cq', 768), ('d_ckv', 256), ('d_kr', 64), ('d_z', 1024),
)

P_LAYOUT = (
    ('d_cq', 768), ('d_ckv', 256), ('a_q', 1024), ('a_iq', 1024), ('a_z', 1024),
    ('b_q', 1024), ('b_k', 1024), ('b_v', 1024), ('b_z', 1024), ('c_q', 1024), ('c_z', 1024),
    ('d_z', 1024), ('a_k', 256), ('a_v', 256), ('a_ik', 128), ('d_kr', 128), ('c_k', 128), ('c_v', 128),
)
P_OFF = {}
_o = 0
for _n, _w in P_LAYOUT:
    assert _o % _w == 0
    P_OFF[_n] = _o
    _o += _w
P_WIDTH = _o

LANES = 128
SUBLANES = 8
NEG = -1e30
FAR = 1e34
LOG2E = 1.4426950408889634
INT_MIN = -2 ** 31
VMEM_LIMIT = 56 * 1024 * 1024

F32 = jnp.float32
BF16 = jnp.bfloat16


def _params(sem, vmem=VMEM_LIMIT):
    return pltpu.CompilerParams(dimension_semantics=sem, vmem_limit_bytes=vmem)


def _dot(a, b):
    return jnp.dot(a, b, preferred_element_type=F32)


def _dot_nt(a, b):
    return lax.dot_general(a, b, (((1,), (1,)), ((), ())), preferred_element_type=F32)


def _alibi_slopes(n_heads):
    return [float(np.float32(2.0 ** (-8.0 * (h + 1) / n_heads))) for h in range(n_heads)]


def _mm_kernel(x_ref, w_ref, o_ref):
    o_ref[...] = _dot(x_ref[...], w_ref[0]).astype(o_ref.dtype)


def _mm_nt_kernel(x_ref, wt_ref, o_ref):
    o_ref[...] = _dot_nt(x_ref[...], wt_ref[0]).astype(o_ref.dtype)


def _matmul(x, w, layer, out_dtype, transposed=False, tm=1024, tn=1024):
    m, k = x.shape
    n = w.shape[1] if transposed else w.shape[2]
    tm, tn = min(tm, m), min(tn, n)
    if transposed:
        body, w_spec = _mm_nt_kernel, pl.BlockSpec((1, tn, k), lambda i, j: (layer, j, 0))
    else:
        body, w_spec = _mm_kernel, pl.BlockSpec((1, k, tn), lambda i, j: (layer, 0, j))
    return pl.pallas_call(
        body,
        grid=(m // tm, n // tn),
        in_specs=[pl.BlockSpec((tm, k), lambda i, j: (i, 0)), w_spec],
        out_specs=pl.BlockSpec((tm, tn), lambda i, j: (i, j)),
        out_shape=jax.ShapeDtypeStruct((m, n), out_dtype),
        compiler_params=_params(("parallel", "parallel")),
        name="dense_matmul",
    )(x, w)


def _cast_kernel(x_ref, o_ref):
    o_ref[...] = x_ref[...].astype(o_ref.dtype)


def _cast_bf16(x, tm=512):
    t, d = x.shape
    tm = min(tm, t)
    row = pl.BlockSpec((tm, d), lambda i: (i, 0))
    return pl.pallas_call(
        _cast_kernel, grid=(t // tm,), in_specs=[row], out_specs=row,
        out_shape=jax.ShapeDtypeStruct((t, d), BF16),
        compiler_params=_params(("parallel",)),
        name="cast_bf16",
    )(x)


def _merge_kernel(x_ref, wg_ref, b0_ref, b1_ref, b2_ref, b3_ref, wb_ref, o_ref, acc_ref, *, sub):
    n = pl.program_id(2)

    @pl.when((pl.program_id(0) == 0) & (pl.program_id(1) == 0) & (n == 0))
    def _():
        acc_ref[...] = jnp.zeros(acc_ref.shape, F32)

    branch = jnp.where(n == 0, b0_ref[...], jnp.where(n == 1, b1_ref[...],
                                                      jnp.where(n == 2, b2_ref[...], b3_ref[...])))
    x = x_ref[...]
    for c in range(o_ref.shape[1] // sub):
        cols = slice(c * sub, (c + 1) * sub)
        gate = jax.nn.sigmoid(_dot(x, wg_ref[0, 0, :, cols]))
        contrib = gate * _dot(branch, wb_ref[0, 0, :, cols])
        acc = jnp.where(n == 0, 0.0, acc_ref[:, cols]) + contrib
        acc_ref[:, cols] = acc
        o_ref[:, cols] = acc.astype(o_ref.dtype)


def _gated_merge(xb, wg, branches, wb, layer, tm=512, tn=1024, sub=256):
    t, d = xb.shape
    tm, tn = min(tm, t), min(tn, d)
    sub = min(sub, tn)
    bw = branches[0].shape[1]
    bspec = pl.BlockSpec((tm, bw), lambda i, j, n: (i, 0))
    return pl.pallas_call(
        functools.partial(_merge_kernel, sub=sub),
        grid=(t // tm, d // tn, 4),
        in_specs=[pl.BlockSpec((tm, d), lambda i, j, n: (i, 0)),
                  pl.BlockSpec((1, 1, d, tn), lambda i, j, n: (layer, n, 0, j)),
                  bspec, bspec, bspec, bspec,
                  pl.BlockSpec((1, 1, bw, tn), lambda i, j, n: (layer, n, 0, j))],
        out_specs=pl.BlockSpec((tm, tn), lambda i, j, n: (i, j)),
        out_shape=jax.ShapeDtypeStruct((t, d), BF16),
        scratch_shapes=[pltpu.VMEM((tm, tn), F32)],
        compiler_params=_params(("arbitrary", "arbitrary", "arbitrary")),
        name="gated_merge",
    )(xb, wg, *branches, wb)


def _ln_kernel(x_ref, y_ref, g_ref, b_ref, o_ref, ob_ref, *, alpha):
    r = alpha * x_ref[...] + y_ref[...]
    mu = jnp.mean(r, axis=-1, keepdims=True)
    c = r - mu
    var = jnp.mean(c * c, axis=-1, keepdims=True)
    out = c * lax.rsqrt(var + LN_EPS) * g_ref[...] + b_ref[...]
    o_ref[...] = out
    ob_ref[...] = out.astype(BF16)


def _residual_layernorm(x, y, gain, bias, alpha, tm=256):
    t, d = x.shape
    tm = min(tm, t)
    row = pl.BlockSpec((tm, d), lambda i: (i, 0))
    vec = pl.BlockSpec((1, d), lambda i: (0, 0))
    return pl.pallas_call(
        functools.partial(_ln_kernel, alpha=alpha),
        grid=(t // tm,),
        in_specs=[row, row, vec, vec],
        out_specs=[row, row],
        out_shape=[jax.ShapeDtypeStruct((t, d), F32), jax.ShapeDtypeStruct((t, d), BF16)],
        compiler_params=_params(("parallel",)),
        name="residual_layernorm",
    )(x, y, gain.reshape(1, d), bias.reshape(1, d))


def _forget_cumsum_kernel(f_ref, bias_ref, c_ref):
    rows, s = f_ref.shape
    lane = lax.broadcasted_iota(jnp.int32, (rows, LANES), 1)
    carry = jnp.zeros((rows, 1), F32)
    for c in range(s // LANES):
        x = jax.nn.log_sigmoid(f_ref[:, c * LANES:(c + 1) * LANES] + bias_ref[...])
        shift = 1
        while shift < LANES:
            x = x + jnp.where(lane >= shift, pltpu.roll(x, shift, 1), 0.0)
            shift *= 2
        x = x + carry
        c_ref[:, c * LANES:(c + 1) * LANES] = x * LOG2E
        carry = x[:, LANES - 1:LANES]


def _forget_cumsum(f_rows, bias_rows):
    rows, s = f_rows.shape
    full = pl.BlockSpec((rows, s), lambda: (0, 0))
    return pl.pallas_call(
        _forget_cumsum_kernel,
        in_specs=[full, pl.BlockSpec((rows, 1), lambda: (0, 0))],
        out_specs=full,
        out_shape=jax.ShapeDtypeStruct((rows, s), F32),
        name="forget_cumsum",
    )(f_rows, bias_rows)


def _tree(parts, op, ways=8):
    parts = list(parts)
    if len(parts) > ways:
        accs = parts[:ways]
        for a in range(ways, len(parts)):
            accs[a % ways] = op(accs[a % ways], parts[a])
        parts = accs
    while len(parts) > 1:
        nxt = [op(parts[a], parts[a + 1]) for a in range(0, len(parts) - 1, 2)]
        if len(parts) % 2:
            nxt.append(parts[-1])
        parts = nxt
    return parts[0]


def _col_reduce(x, op, reduce_fn):
    rows = x.shape[0]
    part = _tree([x[r:r + SUBLANES, :] for r in range(0, rows, SUBLANES)], op)
    return reduce_fn(part, axis=0, keepdims=True)


def _flash_kernel(i_tab, j_tab, *refs, fox, heads, dqk, dv, tile):
    if fox:
        q_ref, k_ref, vt_ref, ck_ref, z_ref, o_ref, m_sc, l_sc, acc_sc = refs
    else:
        q_ref, k_ref, vt_ref, z_ref, o_ref, m_sc, l_sc, acc_sc = refs
    pair = pl.program_id(1)
    i = i_tab[pair]
    j = j_tab[pair]

    @pl.when(j == 0)
    def _():
        m_sc[...] = jnp.full(m_sc.shape, NEG, F32)
        l_sc[...] = jnp.zeros(l_sc.shape, F32)
        acc_sc[...] = jnp.zeros(acc_sc.shape, F32)

    def step(masked):
        if masked:
            causal = (lax.broadcasted_iota(jnp.int32, (tile, tile), 0)
                      <= lax.broadcasted_iota(jnp.int32, (tile, tile), 1))
        def logits(h):
            return _dot_nt(k_ref[:, h * dqk:(h + 1) * dqk], q_ref[:, h * dqk:(h + 1) * dqk])

        s_next = logits(0)
        for h in range(heads):
            s = s_next
            if h + 1 < heads:
                s_next = logits(h + 1)
            if fox:
                s = s - ck_ref[:, h:h + 1]
            if masked:
                s = jnp.where(causal, s, NEG)
            m_prev = m_sc[h:h + 1, :]
            m_new = jnp.maximum(m_prev, _col_reduce(s, jnp.maximum, jnp.max))
            alpha = jnp.exp2(m_prev - m_new)
            p = jnp.exp2(s - m_new)
            l_sc[h:h + 1, :] = alpha * l_sc[h:h + 1, :] + _col_reduce(p, jnp.add, jnp.sum)
            acc_sc[h * dv:(h + 1) * dv, :] = (alpha * acc_sc[h * dv:(h + 1) * dv, :]
                                              + _dot(vt_ref[h * dv:(h + 1) * dv, :], p.astype(BF16)))
            m_sc[h:h + 1, :] = m_new

    @pl.when(j < i)
    def _():
        step(False)

    @pl.when(j == i)
    def _():
        step(True)
        for h in range(heads):
            out_t = acc_sc[h * dv:(h + 1) * dv, :] / l_sc[h:h + 1, :]
            z = z_ref[:, h * dv:(h + 1) * dv].astype(F32)
            o_ref[:, h * dv:(h + 1) * dv] = (out_t.T * jax.nn.silu(z)).astype(o_ref.dtype)


def _flash_attention(q_arr, q_blk, k_arr, k_blk, vt_arr, z_arr, z_blk, *, batch, seq, heads,
                     dqk, dv, cum=None, tile=512):
    tile = min(tile, seq)
    nq = seq // tile
    fox = cum is not None
    pairs = [(i, j) for i in range(nq) for j in range(i + 1)]
    i_tab = jnp.asarray([p[0] for p in pairs], jnp.int32)
    j_tab = jnp.asarray([p[1] for p in pairs], jnp.int32)
    in_specs = [
        pl.BlockSpec((tile, heads * dqk), lambda b, p, it, jt: (b * nq + it[p], q_blk)),
        pl.BlockSpec((tile, heads * dqk), lambda b, p, it, jt: (b * nq + jt[p], k_blk)),
        pl.BlockSpec((heads * dv, tile), lambda b, p, it, jt: (0, b * nq + jt[p])),
    ]
    args = [q_arr, k_arr, vt_arr]
    if fox:
        in_specs.append(pl.BlockSpec((tile, LANES), lambda b, p, it, jt: (b * nq + jt[p], 0)))
        args.append(cum)
    in_specs.append(pl.BlockSpec((tile, heads * dv), lambda b, p, it, jt: (b * nq + it[p], z_blk)))
    args.append(z_arr)
    return pl.pallas_call(
        functools.partial(_flash_kernel, fox=fox, heads=heads, dqk=dqk, dv=dv, tile=tile),
        grid_spec=pltpu.PrefetchScalarGridSpec(
            num_scalar_prefetch=2,
            grid=(batch, len(pairs)),
            in_specs=in_specs,
            out_specs=pl.BlockSpec((tile, heads * dv), lambda b, p, it, jt: (b * nq + it[p], 0)),
            scratch_shapes=[pltpu.VMEM((heads, tile), F32), pltpu.VMEM((heads, tile), F32),
                            pltpu.VMEM((heads * dv, tile), F32)]),
        out_shape=jax.ShapeDtypeStruct((batch * seq, heads * dv), BF16),
        compiler_params=_params(("parallel", "arbitrary")),
        name="flash_fox" if fox else "flash_latent",
    )(i_tab, j_tab, *args)


def _swa_kernel(q_ref, kvc_ref, kvp_ref, z_ref, sink_ref, o_ref, *, slopes):
    i = pl.program_id(1)
    w = WINDOW
    pairs_per_group = C_HEADS // C_KV_HEADS // 2
    kv = jnp.concatenate([kvp_ref[...], kvc_ref[...]], axis=0).astype(F32)
    lane = lax.broadcasted_iota(jnp.int32, (2 * w, LANES), 1)
    low = lane < C_DIM

    def halves(t):
        g0_lo = jnp.where(low, t, 0.0)
        g1_hi = jnp.where(low, 0.0, t)
        return ((g0_lo, pltpu.roll(g0_lo, C_DIM, 1)), (pltpu.roll(g1_hi, C_DIM, 1), g1_hi))

    k_half = halves(kv[:, :LANES])
    v_half = halves(kv[:, LANES:])
    key = lax.broadcasted_iota(jnp.int32, (2 * w, w), 0)
    qry = lax.broadcasted_iota(jnp.int32, (2 * w, w), 1)
    dist = qry + w - key
    first_key = jnp.where(i > 0, 0, w)
    valid = (dist >= 0) & (dist < w) & (key >= first_key)
    dist_m = jnp.where(valid, dist.astype(F32), -NEG)

    q_rows = [jnp.concatenate([q_ref[:, (g * pairs_per_group + jj) * LANES:(g * pairs_per_group + jj + 1) * LANES]
                               for jj in range(pairs_per_group)], axis=0) for g in range(C_KV_HEADS)]
    logits = [[_dot_nt(k_half[g][half].astype(BF16), q_rows[g]) for half in range(2)] for g in range(C_KV_HEADS)]
    for g in range(C_KV_HEADS):
        out_t = None
        for half in range(2):
            probs, inv = [], []
            for jj in range(pairs_per_group):
                h = 2 * (g * pairs_per_group + jj) + half
                s = logits[g][half][:, jj * w:(jj + 1) * w] - (slopes[h] * LOG2E) * dist_m
                sink = jnp.full((1, w), sink_ref[h], F32) * LOG2E
                m = jnp.maximum(_col_reduce(s, jnp.maximum, jnp.max), sink)
                p = jnp.exp2(s - m)
                inv.append(1.0 / (_col_reduce(p, jnp.add, jnp.sum) + jnp.exp2(sink - m)))
                probs.append(p.astype(BF16))
            vt = v_half[g][half].T.astype(BF16)
            contrib = _dot(vt, jnp.concatenate(probs, axis=1)) * jnp.concatenate(inv, axis=1)
            out_t = contrib if out_t is None else out_t + contrib
        for jj in range(pairs_per_group):
            cols = slice((g * pairs_per_group + jj) * LANES, (g * pairs_per_group + jj + 1) * LANES)
            z = z_ref[:, cols].astype(F32)
            o_ref[:, cols] = (out_t[:, jj * w:(jj + 1) * w].T * jax.nn.silu(z)).astype(o_ref.dtype)


def _sliding_window(p_arr, sinks, batch, seq):
    w = WINDOW
    nb = seq // w
    kv_blk = P_OFF['c_k'] // (2 * LANES)
    return pl.pallas_call(
        functools.partial(_swa_kernel, slopes=_alibi_slopes(C_HEADS)),
        grid=(batch, nb),
        in_specs=[pl.BlockSpec((w, BRANCH_WIDTH), lambda b, i: (b * nb + i, P_OFF['c_q'] // BRANCH_WIDTH)),
                  pl.BlockSpec((w, 2 * LANES), lambda b, i: (b * nb + i, kv_blk)),
                  pl.BlockSpec((w, 2 * LANES), lambda b, i: (b * nb + jnp.maximum(i - 1, 0), kv_blk)),
                  pl.BlockSpec((w, BRANCH_WIDTH), lambda b, i: (b * nb + i, P_OFF['c_z'] // BRANCH_WIDTH)),
                  pl.BlockSpec(memory_space=pltpu.SMEM)],
        out_specs=pl.BlockSpec((w, BRANCH_WIDTH), lambda b, i: (b * nb + i, 0)),
        out_shape=jax.ShapeDtypeStruct((batch * seq, BRANCH_WIDTH), BF16),
        compiler_params=_params(("parallel", "parallel")),
        name="sliding_window",
    )(p_arr, p_arr, p_arr, p_arr, sinks.astype(F32))


def _latent_prep_kernel(cq_ref, ckv_ref, kr_ref, tab_ref, qg_ref, kvg_ref, qup_ref, kup_ref, vupt_ref,
                        q_ref, k_ref, vt_ref):
    def rms(x, g):
        return x * lax.rsqrt(jnp.mean(x * x, axis=-1, keepdims=True) + RMS_EPS) * g

    tab = tab_ref[...]
    lane = lax.broadcasted_iota(jnp.int32, tab.shape, 1)

    def rotate(t):
        r = t * tab
        return r + pltpu.roll(r, D_ROPE, 1)

    q = _dot(rms(cq_ref[...].astype(F32), qg_ref[...]).astype(BF16), qup_ref[...])
    ckv = rms(ckv_ref[...].astype(F32), kvg_ref[...]).astype(BF16)
    k_nope = _dot(ckv, kup_ref[...])
    vt_ref[...] = _dot_nt(vupt_ref[...], ckv).astype(vt_ref.dtype)
    k_rot = jnp.where(lane < D_ROPE, rotate(kr_ref[...].astype(F32)), 0.0).astype(k_ref.dtype)
    hw = 2 * LANES
    for h in range(D_HEADS):
        q_ref[:, h * hw:h * hw + LANES] = q[:, h * hw:h * hw + LANES].astype(q_ref.dtype)
        q_ref[:, h * hw + LANES:(h + 1) * hw] = rotate(q[:, h * hw + LANES:(h + 1) * hw]).astype(q_ref.dtype)
        k_ref[:, h * hw:h * hw + LANES] = k_nope[:, h * LANES:(h + 1) * LANES].astype(k_ref.dtype)
        k_ref[:, h * hw + LANES:(h + 1) * hw] = k_rot


def _latent_prep(p_arr, tab, q_gain, kv_gain, q_up, k_up, v_up_t, tm=512):
    t = p_arr.shape[0]
    tm = min(tm, t)
    hw = 2 * LANES

    def const(shape):
        return pl.BlockSpec(shape, lambda i: (0, 0))

    return pl.pallas_call(
        _latent_prep_kernel,
        grid=(t // tm,),
        in_specs=[pl.BlockSpec((tm, D_Q_LORA), lambda i: (i, P_OFF['d_cq'] // D_Q_LORA)),
                  pl.BlockSpec((tm, D_KV_LORA), lambda i: (i, P_OFF['d_ckv'] // D_KV_LORA)),
                  pl.BlockSpec((tm, LANES), lambda i: (i, P_OFF['d_kr'] // LANES)),
                  pl.BlockSpec((tm, LANES), lambda i: (i, 0)),
                  const((1, D_Q_LORA)), const((1, D_KV_LORA)),
                  const((D_Q_LORA, D_HEADS * hw)), const((D_KV_LORA, D_HEADS * LANES)),
                  const((D_HEADS * D_VDIM, D_KV_LORA))],
        out_specs=[pl.BlockSpec((tm, D_HEADS * hw), lambda i: (i, 0)),
                   pl.BlockSpec((tm, D_HEADS * hw), lambda i: (i, 0)),
                   pl.BlockSpec((D_HEADS * D_VDIM, tm), lambda i: (0, i))],
        out_shape=[jax.ShapeDtypeStruct((t, D_HEADS * hw), BF16),
                   jax.ShapeDtypeStruct((t, D_HEADS * hw), BF16),
                   jax.ShapeDtypeStruct((D_HEADS * D_VDIM, t), BF16)],
        compiler_params=_params(("parallel",)),
        name="latent_prep",
    )(p_arr, p_arr, p_arr, tab, q_gain.reshape(1, -1), kv_gain.reshape(1, -1), q_up, k_up, v_up_t)


def _sparse_kernel(q_ref, iq_ref, z_ref, ik_ref, k_ref, vt_ref, wt_ref, o_ref,
                   keys_ref, jcut_ref, m_ref, l_ref, acc_ref, *, topk, ck, slopes, idx_bits):
    i = pl.program_id(1)
    tq = LANES
    nch = ((i + 1) * tq + ck - 1) // ck
    q_pos = i * tq + lax.broadcasted_iota(jnp.int32, (ck, tq), 1)
    s_iota = lax.broadcasted_iota(jnp.int32, (ck, tq), 0)
    lane = lax.broadcasted_iota(jnp.int32, (ck, LANES), 1)
    low = lane < IDX_DIM
    w_all = wt_ref[0] * (IDX_HEADS ** -0.5 * IDX_DIM ** -0.5)

    def score_chunk(c, carry):
        start = pl.multiple_of(c * ck, ck)
        ikc = ik_ref[pl.ds(start, ck), :]
        ik_lo = jnp.where(low, ikc, jnp.zeros_like(ikc))
        ik_hi = jnp.where(low, jnp.zeros_like(ikc), ikc)
        acc = jnp.zeros((ck, tq), F32)
        for pair in range(IDX_HEADS // 2):
            iqp = iq_ref[:, pair * LANES:(pair + 1) * LANES]
            acc = acc + w_all[2 * pair:2 * pair + 1, :] * jnp.maximum(_dot_nt(ik_lo, iqp), 0.0)
            acc = acc + w_all[2 * pair + 1:2 * pair + 2, :] * jnp.maximum(_dot_nt(ik_hi, iqp), 0.0)
        acc = jnp.where(acc == 0.0, 0.0, acc)
        bits = lax.bitcast_convert_type(acc, jnp.int32)
        key = bits ^ ((bits >> 31) & 0x7FFFFFFF)
        key = jnp.where(start + s_iota <= q_pos, key, INT_MIN)
        keys_ref[pl.ds(start, ck), :] = key
        return carry

    lax.fori_loop(0, nch, score_chunk, 0)

    def count(pred_fn):
        def body(c, cnt):
            start = pl.multiple_of(c * ck, ck)
            hit = jnp.where(pred_fn(keys_ref[pl.ds(start, ck), :], start + s_iota), 1.0, 0.0)
            return cnt + _tree([hit[r:r + SUBLANES, :] for r in range(0, ck, SUBLANES)], jnp.add)
        part = lax.fori_loop(0, nch, body, jnp.zeros((SUBLANES, tq), F32))
        return jnp.sum(part, axis=0, keepdims=True)

    kf = float(topk)

    def thr_step(it, state):
        thr, n_ge = state
        cand = thr ^ (jnp.int32(1) << (31 - it))
        cnt = count(lambda kc, pos: kc >= cand)
        ok = cnt >= kf
        return jnp.where(ok, cand, thr), jnp.where(ok, cnt, n_ge)

    total = jnp.full((1, tq), 1.0, F32) * (nch * ck).astype(F32)
    thr, n_ge = lax.fori_loop(0, 32, thr_step, (jnp.full((1, tq), INT_MIN, jnp.int32), total))
    tied = (n_ge > kf) & (thr != INT_MIN)
    jcut_ref[...] = jnp.full(jcut_ref.shape, 2 ** 30, jnp.int32)

    @pl.when(jnp.max(tied.astype(F32)) > 0.0)
    def _():
        need = kf - count(lambda kc, pos: kc > thr)

        def cut_step(it, x):
            cand = x + (jnp.int32(1) << (idx_bits - 1 - it))
            cnt = count(lambda kc, pos: (kc == thr) & (pos < cand))
            return jnp.where(cnt < need, cand, x)
        x = lax.fori_loop(0, idx_bits, cut_step, jnp.zeros((1, tq), jnp.int32))
        jcut_ref[...] = jnp.broadcast_to(jnp.where(tied, x, 2 ** 30), jcut_ref.shape)

    jcut = jcut_ref[0:1, :]

    m_ref[...] = jnp.full(m_ref.shape, NEG, F32)
    l_ref[...] = jnp.zeros(l_ref.shape, F32)
    acc_ref[...] = jnp.zeros(acc_ref.shape, F32)
    rep = A_HEADS // A_KV_HEADS
    q_groups = [jnp.concatenate([q_ref[:, (g * rep + r) * A_DIM:(g * rep + r + 1) * A_DIM] for r in range(rep)],
                                axis=0) for g in range(A_KV_HEADS)]

    def attend_chunk(c, carry):
        start = pl.multiple_of(c * ck, ck)
        kc = keys_ref[pl.ds(start, ck), :]
        pos = start + s_iota
        sel = ((kc > thr) | ((kc == thr) & (pos <= jcut))) & (pos <= q_pos)
        dist = jnp.where(sel, (q_pos - pos).astype(F32), FAR)
        logits_g = [_dot_nt(k_ref[pl.ds(start, ck), g * A_DIM:(g + 1) * A_DIM], q_groups[g])
                    for g in range(A_KV_HEADS)]
        for g in range(A_KV_HEADS):
            vtg = vt_ref[0, c, g * A_DIM:(g + 1) * A_DIM, :]
            for r in range(rep):
                h = g * rep + r
                s = logits_g[g][:, r * tq:(r + 1) * tq] - (slopes[h] * LOG2E) * dist
                m_prev = m_ref[h:h + 1, :]
                m_new = jnp.maximum(m_prev, _col_reduce(s, jnp.maximum, jnp.max))
                alpha = jnp.exp2(m_prev - m_new)
                p = jnp.exp2(s - m_new)
                l_ref[h:h + 1, :] = alpha * l_ref[h:h + 1, :] + _col_reduce(p, jnp.add, jnp.sum)
                acc_ref[h * A_DIM:(h + 1) * A_DIM, :] = (alpha * acc_ref[h * A_DIM:(h + 1) * A_DIM, :]
                                                         + _dot(vtg, p.astype(BF16)))
                m_ref[h:h + 1, :] = m_new
        return carry

    lax.fori_loop(0, nch, attend_chunk, 0)

    for h in range(A_HEADS):
        out_t = acc_ref[h * A_DIM:(h + 1) * A_DIM, :] / l_ref[h:h + 1, :]
        z = z_ref[:, h * A_DIM:(h + 1) * A_DIM].astype(F32)
        o_ref[:, h * A_DIM:(h + 1) * A_DIM] = (out_t.T * jax.nn.silu(z)).astype(o_ref.dtype)


def _sparse_attention(p_arr, vt, wt, batch, seq):
    tq = LANES
    nq = seq // tq
    ck = min(512, seq)
    topk = min(IDX_TOPK_MAX, seq // 4)
    idx_bits = int(seq).bit_length()
    return pl.pallas_call(
        functools.partial(_sparse_kernel, topk=topk, ck=ck, slopes=_alibi_slopes(A_HEADS), idx_bits=idx_bits),
        grid=(batch, nq),
        in_specs=[pl.BlockSpec((tq, BRANCH_WIDTH), lambda b, i: (b * nq + i, P_OFF['a_q'] // BRANCH_WIDTH)),
                  pl.BlockSpec((tq, BRANCH_WIDTH), lambda b, i: (b * nq + i, P_OFF['a_iq'] // BRANCH_WIDTH)),
                  pl.BlockSpec((tq, BRANCH_WIDTH), lambda b, i: (b * nq + i, P_OFF['a_z'] // BRANCH_WIDTH)),
                  pl.BlockSpec((seq, LANES), lambda b, i: (b, P_OFF['a_ik'] // LANES)),
                  pl.BlockSpec((seq, 2 * LANES), lambda b, i: (b, P_OFF['a_k'] // (2 * LANES))),
                  pl.BlockSpec((1, seq // ck, 2 * LANES, ck), lambda b, i: (b, 0, 0, 0)),
                  pl.BlockSpec((1, IDX_HEADS, tq), lambda b, i: (b, 0, i))],
        out_specs=pl.BlockSpec((tq, BRANCH_WIDTH), lambda b, i: (b * nq + i, 0)),
        out_shape=jax.ShapeDtypeStruct((batch * seq, BRANCH_WIDTH), BF16),
        scratch_shapes=[pltpu.VMEM((seq, tq), jnp.int32), pltpu.VMEM((8, tq), jnp.int32),
                        pltpu.VMEM((A_HEADS, tq), F32), pltpu.VMEM((A_HEADS, tq), F32),
                        pltpu.VMEM((A_HEADS * A_DIM, tq), F32)],
        compiler_params=_params(("parallel", "arbitrary")),
        name="sparse_attention",
    )(p_arr, p_arr, p_arr, p_arr, p_arr, vt, wt)


def _swap_halves(w):
    half = w.shape[-1] // 2
    return jnp.concatenate([-w[..., half:], w[..., :half]], axis=-1)


def _prepare_in_proj(w_in):
    sizes = [n for _, n in IN_SEGMENTS]
    offs = np.concatenate([[0], np.cumsum(sizes)])
    span = {name: (int(offs[k]), int(offs[k + 1])) for k, (name, _) in enumerate(IN_SEGMENTS)}
    wt32 = jnp.swapaxes(w_in, 1, 2)
    wt16 = wt32.astype(BF16)
    seg16 = lambda name: wt16[:, span[name][0]:span[name][1], :]
    seg32 = lambda name: wt32[:, span[name][0]:span[name][1], :]
    rows = []
    for name, _ in P_LAYOUT:
        if name == 'a_ik':
            rows += [seg16('a_ik'), seg16('a_ik')]
        elif name == 'd_kr':
            kr = seg16('d_kr')
            rows += [kr, -kr[:, D_ROPE // 2:, :], kr[:, :D_ROPE // 2, :]]
        elif name == 'a_q':
            rows.append((seg32('a_q') * (A_DIM ** -0.5 * LOG2E)).astype(BF16))
        elif name == 'b_q':
            rows.append((seg32('b_q') * (B_DIM ** -0.5 * LOG2E)).astype(BF16))
        elif name == 'c_q':
            rows.append((seg32('c_q') * (C_DIM ** -0.5 * LOG2E)).astype(BF16))
        else:
            rows.append(seg16(name))
    w_main_t = jnp.concatenate(rows, axis=1)
    pad = jnp.zeros((w_in.shape[0], LANES - IDX_HEADS - B_HEADS, w_in.shape[1]), BF16)
    w_small_t = jnp.concatenate([seg16('a_iw'), seg16('b_f'), pad], axis=1)
    return w_main_t, w_small_t


def _prepare_latent(dq_up, dkv_up):
    q = dq_up.reshape(D_Q_LORA, D_HEADS, D_NOPE + D_ROPE) * ((D_NOPE + D_ROPE) ** -0.5 * LOG2E)
    rope = q[..., D_NOPE:]
    q_up = jnp.concatenate([q[..., :D_NOPE], rope, _swap_halves(rope)], axis=-1)
    kv = dkv_up.reshape(D_KV_LORA, D_HEADS, D_NOPE + D_VDIM)
    return (q_up.reshape(D_Q_LORA, -1).astype(BF16),
            kv[..., :D_NOPE].reshape(D_KV_LORA, -1).astype(BF16),
            kv[..., D_NOPE:].reshape(D_KV_LORA, -1).T.astype(BF16))


def _rope_table(positions):
    half = D_ROPE // 2
    inv_freq = ROPE_THETA ** (-jnp.arange(half, dtype=F32) / half)
    ang = positions.astype(F32)[..., None] * inv_freq
    cos, sin = jnp.cos(ang), jnp.sin(ang)
    tab = jnp.concatenate([cos, cos, sin, sin], axis=-1)
    return tab.reshape(-1, 4 * half)


def _layer(layer, x, xb, tab, batch, seq, w_main, w_small, w_gate, w_branch, w_out, dq_gain, dq_up, dkv_gain,
           dkv_up, f_bias, sinks, ln_gain, ln_bias, alpha):
    proj = _matmul(xb, w_main, layer, BF16, transposed=True)
    small = _matmul(xb, w_small, layer, F32, transposed=True, tn=LANES)

    ck = min(512, seq)
    wt = small[:, :IDX_HEADS].reshape(batch, seq, IDX_HEADS).transpose(0, 2, 1)
    a_v = proj[:, P_OFF['a_v']:P_OFF['a_v'] + 2 * LANES]
    vt = a_v.reshape(batch, seq // ck, ck, 2 * LANES).transpose(0, 1, 3, 2)
    o_a = _sparse_attention(proj, vt, wt, batch, seq)

    f_rows = small[:, IDX_HEADS:IDX_HEADS + B_HEADS].reshape(batch, seq, B_HEADS).transpose(0, 2, 1)
    cum = _forget_cumsum(f_rows.reshape(batch * B_HEADS, seq),
                         jnp.tile(f_bias.astype(F32), batch).reshape(batch * B_HEADS, 1))
    cum = cum.reshape(batch, B_HEADS, seq).transpose(0, 2, 1).reshape(batch * seq, B_HEADS)
    cum = jnp.pad(cum, ((0, 0), (0, LANES - B_HEADS)))
    vt_b = proj[:, P_OFF['b_v']:P_OFF['b_v'] + BRANCH_WIDTH].T
    o_b = _flash_attention(proj, P_OFF['b_q'] // BRANCH_WIDTH, proj, P_OFF['b_k'] // BRANCH_WIDTH, vt_b,
                           proj, P_OFF['b_z'] // BRANCH_WIDTH, batch=batch, seq=seq, heads=B_HEADS,
                           dqk=B_DIM, dv=B_DIM, cum=cum)

    o_c = _sliding_window(proj, sinks, batch, seq)

    q_up, k_up, v_up_t = _prepare_latent(dq_up, dkv_up)
    q_d, k_d, vt_d = _latent_prep(proj, tab, dq_gain, dkv_gain, q_up, k_up, v_up_t)
    o_d = _flash_attention(q_d, 0, k_d, 0, vt_d, proj, P_OFF['d_z'] // BRANCH_WIDTH, batch=batch, seq=seq,
                           heads=D_HEADS, dqk=2 * LANES, dv=D_VDIM)

    merged = _gated_merge(xb, w_gate, (o_a, o_b, o_c, o_d), w_branch, layer)
    y = _matmul(merged, w_out, layer, F32)
    return _residual_layernorm(x, y, ln_gain, ln_bias, alpha)


def kernel(x, positions, w_in, w_gate, w_branch, w_out, dq_gain, dq_up, dkv_gain, dkv_up, f_bias, sinks,
           ln_gain, ln_bias):
    batch, seq, d = x.shape
    depth = w_in.shape[0]
    alpha = (2 * depth) ** 0.25
    tab = _rope_table(positions)
    xf = x.reshape(batch * seq, d)
    xb = _cast_bf16(xf)
    w_main, w_small = _prepare_in_proj(w_in)
    w_gate, w_branch, w_out = w_gate.astype(BF16), w_branch.astype(BF16), w_out.astype(BF16)
    for l in range(depth):
        xf, xb = _layer(l, xf, xb, tab, batch, seq, w_main, w_small, w_gate, w_branch, w_out, dq_gain[l],
                        dq_up[l], dkv_gain[l], dkv_up[l], f_bias[l], sinks[l], ln_gain[l], ln_bias[l], alpha)
    return xf.reshape(batch, seq, d)
```

```python
import functools

import numpy as np
import jax
import jax.numpy as jnp
from jax import lax
from jax.experimental import pallas as pl
from jax.experimental.pallas import tpu as pltpu

BRANCH_WIDTH = 1024
A_HEADS, A_KV_HEADS, A_DIM = 8, 2, 128
IDX_HEADS, IDX_DIM, IDX_TOPK_MAX = 16, 64, 256
B_HEADS, B_DIM = 8, 128
C_HEADS, C_KV_HEADS, C_DIM, WINDOW = 16, 2, 64, 128
D_HEADS, D_Q_LORA, D_KV_LORA, D_NOPE, D_ROPE, D_VDIM = 8, 768, 256, 128, 64, 128
ROPE_THETA = 10000.0
RMS_EPS = 1e-6
LN_EPS = 1e-5

IN_SEGMENTS = (
    ('a_q', 1024), ('a_k', 256), ('a_v', 256), ('a_iq', 1024), ('a_ik', 64), ('a_iw', 16), ('a_z', 1024),
    ('b_q', 1024), ('b_k', 1024), ('b_v', 1024), ('b_f', 8), ('b_z', 1024),
    ('c_q', 1024), ('c_k', 128), ('c_v', 128), ('c_z', 1024),
    ('d_cq', 768), ('d_ckv', 256), ('d_kr', 64), ('d_z', 1024),
)

P_LAYOUT = (
    ('d_cq', 768), ('d_ckv', 256), ('a_q', 1024), ('a_iq', 1024), ('a_z', 1024),
    ('b_q', 1024), ('b_k', 1024), ('b_v', 1024), ('b_z', 1024), ('c_q', 1024), ('c_z', 1024),
    ('d_z', 1024), ('a_k', 256), ('a_v', 256), ('a_ik', 128), ('d_kr', 128), ('c_k', 128), ('c_v', 128),
)
P_OFF = {}
_o = 0
for _n, _w in P_LAYOUT:
    assert _o % _w == 0
    P_OFF[_n] = _o
    _o += _w
P_WIDTH = _o

LANES = 128
SUBLANES = 8
NEG = -1e30
FAR = 1e34
LOG2E = 1.4426950408889634
INT_MIN = -2 ** 31
VMEM_LIMIT = 56 * 1024 * 1024

F32 = jnp.float32
BF16 = jnp.bfloat16


def _params(sem, vmem=VMEM_LIMIT):
    return pltpu.CompilerParams(dimension_semantics=sem, vmem_limit_bytes=vmem)


def _dot(a, b):
    return jnp.dot(a, b, preferred_element_type=F32)


def _dot_nt(a, b):
    return lax.dot_general(a, b, (((1,), (1,)), ((), ())), preferred_element_type=F32)


def _alibi_slopes(n_heads):
    return [float(np.float32(2.0 ** (-8.0 * (h + 1) / n_heads))) for h in range(n_heads)]


def _mm_kernel(x_ref, w_ref, o_ref):
    o_ref[...] = _dot(x_ref[...], w_ref[0]).astype(o_ref.dtype)


def _mm_nt_kernel(x_ref, wt_ref, o_ref):
    o_ref[...] = _dot_nt(x_ref[...], wt_ref[0]).astype(o_ref.dtype)


def _matmul(x, w, layer, out_dtype, transposed=False, tm=1024, tn=1024):
    m, k = x.shape
    n = w.shape[1] if transposed else w.shape[2]
    tm, tn = min(tm, m), min(tn, n)
    if transposed:
        body, w_spec = _mm_nt_kernel, pl.BlockSpec((1, tn, k), lambda i, j: (layer, j, 0))
    else:
        body, w_spec = _mm_kernel, pl.BlockSpec((1, k, tn), lambda i, j: (layer, 0, j))
    return pl.pallas_call(
        body,
        grid=(m // tm, n // tn),
        in_specs=[pl.BlockSpec((tm, k), lambda i, j: (i, 0)), w_spec],
        out_specs=pl.BlockSpec((tm, tn), lambda i, j: (i, j)),
        out_shape=jax.ShapeDtypeStruct((m, n), out_dtype),
        compiler_params=_params(("parallel", "parallel")),
        name="dense_matmul",
    )(x, w)


def _cast_kernel(x_ref, o_ref):
    o_ref[...] = x_ref[...].astype(o_ref.dtype)


def _cast_bf16(x, tm=512):
    t, d = x.shape
    tm = min(tm, t)
    row = pl.BlockSpec((tm, d), lambda i: (i, 0))
    return pl.pallas_call(
        _cast_kernel, grid=(t // tm,), in_specs=[row], out_specs=row,
        out_shape=jax.ShapeDtypeStruct((t, d), BF16),
        compiler_params=_params(("parallel",)),
        name="cast_bf16",
    )(x)


def _merge_kernel(x_ref, wg_ref, b0_ref, b1_ref, b2_ref, b3_ref, wb_ref, o_ref, acc_ref, *, sub):
    n = pl.program_id(2)

    @pl.when((pl.program_id(0) == 0) & (pl.program_id(1) == 0) & (n == 0))
    def _():
        acc_ref[...] = jnp.zeros(acc_ref.shape, F32)

    branch = jnp.where(n == 0, b0_ref[...], jnp.where(n == 1, b1_ref[...],
                                                      jnp.where(n == 2, b2_ref[...], b3_ref[...])))
    x = x_ref[...]
    for c in range(o_ref.shape[1] // sub):
        cols = slice(c * sub, (c + 1) * sub)
        gate = jax.nn.sigmoid(_dot(x, wg_ref[0, 0, :, cols]))
        contrib = gate * _dot(branch, wb_ref[0, 0, :, cols])
        acc = jnp.where(n == 0, 0.0, acc_ref[:, cols]) + contrib
        acc_ref[:, cols] = acc
        o_ref[:, cols] = acc.astype(o_ref.dtype)


def _gated_merge(xb, wg, branches, wb, layer, tm=512, tn=1024, sub=256):
    t, d = xb.shape
    tm, tn = min(tm, t), min(tn, d)
    sub = min(sub, tn)
    bw = branches[0].shape[1]
    bspec = pl.BlockSpec((tm, bw), lambda i, j, n: (i, 0))
    return pl.pallas_call(
        functools.partial(_merge_kernel, sub=sub),
        grid=(t // tm, d // tn, 4),
        in_specs=[pl.BlockSpec((tm, d), lambda i, j, n: (i, 0)),
                  pl.BlockSpec((1, 1, d, tn), lambda i, j, n: (layer, n, 0, j)),
                  bspec, bspec, bspec, bspec,
                  pl.BlockSpec((1, 1, bw, tn), lambda i, j, n: (layer, n, 0, j))],
        out_specs=pl.BlockSpec((tm, tn), lambda i, j, n: (i, j)),
        out_shape=jax.ShapeDtypeStruct((t, d), BF16),
        scratch_shapes=[pltpu.VMEM((tm, tn), F32)],
        compiler_params=_params(("arbitrary", "arbitrary", "arbitrary")),
        name="gated_merge",
    )(xb, wg, *branches, wb)


def _ln_kernel(x_ref, y_ref, g_ref, b_ref, o_ref, ob_ref, *, alpha):
    r = alpha * x_ref[...] + y_ref[...]
    mu = jnp.mean(r, axis=-1, keepdims=True)
    c = r - mu
    var = jnp.mean(c * c, axis=-1, keepdims=True)
    out = c * lax.rsqrt(var + LN_EPS) * g_ref[...] + b_ref[...]
    o_ref[...] = out
    ob_ref[...] = out.astype(BF16)


def _residual_layernorm(x, y, gain, bias, alpha, tm=256):
    t, d = x.shape
    tm = min(tm, t)
    row = pl.BlockSpec((tm, d), lambda i: (i, 0))
    vec = pl.BlockSpec((1, d), lambda i: (0, 0))
    return pl.pallas_call(
        functools.partial(_ln_kernel, alpha=alpha),
        grid=(t // tm,),
        in_specs=[row, row, vec, vec],
        out_specs=[row, row],
        out_shape=[jax.ShapeDtypeStruct((t, d), F32), jax.ShapeDtypeStruct((t, d), BF16)],
        compiler_params=_params(("parallel",)),
        name="residual_layernorm",
    )(x, y, gain.reshape(1, d), bias.reshape(1, d))


def _forget_cumsum_kernel(f_ref, bias_ref, c_ref):
    rows, s = f_ref.shape
    lane = lax.broadcasted_iota(jnp.int32, (rows, LANES), 1)
    carry = jnp.zeros((rows, 1), F32)
    for c in range(s // LANES):
        x = jax.nn.log_sigmoid(f_ref[:, c * LANES:(c + 1) * LANES] + bias_ref[...])
        shift = 1
        while shift < LANES:
            x = x + jnp.where(lane >= shift, pltpu.roll(x, shift, 1), 0.0)
            shift *= 2
        x = x + carry
        c_ref[:, c * LANES:(c + 1) * LANES] = x * LOG2E
        carry = x[:, LANES - 1:LANES]


def _forget_cumsum(f_rows, bias_rows):
    rows, s = f_rows.shape
    full = pl.BlockSpec((rows, s), lambda: (0, 0))
    return pl.pallas_call(
        _forget_cumsum_kernel,
        in_specs=[full, pl.BlockSpec((rows, 1), lambda: (0, 0))],
        out_specs=full,
        out_shape=jax.ShapeDtypeStruct((rows, s), F32),
        name="forget_cumsum",
    )(f_rows, bias_rows)


def _tree(parts, op, ways=8):
    parts = list(parts)
    if len(parts) > ways:
        accs = parts[:ways]
        for a in range(ways, len(parts)):
            accs[a % ways] = op(accs[a % ways], parts[a])
        parts = accs
    while len(parts) > 1:
        nxt = [op(parts[a], parts[a + 1]) for a in range(0, len(parts) - 1, 2)]
        if len(parts) % 2:
            nxt.append(parts[-1])
        parts = nxt
    return parts[0]


def _col_reduce(x, op, reduce_fn):
    rows = x.shape[0]
    part = _tree([x[r:r + SUBLANES, :] for r in range(0, rows, SUBLANES)], op)
    return reduce_fn(part, axis=0, keepdims=True)


def _flash_kernel(i_tab, j_tab, *refs, fox, heads, dqk, dv, tile, slab):
    if fox:
        q_ref, k_ref, vt_ref, ck_ref, z_ref, o_ref, m_sc, l_sc, acc_sc = refs
    else:
        q_ref, k_ref, vt_ref, z_ref, o_ref, m_sc, l_sc, acc_sc = refs
    pair = pl.program_id(1)
    i = i_tab[pair]
    j = j_tab[pair]

    @pl.when(j == 0)
    def _():
        m_sc[...] = jnp.full(m_sc.shape, NEG, F32)
        l_sc[...] = jnp.zeros(l_sc.shape, F32)
        acc_sc[...] = jnp.zeros(acc_sc.shape, F32)

    def step(masked):
        if masked:
            causal = (lax.broadcasted_iota(jnp.int32, (tile, tile), 0)
                      <= lax.broadcasted_iota(jnp.int32, (tile, tile), 1))
        def logits(h):
            return _dot_nt(k_ref[:, h * dqk:(h + 1) * dqk], q_ref[:, h * dqk:(h + 1) * dqk])

        s_next = logits(0)
        for h in range(heads):
            s = s_next
            if h + 1 < heads:
                s_next = logits(h + 1)
            if fox:
                s = s - ck_ref[:, h:h + 1]
            if masked:
                s = jnp.where(causal, s, NEG)
            m_prev = m_sc[h:h + 1, :]
            m_new = jnp.maximum(m_prev, _col_reduce(s, jnp.maximum, jnp.max))
            alpha = jnp.exp2(m_prev - m_new)
            psum, pv = None, None
            for k0 in range(0, tile, slab):
                p = jnp.exp2(s[k0:k0 + slab, :] - m_new)
                part = _tree([p[r:r + SUBLANES, :] for r in range(0, slab, SUBLANES)], jnp.add)
                psum = part if psum is None else psum + part
                contrib = _dot(vt_ref[h * dv:(h + 1) * dv, k0:k0 + slab], p.astype(BF16))
                pv = contrib if pv is None else pv + contrib
            l_sc[h:h + 1, :] = alpha * l_sc[h:h + 1, :] + jnp.sum(psum, axis=0, keepdims=True)
            acc_sc[h * dv:(h + 1) * dv, :] = alpha * acc_sc[h * dv:(h + 1) * dv, :] + pv
            m_sc[h:h + 1, :] = m_new

    @pl.when(j < i)
    def _():
        step(False)

    @pl.when(j == i)
    def _():
        step(True)
        for h in range(heads):
            out_t = acc_sc[h * dv:(h + 1) * dv, :] / l_sc[h:h + 1, :]
            z = z_ref[:, h * dv:(h + 1) * dv].astype(F32)
            o_ref[:, h * dv:(h + 1) * dv] = (out_t.T * jax.nn.silu(z)).astype(o_ref.dtype)


def _flash_attention(q_arr, q_blk, k_arr, k_blk, vt_arr, z_arr, z_blk, *, batch, seq, heads,
                     dqk, dv, cum=None, tile=512):
    tile = min(tile, seq)
    nq = seq // tile
    fox = cum is not None
    pairs = [(i, j) for i in range(nq) for j in range(i + 1)]
    i_tab = jnp.asarray([p[0] for p in pairs], jnp.int32)
    j_tab = jnp.asarray([p[1] for p in pairs], jnp.int32)
    in_specs = [
        pl.BlockSpec((tile, heads * dqk), lambda b, p, it, jt: (b * nq + it[p], q_blk)),
        pl.BlockSpec((tile, heads * dqk), lambda b, p, it, jt: (b * nq + jt[p], k_blk)),
        pl.BlockSpec((heads * dv, tile), lambda b, p, it, jt: (0, b * nq + jt[p])),
    ]
    args = [q_arr, k_arr, vt_arr]
    if fox:
        in_specs.append(pl.BlockSpec((tile, LANES), lambda b, p, it, jt: (b * nq + jt[p], 0)))
        args.append(cum)
    in_specs.append(pl.BlockSpec((tile, heads * dv), lambda b, p, it, jt: (b * nq + it[p], z_blk)))
    args.append(z_arr)
    return pl.pallas_call(
        functools.partial(_flash_kernel, fox=fox, heads=heads, dqk=dqk, dv=dv, tile=tile,
                          slab=min(2 * LANES, tile)),
        grid_spec=pltpu.PrefetchScalarGridSpec(
            num_scalar_prefetch=2,
            grid=(batch, len(pairs)),
            in_specs=in_specs,
            out_specs=pl.BlockSpec((tile, heads * dv), lambda b, p, it, jt: (b * nq + it[p], 0)),
            scratch_shapes=[pltpu.VMEM((heads, tile), F32), pltpu.VMEM((heads, tile), F32),
                            pltpu.VMEM((heads * dv, tile), F32)]),
        out_shape=jax.ShapeDtypeStruct((batch * seq, heads * dv), BF16),
        compiler_params=_params(("parallel", "arbitrary")),
        name="flash_fox" if fox else "flash_latent",
    )(i_tab, j_tab, *args)


def _swa_kernel(q_ref, kvc_ref, kvp_ref, z_ref, sink_ref, o_ref, *, slopes):
    i = pl.program_id(1)
    w = WINDOW
    pairs_per_group = C_HEADS // C_KV_HEADS // 2
    kv = jnp.concatenate([kvp_ref[...], kvc_ref[...]], axis=0).astype(F32)
    lane = lax.broadcasted_iota(jnp.int32, (2 * w, LANES), 1)
    low = lane < C_DIM

    def halves(t):
        g0_lo = jnp.where(low, t, 0.0)
        g1_hi = jnp.where(low, 0.0, t)
        return ((g0_lo, pltpu.roll(g0_lo, C_DIM, 1)), (pltpu.roll(g1_hi, C_DIM, 1), g1_hi))

    k_half = halves(kv[:, :LANES])
    v_half = halves(kv[:, LANES:])
    key = lax.broadcasted_iota(jnp.int32, (2 * w, w), 0)
    qry = lax.broadcasted_iota(jnp.int32, (2 * w, w), 1)
    dist = qry + w - key
    first_key = jnp.where(i > 0, 0, w)
    valid = (dist >= 0) & (dist < w) & (key >= first_key)
    dist_m = jnp.where(valid, dist.astype(F32), -NEG)

    q_rows = [jnp.concatenate([q_ref[:, (g * pairs_per_group + jj) * LANES:(g * pairs_per_group + jj + 1) * LANES]
                               for jj in range(pairs_per_group)], axis=0) for g in range(C_KV_HEADS)]
    logits = [[_dot_nt(k_half[g][half].astype(BF16), q_rows[g]) for half in range(2)] for g in range(C_KV_HEADS)]
    for g in range(C_KV_HEADS):
        out_t = None
        for half in range(2):
            probs, inv = [], []
            for jj in range(pairs_per_group):
                h = 2 * (g * pairs_per_group + jj) + half
                s = logits[g][half][:, jj * w:(jj + 1) * w] - (slopes[h] * LOG2E) * dist_m
                sink = jnp.full((1, w), sink_ref[h], F32) * LOG2E
                m = jnp.maximum(_col_reduce(s, jnp.maximum, jnp.max), sink)
                p = jnp.exp2(s - m)
                inv.append(1.0 / (_col_reduce(p, jnp.add, jnp.sum) + jnp.exp2(sink - m)))
                probs.append(p.astype(BF16))
            vt = v_half[g][half].T.astype(BF16)
            contrib = _dot(vt, jnp.concatenate(probs, axis=1)) * jnp.concatenate(inv, axis=1)
            out_t = contrib if out_t is None else out_t + contrib
        for jj in range(pairs_per_group):
            cols = slice((g * pairs_per_group + jj) * LANES, (g * pairs_per_group + jj + 1) * LANES)
            z = z_ref[:, cols].astype(F32)
            o_ref[:, cols] = (out_t[:, jj * w:(jj + 1) * w].T * jax.nn.silu(z)).astype(o_ref.dtype)


def _sliding_window(p_arr, sinks, batch, seq):
    w = WINDOW
    nb = seq // w
    kv_blk = P_OFF['c_k'] // (2 * LANES)
    return pl.pallas_call(
        functools.partial(_swa_kernel, slopes=_alibi_slopes(C_HEADS)),
        grid=(batch, nb),
        in_specs=[pl.BlockSpec((w, BRANCH_WIDTH), lambda b, i: (b * nb + i, P_OFF['c_q'] // BRANCH_WIDTH)),
                  pl.BlockSpec((w, 2 * LANES), lambda b, i: (b * nb + i, kv_blk)),
                  pl.BlockSpec((w, 2 * LANES), lambda b, i: (b * nb + jnp.maximum(i - 1, 0), kv_blk)),
                  pl.BlockSpec((w, BRANCH_WIDTH), lambda b, i: (b * nb + i, P_OFF['c_z'] // BRANCH_WIDTH)),
                  pl.BlockSpec(memory_space=pltpu.SMEM)],
        out_specs=pl.BlockSpec((w, BRANCH_WIDTH), lambda b, i: (b * nb + i, 0)),
        out_shape=jax.ShapeDtypeStruct((batch * seq, BRANCH_WIDTH), BF16),
        compiler_params=_params(("parallel", "parallel")),
        name="sliding_window",
    )(p_arr, p_arr, p_arr, p_arr, sinks.astype(F32))


def _latent_prep_kernel(cq_ref, ckv_ref, kr_ref, tab_ref, qg_ref, kvg_ref, qup_ref, kup_ref, vupt_ref,
                        q_ref, k_ref, vt_ref):
    def rms(x, g):
        return x * lax.rsqrt(jnp.mean(x * x, axis=-1, keepdims=True) + RMS_EPS) * g

    tab = tab_ref[...]
    lane = lax.broadcasted_iota(jnp.int32, tab.shape, 1)

    def rotate(t):
        r = t * tab
        return r + pltpu.roll(r, D_ROPE, 1)

    q = _dot(rms(cq_ref[...].astype(F32), qg_ref[...]).astype(BF16), qup_ref[...])
    ckv = rms(ckv_ref[...].astype(F32), kvg_ref[...]).astype(BF16)
    k_nope = _dot(ckv, kup_ref[...])
    vt_ref[...] = _dot_nt(vupt_ref[...], ckv).astype(vt_ref.dtype)
    k_rot = jnp.where(lane < D_ROPE, rotate(kr_ref[...].astype(F32)), 0.0).astype(k_ref.dtype)
    hw = 2 * LANES
    for h in range(D_HEADS):
        q_ref[:, h * hw:h * hw + LANES] = q[:, h * hw:h * hw + LANES].astype(q_ref.dtype)
        q_ref[:, h * hw + LANES:(h + 1) * hw] = rotate(q[:, h * hw + LANES:(h + 1) * hw]).astype(q_ref.dtype)
        k_ref[:, h * hw:h * hw + LANES] = k_nope[:, h * LANES:(h + 1) * LANES].astype(k_ref.dtype)
        k_ref[:, h * hw + LANES:(h + 1) * hw] = k_rot


def _latent_prep(p_arr, tab, q_gain, kv_gain, q_up, k_up, v_up_t, tm=512):
    t = p_arr.shape[0]
    tm = min(tm, t)
    hw = 2 * LANES

    def const(shape):
        return pl.BlockSpec(shape, lambda i: (0, 0))

    return pl.pallas_call(
        _latent_prep_kernel,
        grid=(t // tm,),
        in_specs=[pl.BlockSpec((tm, D_Q_LORA), lambda i: (i, P_OFF['d_cq'] // D_Q_LORA)),
                  pl.BlockSpec((tm, D_KV_LORA), lambda i: (i, P_OFF['d_ckv'] // D_KV_LORA)),
                  pl.BlockSpec((tm, LANES), lambda i: (i, P_OFF['d_kr'] // LANES)),
                  pl.BlockSpec((tm, LANES), lambda i: (i, 0)),
                  const((1, D_Q_LORA)), const((1, D_KV_LORA)),
                  const((D_Q_LORA, D_HEADS * hw)), const((D_KV_LORA, D_HEADS * LANES)),
                  const((D_HEADS * D_VDIM, D_KV_LORA))],
        out_specs=[pl.BlockSpec((tm, D_HEADS * hw), lambda i: (i, 0)),
                   pl.BlockSpec((tm, D_HEADS * hw), lambda i: (i, 0)),
                   pl.BlockSpec((D_HEADS * D_VDIM, tm), lambda i: (0, i))],
        out_shape=[jax.ShapeDtypeStruct((t, D_HEADS * hw), BF16),
                   jax.ShapeDtypeStruct((t, D_HEADS * hw), BF16),
                   jax.ShapeDtypeStruct((D_HEADS * D_VDIM, t), BF16)],
        compiler_params=_params(("parallel",)),
        name="latent_prep",
    )(p_arr, p_arr, p_arr, tab, q_gain.reshape(1, -1), kv_gain.reshape(1, -1), q_up, k_up, v_up_t)


def _sparse_kernel(q_ref, iq_ref, z_ref, ik_ref, k_ref, vt_ref, wt_ref, o_ref,
                   keys_ref, jcut_ref, m_ref, l_ref, acc_ref, *, topk, ck, slopes, idx_bits):
    i = pl.program_id(1)
    tq = LANES
    nch = ((i + 1) * tq + ck - 1) // ck
    slab = min(2 * LANES, ck)
    q_pos = i * tq + lax.broadcasted_iota(jnp.int32, (ck, tq), 1)
    s_iota = lax.broadcasted_iota(jnp.int32, (ck, tq), 0)
    lane = lax.broadcasted_iota(jnp.int32, (ck, LANES), 1)
    low = lane < IDX_DIM
    w_all = wt_ref[0] * (IDX_HEADS ** -0.5 * IDX_DIM ** -0.5)

    def score_chunk(c, carry):
        start = pl.multiple_of(c * ck, ck)
        ikc = ik_ref[pl.ds(start, ck), :]
        ik_lo = jnp.where(low, ikc, jnp.zeros_like(ikc))
        ik_hi = jnp.where(low, jnp.zeros_like(ikc), ikc)
        acc = jnp.zeros((ck, tq), F32)
        for pair in range(IDX_HEADS // 2):
            iqp = iq_ref[:, pair * LANES:(pair + 1) * LANES]
            acc = acc + w_all[2 * pair:2 * pair + 1, :] * jnp.maximum(_dot_nt(ik_lo, iqp), 0.0)
            acc = acc + w_all[2 * pair + 1:2 * pair + 2, :] * jnp.maximum(_dot_nt(ik_hi, iqp), 0.0)
        acc = jnp.where(acc == 0.0, 0.0, acc)
        bits = lax.bitcast_convert_type(acc, jnp.int32)
        key = bits ^ ((bits >> 31) & 0x7FFFFFFF)
        key = jnp.where(start + s_iota <= q_pos, key, INT_MIN)
        keys_ref[pl.ds(start, ck), :] = key
        return carry

    lax.fori_loop(0, nch, score_chunk, 0)

    def count(pred_fn):
        def body(c, cnt):
            start = pl.multiple_of(c * ck, ck)
            hit = jnp.where(pred_fn(keys_ref[pl.ds(start, ck), :], start + s_iota), 1.0, 0.0)
            return cnt + _tree([hit[r:r + SUBLANES, :] for r in range(0, ck, SUBLANES)], jnp.add)
        part = lax.fori_loop(0, nch, body, jnp.zeros((SUBLANES, tq), F32))
        return jnp.sum(part, axis=0, keepdims=True)

    kf = float(topk)

    def thr_step(it, state):
        thr, n_ge = state
        cand = thr ^ (jnp.int32(1) << (31 - it))
        cnt = count(lambda kc, pos: kc >= cand)
        ok = cnt >= kf
        return jnp.where(ok, cand, thr), jnp.where(ok, cnt, n_ge)

    total = jnp.full((1, tq), 1.0, F32) * (nch * ck).astype(F32)
    thr, n_ge = lax.fori_loop(0, 32, thr_step, (jnp.full((1, tq), INT_MIN, jnp.int32), total))
    tied = (n_ge > kf) & (thr != INT_MIN)
    jcut_ref[...] = jnp.full(jcut_ref.shape, 2 ** 30, jnp.int32)

    @pl.when(jnp.max(tied.astype(F32)) > 0.0)
    def _():
        need = kf - count(lambda kc, pos: kc > thr)

        def cut_step(it, x):
            cand = x + (jnp.int32(1) << (idx_bits - 1 - it))
            cnt = count(lambda kc, pos: (kc == thr) & (pos < cand))
            return jnp.where(cnt < need, cand, x)
        x = lax.fori_loop(0, idx_bits, cut_step, jnp.zeros((1, tq), jnp.int32))
        jcut_ref[...] = jnp.broadcast_to(jnp.where(tied, x, 2 ** 30), jcut_ref.shape)

    jcut = jcut_ref[0:1, :]

    m_ref[...] = jnp.full(m_ref.shape, NEG, F32)
    l_ref[...] = jnp.zeros(l_ref.shape, F32)
    acc_ref[...] = jnp.zeros(acc_ref.shape, F32)
    rep = A_HEADS // A_KV_HEADS
    q_groups = [jnp.concatenate([q_ref[:, (g * rep + r) * A_DIM:(g * rep + r + 1) * A_DIM] for r in range(rep)],
                                axis=0) for g in range(A_KV_HEADS)]

    def attend_chunk(c, carry):
        start = pl.multiple_of(c * ck, ck)
        kc = keys_ref[pl.ds(start, ck), :]
        pos = start + s_iota
        sel = ((kc > thr) | ((kc == thr) & (pos <= jcut))) & (pos <= q_pos)
        dist = jnp.where(sel, (q_pos - pos).astype(F32), FAR)
        logits_g = [_dot_nt(k_ref[pl.ds(start, ck), g * A_DIM:(g + 1) * A_DIM], q_groups[g])
                    for g in range(A_KV_HEADS)]
        for g in range(A_KV_HEADS):
            vtg = vt_ref[0, c, g * A_DIM:(g + 1) * A_DIM, :]
            for r in range(rep):
                h = g * rep + r
                s = logits_g[g][:, r * tq:(r + 1) * tq] - (slopes[h] * LOG2E) * dist
                m_prev = m_ref[h:h + 1, :]
                m_new = jnp.maximum(m_prev, _col_reduce(s, jnp.maximum, jnp.max))
                alpha = jnp.exp2(m_prev - m_new)
                psum, pv = None, None
                for k0 in range(0, ck, slab):
                    p = jnp.exp2(s[k0:k0 + slab, :] - m_new)
                    part = _tree([p[r0:r0 + SUBLANES, :] for r0 in range(0, slab, SUBLANES)], jnp.add)
                    psum = part if psum is None else psum + part
                    contrib = _dot(vtg[:, k0:k0 + slab], p.astype(BF16))
                    pv = contrib if pv is None else pv + contrib
                l_ref[h:h + 1, :] = alpha * l_ref[h:h + 1, :] + jnp.sum(psum, axis=0, keepdims=True)
                acc_ref[h * A_DIM:(h + 1) * A_DIM, :] = alpha * acc_ref[h * A_DIM:(h + 1) * A_DIM, :] + pv
                m_ref[h:h + 1, :] = m_new
        return carry

    lax.fori_loop(0, nch, attend_chunk, 0)

    for h in range(A_HEADS):
        out_t = acc_ref[h * A_DIM:(h + 1) * A_DIM, :] / l_ref[h:h + 1, :]
        z = z_ref[:, h * A_DIM:(h + 1) * A_DIM].astype(F32)
        o_ref[:, h * A_DIM:(h + 1) * A_DIM] = (out_t.T * jax.nn.silu(z)).astype(o_ref.dtype)


def _sparse_attention(p_arr, vt, wt, batch, seq):
    tq = LANES
    nq = seq // tq
    ck = min(512, seq)
    topk = min(IDX_TOPK_MAX, seq // 4)
    idx_bits = int(seq).bit_length()
    return pl.pallas_call(
        functools.partial(_sparse_kernel, topk=topk, ck=ck, slopes=_alibi_slopes(A_HEADS), idx_bits=idx_bits),
        grid=(batch, nq),
        in_specs=[pl.BlockSpec((tq, BRANCH_WIDTH), lambda b, i: (b * nq + i, P_OFF['a_q'] // BRANCH_WIDTH)),
                  pl.BlockSpec((tq, BRANCH_WIDTH), lambda b, i: (b * nq + i, P_OFF['a_iq'] // BRANCH_WIDTH)),
                  pl.BlockSpec((tq, BRANCH_WIDTH), lambda b, i: (b * nq + i, P_OFF['a_z'] // BRANCH_WIDTH)),
                  pl.BlockSpec((seq, LANES), lambda b, i: (b, P_OFF['a_ik'] // LANES)),
                  pl.BlockSpec((seq, 2 * LANES), lambda b, i: (b, P_OFF['a_k'] // (2 * LANES))),
                  pl.BlockSpec((1, seq // ck, 2 * LANES, ck), lambda b, i: (b, 0, 0, 0)),
                  pl.BlockSpec((1, IDX_HEADS, tq), lambda b, i: (b, 0, i))],
        out_specs=pl.BlockSpec((tq, BRANCH_WIDTH), lambda b, i: (b * nq + i, 0)),
        out_shape=jax.ShapeDtypeStruct((batch * seq, BRANCH_WIDTH), BF16),
        scratch_shapes=[pltpu.VMEM((seq, tq), jnp.int32), pltpu.VMEM((8, tq), jnp.int32),
                        pltpu.VMEM((A_HEADS, tq), F32), pltpu.VMEM((A_HEADS, tq), F32),
                        pltpu.VMEM((A_HEADS * A_DIM, tq), F32)],
        compiler_params=_params(("parallel", "arbitrary")),
        name="sparse_attention",
    )(p_arr, p_arr, p_arr, p_arr, p_arr, vt, wt)


def _swap_halves(w):
    half = w.shape[-1] // 2
    return jnp.concatenate([-w[..., half:], w[..., :half]], axis=-1)


def _prepare_in_proj(w_in):
    sizes = [n for _, n in IN_SEGMENTS]
    offs = np.concatenate([[0], np.cumsum(sizes)])
    span = {name: (int(offs[k]), int(offs[k + 1])) for k, (name, _) in enumerate(IN_SEGMENTS)}
    wt32 = jnp.swapaxes(w_in, 1, 2)
    wt16 = wt32.astype(BF16)
    seg16 = lambda name: wt16[:, span[name][0]:span[name][1], :]
    seg32 = lambda name: wt32[:, span[name][0]:span[name][1], :]
    rows = []
    for name, _ in P_LAYOUT:
        if name == 'a_ik':
            rows += [seg16('a_ik'), seg16('a_ik')]
        elif name == 'd_kr':
            kr = seg16('d_kr')
            rows += [kr, -kr[:, D_ROPE // 2:, :], kr[:, :D_ROPE // 2, :]]
        elif name == 'a_q':
            rows.append((seg32('a_q') * (A_DIM ** -0.5 * LOG2E)).astype(BF16))
        elif name == 'b_q':
            rows.append((seg32('b_q') * (B_DIM ** -0.5 * LOG2E)).astype(BF16))
        elif name == 'c_q':
            rows.append((seg32('c_q') * (C_DIM ** -0.5 * LOG2E)).astype(BF16))
        else:
            rows.append(seg16(name))
    w_main_t = jnp.concatenate(rows, axis=1)
    pad = jnp.zeros((w_in.shape[0], LANES - IDX_HEADS - B_HEADS, w_in.shape[1]), BF16)
    w_small_t = jnp.concatenate([seg16('a_iw'), seg16('b_f'), pad], axis=1)
    return w_main_t, w_small_t


def _prepare_latent(dq_up, dkv_up):
    q = dq_up.reshape(D_Q_LORA, D_HEADS, D_NOPE + D_ROPE) * ((D_NOPE + D_ROPE) ** -0.5 * LOG2E)
    rope = q[..., D_NOPE:]
    q_up = jnp.concatenate([q[..., :D_NOPE], rope, _swap_halves(rope)], axis=-1)
    kv = dkv_up.reshape(D_KV_LORA, D_HEADS, D_NOPE + D_VDIM)
    return (q_up.reshape(D_Q_LORA, -1).astype(BF16),
            kv[..., :D_NOPE].reshape(D_KV_LORA, -1).astype(BF16),
            kv[..., D_NOPE:].reshape(D_KV_LORA, -1).T.astype(BF16))


def _rope_table(positions):
    half = D_ROPE // 2
    inv_freq = ROPE_THETA ** (-jnp.arange(half, dtype=F32) / half)
    ang = positions.astype(F32)[..., None] * inv_freq
    cos, sin = jnp.cos(ang), jnp.sin(ang)
    tab = jnp.concatenate([cos, cos, sin, sin], axis=-1)
    return tab.reshape(-1, 4 * half)


def _layer(layer, x, xb, tab, batch, seq, w_main, w_small, w_gate, w_branch, w_out, dq_gain, dq_up, dkv_gain,
           dkv_up, f_bias, sinks, ln_gain, ln_bias, alpha):
    proj = _matmul(xb, w_main, layer, BF16, transposed=True)
    small = _matmul(xb, w_small, layer, F32, transposed=True, tn=LANES)

    ck = min(512, seq)
    wt = small[:, :IDX_HEADS].reshape(batch, seq, IDX_HEADS).transpose(0, 2, 1)
    a_v = proj[:, P_OFF['a_v']:P_OFF['a_v'] + 2 * LANES]
    vt = a_v.reshape(batch, seq // ck, ck, 2 * LANES).transpose(0, 1, 3, 2)
    o_a = _sparse_attention(proj, vt, wt, batch, seq)

    f_rows = small[:, IDX_HEADS:IDX_HEADS + B_HEADS].reshape(batch, seq, B_HEADS).transpose(0, 2, 1)
    cum = _forget_cumsum(f_rows.reshape(batch * B_HEADS, seq),
                         jnp.tile(f_bias.astype(F32), batch).reshape(batch * B_HEADS, 1))
    cum = cum.reshape(batch, B_HEADS, seq).transpose(0, 2, 1).reshape(batch * seq, B_HEADS)
    cum = jnp.pad(cum, ((0, 0), (0, LANES - B_HEADS)))
    vt_b = proj[:, P_OFF['b_v']:P_OFF['b_v'] + BRANCH_WIDTH].T
    o_b = _flash_attention(proj, P_OFF['b_q'] // BRANCH_WIDTH, proj, P_OFF['b_k'] // BRANCH_WIDTH, vt_b,
                           proj, P_OFF['b_z'] // BRANCH_WIDTH, batch=batch, seq=seq, heads=B_HEADS,
                           dqk=B_DIM, dv=B_DIM, cum=cum)

    o_c = _sliding_window(proj, sinks, batch, seq)

    q_up, k_up, v_up_t = _prepare_latent(dq_up, dkv_up)
    q_d, k_d, vt_d = _latent_prep(proj, tab, dq_gain, dkv_gain, q_up, k_up, v_up_t)
    o_d = _flash_attention(q_d, 0, k_d, 0, vt_d, proj, P_OFF['d_z'] // BRANCH_WIDTH, batch=batch, seq=seq,
                           heads=D_HEADS, dqk=2 * LANES, dv=D_VDIM)

    merged = _gated_merge(xb, w_gate, (o_a, o_b, o_c, o_d), w_branch, layer)
    y = _matmul(merged, w_out, layer, F32)
    return _residual_layernorm(x, y, ln_gain, ln_bias, alpha)


def kernel(x, positions, w_in, w_gate, w_branch, w_out, dq_gain, dq_up, dkv_gain, dkv_up, f_bias, sinks,
           ln_gain, ln_bias):
    batch, seq, d = x.shape
    depth = w_in.shape[0]
    alpha = (2 * depth) ** 0.25
    tab = _rope_table(positions)
    xf = x.reshape(batch * seq, d)
    xb = _cast_bf16(xf)
    w_main, w_small = _prepare_in_proj(w_in)
    w_gate, w_branch, w_out = w_gate.astype(BF16), w_branch.astype(BF16), w_out.astype(BF16)
    for l in range(depth):
        xf, xb = _layer(l, xf, xb, tab, batch, seq, w_main, w_small, w_gate, w_branch, w_out, dq_gain[l],
                        dq_up[l], dkv_gain[l], dkv_up[l], f_bias[l], sinks[l], ln_gain[l], ln_bias[l], alpha)
    return xf.reshape(batch, seq, d)
```

```python
import functools

import numpy as np
import jax
import jax.numpy as jnp
from jax import lax
from jax.experimental import pallas as pl
from jax.experimental.pallas import tpu as pltpu

BRANCH_WIDTH = 1024
A_HEADS, A_KV_HEADS, A_DIM = 8, 2, 128
IDX_HEADS, IDX_DIM, IDX_TOPK_MAX = 16, 64, 256
B_HEADS, B_DIM = 8, 128
C_HEADS, C_KV_HEADS, C_DIM, WINDOW = 16, 2, 64, 128
D_HEADS, D_Q_LORA, D_KV_LORA, D_NOPE, D_ROPE, D_VDIM = 8, 768, 256, 128, 64, 128
ROPE_THETA = 10000.0
RMS_EPS = 1e-6
LN_EPS = 1e-5

IN_SEGMENTS = (
    ('a_q', 1024), ('a_k', 256), ('a_v', 256), ('a_iq', 1024), ('a_ik', 64), ('a_iw', 16), ('a_z', 1024),
    ('b_q', 1024), ('b_k', 1024), ('b_v', 1024), ('b_f', 8), ('b_z', 1024),
    ('c_q', 1024), ('c_k', 128), ('c_v', 128), ('c_z', 1024),
    ('d_cq', 768), ('d_ckv', 256), ('d_kr', 64), ('d_z', 1024),
)

P_LAYOUT = (
    ('d_cq', 768), ('d_ckv', 256), ('a_q', 1024), ('a_iq', 1024), ('a_z', 1024),
    ('b_q', 1024), ('b_k', 1024), ('b_v', 1024), ('b_z', 1024), ('c_q', 1024), ('c_z', 1024),
    ('d_z', 1024), ('a_k', 256), ('a_v', 256), ('a_ik', 128), ('d_kr', 128), ('c_k', 128), ('c_v', 128),
)
P_OFF = {}
_o = 0
for _n, _w in P_LAYOUT:
    assert _o % _w == 0
    P_OFF[_n] = _o
    _o += _w
P_WIDTH = _o

LANES = 128
SUBLANES = 8
NEG = -1e30
FAR = 1e34
LOG2E = 1.4426950408889634
INT_MIN = -2 ** 31
VMEM_LIMIT = 56 * 1024 * 1024

F32 = jnp.float32
BF16 = jnp.bfloat16


def _params(sem, vmem=VMEM_LIMIT):
    return pltpu.CompilerParams(dimension_semantics=sem, vmem_limit_bytes=vmem)


def _dot(a, b):
    return jnp.dot(a, b, preferred_element_type=F32)


def _dot_nt(a, b):
    return lax.dot_general(a, b, (((1,), (1,)), ((), ())), preferred_element_type=F32)


def _alibi_slopes(n_heads):
    return [float(np.float32(2.0 ** (-8.0 * (h + 1) / n_heads))) for h in range(n_heads)]


def _mm_kernel(x_ref, w_ref, o_ref):
    o_ref[...] = _dot(x_ref[...], w_ref[0]).astype(o_ref.dtype)


def _mm_nt_kernel(x_ref, wt_ref, o_ref):
    o_ref[...] = _dot_nt(x_ref[...], wt_ref[0]).astype(o_ref.dtype)


def _matmul(x, w, layer, out_dtype, transposed=False, tm=1024, tn=1024):
    m, k = x.shape
    n = w.shape[1] if transposed else w.shape[2]
    tm, tn = min(tm, m), min(tn, n)
    if transposed:
        body, w_spec = _mm_nt_kernel, pl.BlockSpec((1, tn, k), lambda i, j: (layer, j, 0))
    else:
        body, w_spec = _mm_kernel, pl.BlockSpec((1, k, tn), lambda i, j: (layer, 0, j))
    return pl.pallas_call(
        body,
        grid=(m // tm, n // tn),
        in_specs=[pl.BlockSpec((tm, k), lambda i, j: (i, 0)), w_spec],
        out_specs=pl.BlockSpec((tm, tn), lambda i, j: (i, j)),
        out_shape=jax.ShapeDtypeStruct((m, n), out_dtype),
        compiler_params=_params(("parallel", "parallel")),
        name="dense_matmul",
    )(x, w)


def _cast_kernel(x_ref, o_ref):
    o_ref[...] = x_ref[...].astype(o_ref.dtype)


def _cast_bf16(x, tm=512):
    t, d = x.shape
    tm = min(tm, t)
    row = pl.BlockSpec((tm, d), lambda i: (i, 0))
    return pl.pallas_call(
        _cast_kernel, grid=(t // tm,), in_specs=[row], out_specs=row,
        out_shape=jax.ShapeDtypeStruct((t, d), BF16),
        compiler_params=_params(("parallel",)),
        name="cast_bf16",
    )(x)


def _merge_kernel(x_ref, wg_ref, b0_ref, b1_ref, b2_ref, b3_ref, wb_ref, o_ref, acc_ref, *, sub):
    n = pl.program_id(2)

    @pl.when((pl.program_id(0) == 0) & (pl.program_id(1) == 0) & (n == 0))
    def _():
        acc_ref[...] = jnp.zeros(acc_ref.shape, F32)

    branch = jnp.where(n == 0, b0_ref[...], jnp.where(n == 1, b1_ref[...],
                                                      jnp.where(n == 2, b2_ref[...], b3_ref[...])))
    x = x_ref[...]
    for c in range(o_ref.shape[1] // sub):
        cols = slice(c * sub, (c + 1) * sub)
        gate = jax.nn.sigmoid(_dot(x, wg_ref[0, 0, :, cols]))
        contrib = gate * _dot(branch, wb_ref[0, 0, :, cols])
        acc = jnp.where(n == 0, 0.0, acc_ref[:, cols]) + contrib
        acc_ref[:, cols] = acc
        o_ref[:, cols] = acc.astype(o_ref.dtype)


def _gated_merge(xb, wg, branches, wb, layer, tm=512, tn=1024, sub=256):
    t, d = xb.shape
    tm, tn = min(tm, t), min(tn, d)
    sub = min(sub, tn)
    bw = branches[0].shape[1]
    bspec = pl.BlockSpec((tm, bw), lambda i, j, n: (i, 0))
    return pl.pallas_call(
        functools.partial(_merge_kernel, sub=sub),
        grid=(t // tm, d // tn, 4),
        in_specs=[pl.BlockSpec((tm, d), lambda i, j, n: (i, 0)),
                  pl.BlockSpec((1, 1, d, tn), lambda i, j, n: (layer, n, 0, j)),
                  bspec, bspec, bspec, bspec,
                  pl.BlockSpec((1, 1, bw, tn), lambda i, j, n: (layer, n, 0, j))],
        out_specs=pl.BlockSpec((tm, tn), lambda i, j, n: (i, j)),
        out_shape=jax.ShapeDtypeStruct((t, d), BF16),
        scratch_shapes=[pltpu.VMEM((tm, tn), F32)],
        compiler_params=_params(("arbitrary", "arbitrary", "arbitrary")),
        name="gated_merge",
    )(xb, wg, *branches, wb)


def _ln_kernel(x_ref, y_ref, g_ref, b_ref, o_ref, ob_ref, *, alpha):
    r = alpha * x_ref[...] + y_ref[...]
    mu = jnp.mean(r, axis=-1, keepdims=True)
    c = r - mu
    var = jnp.mean(c * c, axis=-1, keepdims=True)
    out = c * lax.rsqrt(var + LN_EPS) * g_ref[...] + b_ref[...]
    o_ref[...] = out
    ob_ref[...] = out.astype(BF16)


def _residual_layernorm(x, y, gain, bias, alpha, tm=256):
    t, d = x.shape
    tm = min(tm, t)
    row = pl.BlockSpec((tm, d), lambda i: (i, 0))
    vec = pl.BlockSpec((1, d), lambda i: (0, 0))
    return pl.pallas_call(
        functools.partial(_ln_kernel, alpha=alpha),
        grid=(t // tm,),
        in_specs=[row, row, vec, vec],
        out_specs=[row, row],
        out_shape=[jax.ShapeDtypeStruct((t, d), F32), jax.ShapeDtypeStruct((t, d), BF16)],
        compiler_params=_params(("parallel",)),
        name="residual_layernorm",
    )(x, y, gain.reshape(1, d), bias.reshape(1, d))


def _forget_cumsum_kernel(f_ref, bias_ref, c_ref):
    rows, s = f_ref.shape
    lane = lax.broadcasted_iota(jnp.int32, (rows, LANES), 1)
    carry = jnp.zeros((rows, 1), F32)
    for c in range(s // LANES):
        x = jax.nn.log_sigmoid(f_ref[:, c * LANES:(c + 1) * LANES] + bias_ref[...])
        shift = 1
        while shift < LANES:
            x = x + jnp.where(lane >= shift, pltpu.roll(x, shift, 1), 0.0)
            shift *= 2
        x = x + carry
        c_ref[:, c * LANES:(c + 1) * LANES] = x * LOG2E
        carry = x[:, LANES - 1:LANES]


def _forget_cumsum(f_rows, bias_rows):
    rows, s = f_rows.shape
    full = pl.BlockSpec((rows, s), lambda: (0, 0))
    return pl.pallas_call(
        _forget_cumsum_kernel,
        in_specs=[full, pl.BlockSpec((rows, 1), lambda: (0, 0))],
        out_specs=full,
        out_shape=jax.ShapeDtypeStruct((rows, s), F32),
        name="forget_cumsum",
    )(f_rows, bias_rows)


def _tree(parts, op, ways=8):
    parts = list(parts)
    if len(parts) > ways:
        accs = parts[:ways]
        for a in range(ways, len(parts)):
            accs[a % ways] = op(accs[a % ways], parts[a])
        parts = accs
    while len(parts) > 1:
        nxt = [op(parts[a], parts[a + 1]) for a in range(0, len(parts) - 1, 2)]
        if len(parts) % 2:
            nxt.append(parts[-1])
        parts = nxt
    return parts[0]


def _col_reduce(x, op, reduce_fn):
    rows = x.shape[0]
    part = _tree([x[r:r + SUBLANES, :] for r in range(0, rows, SUBLANES)], op)
    return reduce_fn(part, axis=0, keepdims=True)


def _flash_kernel(i_tab, j_tab, *refs, fox, heads, dqk, dv, tile, slab):
    if fox:
        q_ref, k_ref, vt_ref, ck_ref, z_ref, o_ref, m_sc, l_sc, acc_sc = refs
    else:
        q_ref, k_ref, vt_ref, z_ref, o_ref, m_sc, l_sc, acc_sc = refs
    pair = pl.program_id(1)
    i = i_tab[pair]
    j = j_tab[pair]

    @pl.when(j == 0)
    def _():
        m_sc[...] = jnp.full(m_sc.shape, NEG, F32)
        l_sc[...] = jnp.zeros(l_sc.shape, F32)
        acc_sc[...] = jnp.zeros(acc_sc.shape, F32)

    def step(masked):
        if masked:
            causal = (lax.broadcasted_iota(jnp.int32, (tile, tile), 0)
                      <= lax.broadcasted_iota(jnp.int32, (tile, tile), 1))
        def logits(h):
            return _dot_nt(k_ref[:, h * dqk:(h + 1) * dqk], q_ref[:, h * dqk:(h + 1) * dqk])

        s_next = logits(0)
        for h in range(heads):
            s = s_next
            if h + 1 < heads:
                s_next = logits(h + 1)
            if fox:
                s = s - ck_ref[:, h:h + 1]
            if masked:
                s = jnp.where(causal, s, NEG)
            m_prev = m_sc[h:h + 1, :]
            m_new = jnp.maximum(m_prev, _col_reduce(s, jnp.maximum, jnp.max))
            alpha = jnp.exp2(m_prev - m_new)
            psum, pv = None, None
            for k0 in range(0, tile, slab):
                p = jnp.exp2(s[k0:k0 + slab, :] - m_new)
                part = _tree([p[r:r + SUBLANES, :] for r in range(0, slab, SUBLANES)], jnp.add)
                psum = part if psum is None else psum + part
                contrib = _dot(vt_ref[h * dv:(h + 1) * dv, k0:k0 + slab], p.astype(BF16))
                pv = contrib if pv is None else pv + contrib
            l_sc[h:h + 1, :] = alpha * l_sc[h:h + 1, :] + jnp.sum(psum, axis=0, keepdims=True)
            acc_sc[h * dv:(h + 1) * dv, :] = alpha * acc_sc[h * dv:(h + 1) * dv, :] + pv
            m_sc[h:h + 1, :] = m_new

    @pl.when(j < i)
    def _():
        step(False)

    @pl.when(j == i)
    def _():
        step(True)
        for h in range(heads):
            out_t = acc_sc[h * dv:(h + 1) * dv, :] / l_sc[h:h + 1, :]
            z = z_ref[:, h * dv:(h + 1) * dv].astype(F32)
            o_ref[:, h * dv:(h + 1) * dv] = (out_t.T * jax.nn.silu(z)).astype(o_ref.dtype)


def _flash_attention(q_arr, q_blk, k_arr, k_blk, vt_arr, z_arr, z_blk, *, batch, seq, heads,
                     dqk, dv, cum=None, tile=512):
    tile = min(tile, seq)
    nq = seq // tile
    fox = cum is not None
    pairs = [(i, j) for i in range(nq) for j in range(i + 1)]
    i_tab = jnp.asarray([p[0] for p in pairs], jnp.int32)
    j_tab = jnp.asarray([p[1] for p in pairs], jnp.int32)
    in_specs = [
        pl.BlockSpec((tile, heads * dqk), lambda b, p, it, jt: (b * nq + it[p], q_blk)),
        pl.BlockSpec((tile, heads * dqk), lambda b, p, it, jt: (b * nq + jt[p], k_blk)),
        pl.BlockSpec((heads * dv, tile), lambda b, p, it, jt: (0, b * nq + jt[p])),
    ]
    args = [q_arr, k_arr, vt_arr]
    if fox:
        in_specs.append(pl.BlockSpec((tile, LANES), lambda b, p, it, jt: (b * nq + jt[p], 0)))
        args.append(cum)
    in_specs.append(pl.BlockSpec((tile, heads * dv), lambda b, p, it, jt: (b * nq + it[p], z_blk)))
    args.append(z_arr)
    return pl.pallas_call(
        functools.partial(_flash_kernel, fox=fox, heads=heads, dqk=dqk, dv=dv, tile=tile,
                          slab=min(2 * LANES, tile)),
        grid_spec=pltpu.PrefetchScalarGridSpec(
            num_scalar_prefetch=2,
            grid=(batch, len(pairs)),
            in_specs=in_specs,
            out_specs=pl.BlockSpec((tile, heads * dv), lambda b, p, it, jt: (b * nq + it[p], 0)),
            scratch_shapes=[pltpu.VMEM((heads, tile), F32), pltpu.VMEM((heads, tile), F32),
                            pltpu.VMEM((heads * dv, tile), F32)]),
        out_shape=jax.ShapeDtypeStruct((batch * seq, heads * dv), BF16),
        compiler_params=_params(("parallel", "arbitrary")),
        name="flash_fox" if fox else "flash_latent",
    )(i_tab, j_tab, *args)


def _swa_kernel(q_ref, kvc_ref, kvp_ref, z_ref, sink_ref, o_ref, *, slopes):
    i = pl.program_id(1)
    w = WINDOW
    pairs_per_group = C_HEADS // C_KV_HEADS // 2
    kv = jnp.concatenate([kvp_ref[...], kvc_ref[...]], axis=0).astype(F32)
    lane = lax.broadcasted_iota(jnp.int32, (2 * w, LANES), 1)
    low = lane < C_DIM

    def halves(t):
        g0_lo = jnp.where(low, t, 0.0)
        g1_hi = jnp.where(low, 0.0, t)
        return ((g0_lo, pltpu.roll(g0_lo, C_DIM, 1)), (pltpu.roll(g1_hi, C_DIM, 1), g1_hi))

    k_half = halves(kv[:, :LANES])
    v_half = halves(kv[:, LANES:])
    key = lax.broadcasted_iota(jnp.int32, (2 * w, w), 0)
    qry = lax.broadcasted_iota(jnp.int32, (2 * w, w), 1)
    dist = qry + w - key
    first_key = jnp.where(i > 0, 0, w)
    valid = (dist >= 0) & (dist < w) & (key >= first_key)
    dist_m = jnp.where(valid, dist.astype(F32), -NEG)

    q_rows = [jnp.concatenate([q_ref[:, (g * pairs_per_group + jj) * LANES:(g * pairs_per_group + jj + 1) * LANES]
                               for jj in range(pairs_per_group)], axis=0) for g in range(C_KV_HEADS)]
    logits = [[_dot_nt(k_half[g][half].astype(BF16), q_rows[g]) for half in range(2)] for g in range(C_KV_HEADS)]
    for g in range(C_KV_HEADS):
        out_t = None
        for half in range(2):
            probs, inv = [], []
            for jj in range(pairs_per_group):
                h = 2 * (g * pairs_per_group + jj) + half
                s = logits[g][half][:, jj * w:(jj + 1) * w] - (slopes[h] * LOG2E) * dist_m
                sink = jnp.full((1, w), sink_ref[h], F32) * LOG2E
                m = jnp.maximum(_col_reduce(s, jnp.maximum, jnp.max), sink)
                p = jnp.exp2(s - m)
                inv.append(1.0 / (_col_reduce(p, jnp.add, jnp.sum) + jnp.exp2(sink - m)))
                probs.append(p.astype(BF16))
            vt = v_half[g][half].T.astype(BF16)
            contrib = _dot(vt, jnp.concatenate(probs, axis=1)) * jnp.concatenate(inv, axis=1)
            out_t = contrib if out_t is None else out_t + contrib
        for jj in range(pairs_per_group):
            cols = slice((g * pairs_per_group + jj) * LANES, (g * pairs_per_group + jj + 1) * LANES)
            z = z_ref[:, cols].astype(F32)
            o_ref[:, cols] = (out_t[:, jj * w:(jj + 1) * w].T * jax.nn.silu(z)).astype(o_ref.dtype)


def _sliding_window(p_arr, sinks, batch, seq):
    w = WINDOW
    nb = seq // w
    kv_blk = P_OFF['c_k'] // (2 * LANES)
    return pl.pallas_call(
        functools.partial(_swa_kernel, slopes=_alibi_slopes(C_HEADS)),
        grid=(batch, nb),
        in_specs=[pl.BlockSpec((w, BRANCH_WIDTH), lambda b, i: (b * nb + i, P_OFF['c_q'] // BRANCH_WIDTH)),
                  pl.BlockSpec((w, 2 * LANES), lambda b, i: (b * nb + i, kv_blk)),
                  pl.BlockSpec((w, 2 * LANES), lambda b, i: (b * nb + jnp.maximum(i - 1, 0), kv_blk)),
                  pl.BlockSpec((w, BRANCH_WIDTH), lambda b, i: (b * nb + i, P_OFF['c_z'] // BRANCH_WIDTH)),
                  pl.BlockSpec(memory_space=pltpu.SMEM)],
        out_specs=pl.BlockSpec((w, BRANCH_WIDTH), lambda b, i: (b * nb + i, 0)),
        out_shape=jax.ShapeDtypeStruct((batch * seq, BRANCH_WIDTH), BF16),
        compiler_params=_params(("parallel", "parallel")),
        name="sliding_window",
    )(p_arr, p_arr, p_arr, p_arr, sinks.astype(F32))


def _latent_prep_kernel(cq_ref, ckv_ref, kr_ref, tab_ref, qg_ref, kvg_ref, qup_ref, kup_ref, vupt_ref,
                        q_ref, k_ref, vt_ref):
    def rms(x, g):
        return x * lax.rsqrt(jnp.mean(x * x, axis=-1, keepdims=True) + RMS_EPS) * g

    tab = tab_ref[...]
    lane = lax.broadcasted_iota(jnp.int32, tab.shape, 1)

    def rotate(t):
        r = t * tab
        return r + pltpu.roll(r, D_ROPE, 1)

    q = _dot(rms(cq_ref[...].astype(F32), qg_ref[...]).astype(BF16), qup_ref[...])
    ckv = rms(ckv_ref[...].astype(F32), kvg_ref[...]).astype(BF16)
    k_nope = _dot(ckv, kup_ref[...])
    vt_ref[...] = _dot_nt(vupt_ref[...], ckv).astype(vt_ref.dtype)
    k_rot = jnp.where(lane < D_ROPE, rotate(kr_ref[...].astype(F32)), 0.0).astype(k_ref.dtype)
    hw = 2 * LANES
    for h in range(D_HEADS):
        q_ref[:, h * hw:h * hw + LANES] = q[:, h * hw:h * hw + LANES].astype(q_ref.dtype)
        q_ref[:, h * hw + LANES:(h + 1) * hw] = rotate(q[:, h * hw + LANES:(h + 1) * hw]).astype(q_ref.dtype)
        k_ref[:, h * hw:h * hw + LANES] = k_nope[:, h * LANES:(h + 1) * LANES].astype(k_ref.dtype)
        k_ref[:, h * hw + LANES:(h + 1) * hw] = k_rot


def _latent_prep(p_arr, tab, q_gain, kv_gain, q_up, k_up, v_up_t, tm=512):
    t = p_arr.shape[0]
    tm = min(tm, t)
    hw = 2 * LANES

    def const(shape):
        return pl.BlockSpec(shape, lambda i: (0, 0))

    return pl.pallas_call(
        _latent_prep_kernel,
        grid=(t // tm,),
        in_specs=[pl.BlockSpec((tm, D_Q_LORA), lambda i: (i, P_OFF['d_cq'] // D_Q_LORA)),
                  pl.BlockSpec((tm, D_KV_LORA), lambda i: (i, P_OFF['d_ckv'] // D_KV_LORA)),
                  pl.BlockSpec((tm, LANES), lambda i: (i, P_OFF['d_kr'] // LANES)),
                  pl.BlockSpec((tm, LANES), lambda i: (i, 0)),
                  const((1, D_Q_LORA)), const((1, D_KV_LORA)),
                  const((D_Q_LORA, D_HEADS * hw)), const((D_KV_LORA, D_HEADS * LANES)),
                  const((D_HEADS * D_VDIM, D_KV_LORA))],
        out_specs=[pl.BlockSpec((tm, D_HEADS * hw), lambda i: (i, 0)),
                   pl.BlockSpec((tm, D_HEADS * hw), lambda i: (i, 0)),
                   pl.BlockSpec((D_HEADS * D_VDIM, tm), lambda i: (0, i))],
        out_shape=[jax.ShapeDtypeStruct((t, D_HEADS * hw), BF16),
                   jax.ShapeDtypeStruct((t, D_HEADS * hw), BF16),
                   jax.ShapeDtypeStruct((D_HEADS * D_VDIM, t), BF16)],
        compiler_params=_params(("parallel",)),
        name="latent_prep",
    )(p_arr, p_arr, p_arr, tab, q_gain.reshape(1, -1), kv_gain.reshape(1, -1), q_up, k_up, v_up_t)


def _sparse_kernel(q_ref, iq_ref, z_ref, ik_ref, k_ref, vt_ref, wt_ref, o_ref,
                   keys_ref, jcut_ref, m_ref, l_ref, acc_ref, *, topk, ck, slopes, idx_bits):
    i = pl.program_id(1)
    tq = q_ref.shape[0]
    nch = ((i + 1) * tq + ck - 1) // ck
    slab = min(2 * LANES, ck)
    q_pos = i * tq + lax.broadcasted_iota(jnp.int32, (ck, tq), 1)
    s_iota = lax.broadcasted_iota(jnp.int32, (ck, tq), 0)
    lane = lax.broadcasted_iota(jnp.int32, (ck, LANES), 1)
    low = lane < IDX_DIM
    w_all = wt_ref[0] * (IDX_HEADS ** -0.5 * IDX_DIM ** -0.5)

    def score_chunk(c, carry):
        start = pl.multiple_of(c * ck, ck)
        ikc = ik_ref[pl.ds(start, ck), :]
        ik_lo = jnp.where(low, ikc, jnp.zeros_like(ikc))
        ik_hi = jnp.where(low, jnp.zeros_like(ikc), ikc)
        acc = jnp.zeros((ck, tq), F32)
        for pair in range(IDX_HEADS // 2):
            iqp = iq_ref[:, pair * LANES:(pair + 1) * LANES]
            acc = acc + w_all[2 * pair:2 * pair + 1, :] * jnp.maximum(_dot_nt(ik_lo, iqp), 0.0)
            acc = acc + w_all[2 * pair + 1:2 * pair + 2, :] * jnp.maximum(_dot_nt(ik_hi, iqp), 0.0)
        acc = jnp.where(acc == 0.0, 0.0, acc)
        bits = lax.bitcast_convert_type(acc, jnp.int32)
        key = bits ^ ((bits >> 31) & 0x7FFFFFFF)
        key = jnp.where(start + s_iota <= q_pos, key, INT_MIN)
        keys_ref[pl.ds(start, ck), :] = key
        return carry

    lax.fori_loop(0, nch, score_chunk, 0)

    def count(pred_fn):
        def body(c, cnt):
            start = pl.multiple_of(c * ck, ck)
            hit = pred_fn(keys_ref[pl.ds(start, ck), :], start + s_iota)
            ways = 8
            accs = [cnt] + [jnp.zeros((SUBLANES, tq), F32)] * (ways - 1)
            for a, r in enumerate(range(0, ck, SUBLANES)):
                accs[a % ways] = jnp.where(hit[r:r + SUBLANES, :], accs[a % ways] + 1.0, accs[a % ways])
            return _tree(accs, jnp.add)
        part = lax.fori_loop(0, nch, body, jnp.zeros((SUBLANES, tq), F32))
        return jnp.sum(part, axis=0, keepdims=True)

    kf = float(topk)

    def thr_step(it, state):
        thr, n_ge = state
        cand = thr ^ (jnp.int32(1) << (31 - it))
        cnt = count(lambda kc, pos: kc >= cand)
        ok = cnt >= kf
        return jnp.where(ok, cand, thr), jnp.where(ok, cnt, n_ge)

    total = jnp.full((1, tq), 1.0, F32) * (nch * ck).astype(F32)
    thr, n_ge = lax.fori_loop(0, 32, thr_step, (jnp.full((1, tq), INT_MIN, jnp.int32), total))
    tied = (n_ge > kf) & (thr != INT_MIN)
    jcut_ref[...] = jnp.full(jcut_ref.shape, 2 ** 30, jnp.int32)

    @pl.when(jnp.max(tied.astype(F32)) > 0.0)
    def _():
        need = kf - count(lambda kc, pos: kc > thr)

        def cut_step(it, x):
            cand = x + (jnp.int32(1) << (idx_bits - 1 - it))
            cnt = count(lambda kc, pos: (kc == thr) & (pos < cand))
            return jnp.where(cnt < need, cand, x)
        x = lax.fori_loop(0, idx_bits, cut_step, jnp.zeros((1, tq), jnp.int32))
        jcut_ref[...] = jnp.broadcast_to(jnp.where(tied, x, 2 ** 30), jcut_ref.shape)

    jcut = jcut_ref[0:1, :]

    m_ref[...] = jnp.full(m_ref.shape, NEG, F32)
    l_ref[...] = jnp.zeros(l_ref.shape, F32)
    acc_ref[...] = jnp.zeros(acc_ref.shape, F32)
    rep = A_HEADS // A_KV_HEADS
    q_groups = [jnp.concatenate([q_ref[:, (g * rep + r) * A_DIM:(g * rep + r + 1) * A_DIM] for r in range(rep)],
                                axis=0) for g in range(A_KV_HEADS)]

    def attend_chunk(c, carry):
        start = pl.multiple_of(c * ck, ck)
        kc = keys_ref[pl.ds(start, ck), :]
        pos = start + s_iota
        sel = ((kc > thr) | ((kc == thr) & (pos <= jcut))) & (pos <= q_pos)
        dist = jnp.where(sel, (q_pos - pos).astype(F32), FAR)
        logits_g = [_dot_nt(k_ref[pl.ds(start, ck), g * A_DIM:(g + 1) * A_DIM], q_groups[g])
                    for g in range(A_KV_HEADS)]
        for g in range(A_KV_HEADS):
            vtg = vt_ref[0, c, g * A_DIM:(g + 1) * A_DIM, :]
            for r in range(rep):
                h = g * rep + r
                s = logits_g[g][:, r * tq:(r + 1) * tq] - (slopes[h] * LOG2E) * dist
                m_prev = m_ref[h:h + 1, :]
                m_new = jnp.maximum(m_prev, _col_reduce(s, jnp.maximum, jnp.max))
                alpha = jnp.exp2(m_prev - m_new)
                psum, pv = None, None
                for k0 in range(0, ck, slab):
                    p = jnp.exp2(s[k0:k0 + slab, :] - m_new)
                    part = _tree([p[r0:r0 + SUBLANES, :] for r0 in range(0, slab, SUBLANES)], jnp.add)
                    psum = part if psum is None else psum + part
                    contrib = _dot(vtg[:, k0:k0 + slab], p.astype(BF16))
                    pv = contrib if pv is None else pv + contrib
                l_ref[h:h + 1, :] = alpha * l_ref[h:h + 1, :] + jnp.sum(psum, axis=0, keepdims=True)
                acc_ref[h * A_DIM:(h + 1) * A_DIM, :] = alpha * acc_ref[h * A_DIM:(h + 1) * A_DIM, :] + pv
                m_ref[h:h + 1, :] = m_new
        return carry

    lax.fori_loop(0, nch, attend_chunk, 0)

    for h in range(A_HEADS):
        out_t = acc_ref[h * A_DIM:(h + 1) * A_DIM, :] / l_ref[h:h + 1, :]
        z = z_ref[:, h * A_DIM:(h + 1) * A_DIM].astype(F32)
        o_ref[:, h * A_DIM:(h + 1) * A_DIM] = (out_t.T * jax.nn.silu(z)).astype(o_ref.dtype)


def _key_chunk(seq):
    return min(512, seq)


def _sparse_attention(p_arr, vt, wt, batch, seq):
    tq = min(2 * LANES, seq)
    nq = seq // tq
    ck = _key_chunk(seq)
    topk = min(IDX_TOPK_MAX, seq // 4)
    idx_bits = int(seq).bit_length()
    return pl.pallas_call(
        functools.partial(_sparse_kernel, topk=topk, ck=ck, slopes=_alibi_slopes(A_HEADS), idx_bits=idx_bits),
        grid=(batch, nq),
        in_specs=[pl.BlockSpec((tq, BRANCH_WIDTH), lambda b, i: (b * nq + i, P_OFF['a_q'] // BRANCH_WIDTH)),
                  pl.BlockSpec((tq, BRANCH_WIDTH), lambda b, i: (b * nq + i, P_OFF['a_iq'] // BRANCH_WIDTH)),
                  pl.BlockSpec((tq, BRANCH_WIDTH), lambda b, i: (b * nq + i, P_OFF['a_z'] // BRANCH_WIDTH)),
                  pl.BlockSpec((seq, LANES), lambda b, i: (b, P_OFF['a_ik'] // LANES)),
                  pl.BlockSpec((seq, 2 * LANES), lambda b, i: (b, P_OFF['a_k'] // (2 * LANES))),
                  pl.BlockSpec((1, seq // ck, 2 * LANES, ck), lambda b, i: (b, 0, 0, 0)),
                  pl.BlockSpec((1, IDX_HEADS, tq), lambda b, i: (b, 0, i))],
        out_specs=pl.BlockSpec((tq, BRANCH_WIDTH), lambda b, i: (b * nq + i, 0)),
        out_shape=jax.ShapeDtypeStruct((batch * seq, BRANCH_WIDTH), BF16),
        scratch_shapes=[pltpu.VMEM((seq, tq), jnp.int32), pltpu.VMEM((8, tq), jnp.int32),
                        pltpu.VMEM((A_HEADS, tq), F32), pltpu.VMEM((A_HEADS, tq), F32),
                        pltpu.VMEM((A_HEADS * A_DIM, tq), F32)],
        compiler_params=_params(("parallel", "arbitrary")),
        name="sparse_attention",
    )(p_arr, p_arr, p_arr, p_arr, p_arr, vt, wt)


def _swap_halves(w):
    half = w.shape[-1] // 2
    return jnp.concatenate([-w[..., half:], w[..., :half]], axis=-1)


def _prepare_in_proj(w_in):
    sizes = [n for _, n in IN_SEGMENTS]
    offs = np.concatenate([[0], np.cumsum(sizes)])
    span = {name: (int(offs[k]), int(offs[k + 1])) for k, (name, _) in enumerate(IN_SEGMENTS)}
    wt32 = jnp.swapaxes(w_in, 1, 2)
    wt16 = wt32.astype(BF16)
    seg16 = lambda name: wt16[:, span[name][0]:span[name][1], :]
    seg32 = lambda name: wt32[:, span[name][0]:span[name][1], :]
    rows = []
    for name, _ in P_LAYOUT:
        if name == 'a_ik':
            rows += [seg16('a_ik'), seg16('a_ik')]
        elif name == 'd_kr':
            kr = seg16('d_kr')
            rows += [kr, -kr[:, D_ROPE // 2:, :], kr[:, :D_ROPE // 2, :]]
        elif name == 'a_q':
            rows.append((seg32('a_q') * (A_DIM ** -0.5 * LOG2E)).astype(BF16))
        elif name == 'b_q':
            rows.append((seg32('b_q') * (B_DIM ** -0.5 * LOG2E)).astype(BF16))
        elif name == 'c_q':
            rows.append((seg32('c_q') * (C_DIM ** -0.5 * LOG2E)).astype(BF16))
        else:
            rows.append(seg16(name))
    w_main_t = jnp.concatenate(rows, axis=1)
    pad = jnp.zeros((w_in.shape[0], LANES - IDX_HEADS - B_HEADS, w_in.shape[1]), BF16)
    w_small_t = jnp.concatenate([seg16('a_iw'), seg16('b_f'), pad], axis=1)
    return w_main_t, w_small_t


def _prepare_latent(dq_up, dkv_up):
    q = dq_up.reshape(D_Q_LORA, D_HEADS, D_NOPE + D_ROPE) * ((D_NOPE + D_ROPE) ** -0.5 * LOG2E)
    rope = q[..., D_NOPE:]
    q_up = jnp.concatenate([q[..., :D_NOPE], rope, _swap_halves(rope)], axis=-1)
    kv = dkv_up.reshape(D_KV_LORA, D_HEADS, D_NOPE + D_VDIM)
    return (q_up.reshape(D_Q_LORA, -1).astype(BF16),
            kv[..., :D_NOPE].reshape(D_KV_LORA, -1).astype(BF16),
            kv[..., D_NOPE:].reshape(D_KV_LORA, -1).T.astype(BF16))


def _rope_table(positions):
    half = D_ROPE // 2
    inv_freq = ROPE_THETA ** (-jnp.arange(half, dtype=F32) / half)
    ang = positions.astype(F32)[..., None] * inv_freq
    cos, sin = jnp.cos(ang), jnp.sin(ang)
    tab = jnp.concatenate([cos, cos, sin, sin], axis=-1)
    return tab.reshape(-1, 4 * half)


def _layer(layer, x, xb, tab, batch, seq, w_main, w_small, w_gate, w_branch, w_out, dq_gain, dq_up, dkv_gain,
           dkv_up, f_bias, sinks, ln_gain, ln_bias, alpha):
    proj = _matmul(xb, w_main, layer, BF16, transposed=True)
    small = _matmul(xb, w_small, layer, F32, transposed=True, tn=LANES)

    ck = _key_chunk(seq)
    wt = small[:, :IDX_HEADS].reshape(batch, seq, IDX_HEADS).transpose(0, 2, 1)
    a_v = proj[:, P_OFF['a_v']:P_OFF['a_v'] + 2 * LANES]
    vt = a_v.reshape(batch, seq // ck, ck, 2 * LANES).transpose(0, 1, 3, 2)
    o_a = _sparse_attention(proj, vt, wt, batch, seq)

    f_rows = small[:, IDX_HEADS:IDX_HEADS + B_HEADS].reshape(batch, seq, B_HEADS).transpose(0, 2, 1)
    cum = _forget_cumsum(f_rows.reshape(batch * B_HEADS, seq),
                         jnp.tile(f_bias.astype(F32), batch).reshape(batch * B_HEADS, 1))
    cum = cum.reshape(batch, B_HEADS, seq).transpose(0, 2, 1).reshape(batch * seq, B_HEADS)
    cum = jnp.pad(cum, ((0, 0), (0, LANES - B_HEADS)))
    vt_b = proj[:, P_OFF['b_v']:P_OFF['b_v'] + BRANCH_WIDTH].T
    o_b = _flash_attention(proj, P_OFF['b_q'] // BRANCH_WIDTH, proj, P_OFF['b_k'] // BRANCH_WIDTH, vt_b,
                           proj, P_OFF['b_z'] // BRANCH_WIDTH, batch=batch, seq=seq, heads=B_HEADS,
                           dqk=B_DIM, dv=B_DIM, cum=cum)

    o_c = _sliding_window(proj, sinks, batch, seq)

    q_up, k_up, v_up_t = _prepare_latent(dq_up, dkv_up)
    q_d, k_d, vt_d = _latent_prep(proj, tab, dq_gain, dkv_gain, q_up, k_up, v_up_t)
    o_d = _flash_attention(q_d, 0, k_d, 0, vt_d, proj, P_OFF['d_z'] // BRANCH_WIDTH, batch=batch, seq=seq,
                           heads=D_HEADS, dqk=2 * LANES, dv=D_VDIM)

    merged = _gated_merge(xb, w_gate, (o_a, o_b, o_c, o_d), w_branch, layer)
    y = _matmul(merged, w_out, layer, F32)
    return _residual_layernorm(x, y, ln_gain, ln_bias, alpha)


def kernel(x, positions, w_in, w_gate, w_branch, w_out, dq_gain, dq_up, dkv_gain, dkv_up, f_bias, sinks,
           ln_gain, ln_bias):
    batch, seq, d = x.shape
    depth = w_in.shape[0]
    alpha = (2 * depth) ** 0.25
    tab = _rope_table(positions)
    xf = x.reshape(batch * seq, d)
    xb = _cast_bf16(xf)
    w_main, w_small = _prepare_in_proj(w_in)
    w_gate, w_branch, w_out = w_gate.astype(BF16), w_branch.astype(BF16), w_out.astype(BF16)
    for l in range(depth):
        xf, xb = _layer(l, xf, xb, tab, batch, seq, w_main, w_small, w_gate, w_branch, w_out, dq_gain[l],
                        dq_up[l], dkv_gain[l], dkv_up[l], f_bias[l], sinks[l], ln_gain[l], ln_bias[l], alpha)
    return xf.reshape(batch, seq, d)
```

```python
import functools

import numpy as np
import jax
import jax.numpy as jnp
from jax import lax
from jax.experimental import pallas as pl
from jax.experimental.pallas import tpu as pltpu

BRANCH_WIDTH = 1024
A_HEADS, A_KV_HEADS, A_DIM = 8, 2, 128
IDX_HEADS, IDX_DIM, IDX_TOPK_MAX = 16, 64, 256
B_HEADS, B_DIM = 8, 128
C_HEADS, C_KV_HEADS, C_DIM, WINDOW = 16, 2, 64, 128
D_HEADS, D_Q_LORA, D_KV_LORA, D_NOPE, D_ROPE, D_VDIM = 8, 768, 256, 128, 64, 128
ROPE_THETA = 10000.0
RMS_EPS = 1e-6
LN_EPS = 1e-5

IN_SEGMENTS = (
    ('a_q', 1024), ('a_k', 256), ('a_v', 256), ('a_iq', 1024), ('a_ik', 64), ('a_iw', 16), ('a_z', 1024),
    ('b_q', 1024), ('b_k', 1024), ('b_v', 1024), ('b_f', 8), ('b_z', 1024),
    ('c_q', 1024), ('c_k', 128), ('c_v', 128), ('c_z', 1024),
    ('d_cq', 768), ('d_ckv', 256), ('d_kr', 64), ('d_z', 1024),
)

P_LAYOUT = (
    ('d_cq', 768), ('d_ckv', 256), ('a_q', 1024), ('a_iq', 1024), ('a_z', 1024),
    ('b_q', 1024), ('b_k', 1024), ('b_v', 1024), ('b_z', 1024), ('c_q', 1024), ('c_z', 1024),
    ('d_z', 1024), ('a_k', 256), ('a_v', 256), ('a_ik', 128), ('d_kr', 128), ('c_k', 128), ('c_v', 128),
)
P_OFF = {}
_o = 0
for _n, _w in P_LAYOUT:
    assert _o % _w == 0
    P_OFF[_n] = _o
    _o += _w
P_WIDTH = _o

LANES = 128
SUBLANES = 8
NEG = -1e30
FAR = 1e34
LOG2E = 1.4426950408889634
INT_MIN = -2 ** 31
VMEM_LIMIT = 56 * 1024 * 1024

F32 = jnp.float32
BF16 = jnp.bfloat16


def _params(sem, vmem=VMEM_LIMIT):
    return pltpu.CompilerParams(dimension_semantics=sem, vmem_limit_bytes=vmem)


def _dot(a, b):
    return jnp.dot(a, b, preferred_element_type=F32)


def _dot_nt(a, b):
    return lax.dot_general(a, b, (((1,), (1,)), ((), ())), preferred_element_type=F32)


def _alibi_slopes(n_heads):
    return [float(np.float32(2.0 ** (-8.0 * (h + 1) / n_heads))) for h in range(n_heads)]


def _mm_kernel(x_ref, w_ref, o_ref):
    o_ref[...] = _dot(x_ref[...], w_ref[0]).astype(o_ref.dtype)


def _matmul(x, w, layer, out_dtype, tm=1024, tn=1024):
    m, k = x.shape
    n = w.shape[2]
    tm, tn = min(tm, m), min(tn, n)
    assert m % tm == 0 and n % tn == 0, (m, n, tm, tn)
    return pl.pallas_call(
        _mm_kernel,
        grid=(m // tm, n // tn),
        in_specs=[pl.BlockSpec((tm, k), lambda i, j: (i, 0)),
                  pl.BlockSpec((1, k, tn), lambda i, j: (layer, 0, j))],
        out_specs=pl.BlockSpec((tm, tn), lambda i, j: (i, j)),
        out_shape=jax.ShapeDtypeStruct((m, n), out_dtype),
        compiler_params=_params(("parallel", "parallel")),
        name="dense_matmul",
    )(x, w)


def _in_proj_kernel(x_ref, wt_ref, wst_ref, o_ref, os_ref):
    o_ref[...] = _dot_nt(x_ref[...], wt_ref[0]).astype(o_ref.dtype)

    @pl.when(pl.program_id(1) == 0)
    def _():
        os_ref[...] = _dot_nt(x_ref[...], wst_ref[0])


def _input_projection(x, w_main_t, w_small_t, layer, tm=1024, tn=1024):
    m, k = x.shape
    n, ns = w_main_t.shape[1], w_small_t.shape[1]
    tm, tn = min(tm, m), min(tn, n)
    assert m % tm == 0 and n % tn == 0, (m, n, tm, tn)
    return pl.pallas_call(
        _in_proj_kernel,
        grid=(m // tm, n // tn),
        in_specs=[pl.BlockSpec((tm, k), lambda i, j: (i, 0)),
                  pl.BlockSpec((1, tn, k), lambda i, j: (layer, j, 0)),
                  pl.BlockSpec((1, ns, k), lambda i, j: (layer, 0, 0))],
        out_specs=[pl.BlockSpec((tm, tn), lambda i, j: (i, j)),
                   pl.BlockSpec((tm, ns), lambda i, j: (i, 0))],
        out_shape=[jax.ShapeDtypeStruct((m, n), BF16), jax.ShapeDtypeStruct((m, ns), F32)],
        compiler_params=_params(("parallel", "arbitrary")),
        name="input_projection",
    )(x, w_main_t, w_small_t)


def _cast_kernel(x_ref, o_ref):
    o_ref[...] = x_ref[...].astype(o_ref.dtype)


def _cast_bf16(x, tm=512):
    t, d = x.shape
    tm = min(tm, t)
    row = pl.BlockSpec((tm, d), lambda i: (i, 0))
    return pl.pallas_call(
        _cast_kernel, grid=(t // tm,), in_specs=[row], out_specs=row,
        out_shape=jax.ShapeDtypeStruct((t, d), BF16),
        compiler_params=_params(("parallel",)),
        name="cast_bf16",
    )(x)


def _merge_kernel(x_ref, wg_ref, b0_ref, b1_ref, b2_ref, b3_ref, wb_ref, o_ref, acc_ref, *, sub):
    n = pl.program_id(2)

    @pl.when((pl.program_id(0) == 0) & (pl.program_id(1) == 0) & (n == 0))
    def _():
        acc_ref[...] = jnp.zeros(acc_ref.shape, F32)

    branch = jnp.where(n == 0, b0_ref[...], jnp.where(n == 1, b1_ref[...],
                                                      jnp.where(n == 2, b2_ref[...], b3_ref[...])))
    x = x_ref[...]
    for c in range(o_ref.shape[1] // sub):
        cols = slice(c * sub, (c + 1) * sub)
        gate = jax.nn.sigmoid(_dot(x, wg_ref[0, 0, :, cols]))
        contrib = gate * _dot(branch, wb_ref[0, 0, :, cols])
        acc = jnp.where(n == 0, 0.0, acc_ref[:, cols]) + contrib
        acc_ref[:, cols] = acc
        o_ref[:, cols] = acc.astype(o_ref.dtype)


def _gated_merge(xb, wg, branches, wb, layer, tm=512, tn=1024, sub=256):
    t, d = xb.shape
    tm, tn = min(tm, t), min(tn, d)
    sub = min(sub, tn)
    bw = branches[0].shape[1]
    bspec = pl.BlockSpec((tm, bw), lambda i, j, n: (i, 0))
    return pl.pallas_call(
        functools.partial(_merge_kernel, sub=sub),
        grid=(t // tm, d // tn, 4),
        in_specs=[pl.BlockSpec((tm, d), lambda i, j, n: (i, 0)),
                  pl.BlockSpec((1, 1, d, tn), lambda i, j, n: (layer, n, 0, j)),
                  bspec, bspec, bspec, bspec,
                  pl.BlockSpec((1, 1, bw, tn), lambda i, j, n: (layer, n, 0, j))],
        out_specs=pl.BlockSpec((tm, tn), lambda i, j, n: (i, j)),
        out_shape=jax.ShapeDtypeStruct((t, d), BF16),
        scratch_shapes=[pltpu.VMEM((tm, tn), F32)],
        compiler_params=_params(("arbitrary", "arbitrary", "arbitrary")),
        name="gated_merge",
    )(xb, wg, *branches, wb)


def _ln_kernel(x_ref, y_ref, g_ref, b_ref, o_ref, *maybe_ob_ref, alpha):
    r = alpha * x_ref[...] + y_ref[...]
    mu = jnp.mean(r, axis=-1, keepdims=True)
    c = r - mu
    var = jnp.mean(c * c, axis=-1, keepdims=True)
    out = c * lax.rsqrt(var + LN_EPS) * g_ref[...] + b_ref[...]
    o_ref[...] = out
    for ob_ref in maybe_ob_ref:
        ob_ref[...] = out.astype(BF16)


def _residual_layernorm(x, y, gain, bias, alpha, with_bf16, tm=256):
    t, d = x.shape
    tm = min(tm, t)
    row = pl.BlockSpec((tm, d), lambda i: (i, 0))
    vec = pl.BlockSpec((1, d), lambda i: (0, 0))
    out_shape = [jax.ShapeDtypeStruct((t, d), F32)] + ([jax.ShapeDtypeStruct((t, d), BF16)] if with_bf16 else [])
    outs = pl.pallas_call(
        functools.partial(_ln_kernel, alpha=alpha),
        grid=(t // tm,),
        in_specs=[row, row, vec, vec],
        out_specs=[row] * len(out_shape),
        out_shape=out_shape,
        compiler_params=_params(("parallel",)),
        name="residual_layernorm",
    )(x, y, gain.reshape(1, d), bias.reshape(1, d))
    return (outs[0], outs[1]) if with_bf16 else (outs[0], None)


def _forget_cumsum_kernel(f_ref, bias_ref, c_ref):
    rows, s = f_ref.shape
    lane = lax.broadcasted_iota(jnp.int32, (rows, LANES), 1)
    carry = jnp.zeros((rows, 1), F32)
    for c in range(s // LANES):
        x = jax.nn.log_sigmoid(f_ref[:, c * LANES:(c + 1) * LANES] + bias_ref[...])
        shift = 1
        while shift < LANES:
            x = x + jnp.where(lane >= shift, pltpu.roll(x, shift, 1), 0.0)
            shift *= 2
        x = x + carry
        c_ref[:, c * LANES:(c + 1) * LANES] = x * LOG2E
        carry = x[:, LANES - 1:LANES]


def _forget_cumsum(f_rows, bias_rows):
    rows, s = f_rows.shape
    full = pl.BlockSpec((rows, s), lambda: (0, 0))
    return pl.pallas_call(
        _forget_cumsum_kernel,
        in_specs=[full, pl.BlockSpec((rows, 1), lambda: (0, 0))],
        out_specs=full,
        out_shape=jax.ShapeDtypeStruct((rows, s), F32),
        name="forget_cumsum",
    )(f_rows, bias_rows)


def _tree(parts, op, ways=8):
    parts = list(parts)
    if len(parts) > ways:
        accs = parts[:ways]
        for a in range(ways, len(parts)):
            accs[a % ways] = op(accs[a % ways], parts[a])
        parts = accs
    while len(parts) > 1:
        nxt = [op(parts[a], parts[a + 1]) for a in range(0, len(parts) - 1, 2)]
        if len(parts) % 2:
            nxt.append(parts[-1])
        parts = nxt
    return parts[0]


def _col_reduce(x, op, reduce_fn):
    rows = x.shape[0]
    part = _tree([x[r:r + SUBLANES, :] for r in range(0, rows, SUBLANES)], op)
    return reduce_fn(part, axis=0, keepdims=True)


def _flash_kernel(i_tab, j_tab, *refs, fox, heads, dqk, dv, tile, slab):
    if fox:
        q_ref, k_ref, vt_ref, ck_ref, z_ref, o_ref, m_sc, l_sc, acc_sc = refs
    else:
        q_ref, k_ref, vt_ref, z_ref, o_ref, m_sc, l_sc, acc_sc = refs
    pair = pl.program_id(1)
    i = i_tab[pair]
    j = j_tab[pair]

    @pl.when(j == 0)
    def _():
        m_sc[...] = jnp.full(m_sc.shape, NEG, F32)
        l_sc[...] = jnp.zeros(l_sc.shape, F32)
        acc_sc[...] = jnp.zeros(acc_sc.shape, F32)

    def step(masked):
        if masked:
            causal = (lax.broadcasted_iota(jnp.int32, (tile, tile), 0)
                      <= lax.broadcasted_iota(jnp.int32, (tile, tile), 1))
        def logits(h):
            return _dot_nt(k_ref[:, h * dqk:(h + 1) * dqk], q_ref[:, h * dqk:(h + 1) * dqk])

        s_next = logits(0)
        for h in range(heads):
            s = s_next
            if h + 1 < heads:
                s_next = logits(h + 1)
            if fox:
                s = s - ck_ref[:, h:h + 1]
            if masked:
                s = jnp.where(causal, s, NEG)
            m_prev = m_sc[h:h + 1, :]
            m_new = jnp.maximum(m_prev, _col_reduce(s, jnp.maximum, jnp.max))
            alpha = jnp.exp2(m_prev - m_new)
            psum, pv = None, None
            for k0 in range(0, tile, slab):
                p = jnp.exp2(s[k0:k0 + slab, :] - m_new)
                part = _tree([p[r:r + SUBLANES, :] for r in range(0, slab, SUBLANES)], jnp.add)
                psum = part if psum is None else psum + part
                contrib = _dot(vt_ref[h * dv:(h + 1) * dv, k0:k0 + slab], p.astype(BF16))
                pv = contrib if pv is None else pv + contrib
            l_sc[h:h + 1, :] = alpha * l_sc[h:h + 1, :] + jnp.sum(psum, axis=0, keepdims=True)
            acc_sc[h * dv:(h + 1) * dv, :] = alpha * acc_sc[h * dv:(h + 1) * dv, :] + pv
            m_sc[h:h + 1, :] = m_new

    @pl.when(j < i)
    def _():
        step(False)

    @pl.when(j == i)
    def _():
        step(True)
        for h in range(heads):
            out_t = acc_sc[h * dv:(h + 1) * dv, :] / l_sc[h:h + 1, :]
            z = z_ref[:, h * dv:(h + 1) * dv].astype(F32)
            o_ref[:, h * dv:(h + 1) * dv] = (out_t.T * jax.nn.silu(z)).astype(o_ref.dtype)


def _flash_attention(q_arr, q_blk, k_arr, k_blk, vt_arr, z_arr, z_blk, *, batch, seq, heads,
                     dqk, dv, cum=None, tile=512):
    tile = min(tile, seq)
    assert seq % tile == 0, (seq, tile)
    nq = seq // tile
    fox = cum is not None
    pairs = [(i, j) for i in range(nq) for j in range(i + 1)]
    i_tab = jnp.asarray([p[0] for p in pairs], jnp.int32)
    j_tab = jnp.asarray([p[1] for p in pairs], jnp.int32)
    in_specs = [
        pl.BlockSpec((tile, heads * dqk), lambda b, p, it, jt: (b * nq + it[p], q_blk)),
        pl.BlockSpec((tile, heads * dqk), lambda b, p, it, jt: (b * nq + jt[p], k_blk)),
        pl.BlockSpec((heads * dv, tile), lambda b, p, it, jt: (0, b * nq + jt[p])),
    ]
    args = [q_arr, k_arr, vt_arr]
    if fox:
        in_specs.append(pl.BlockSpec((tile, LANES), lambda b, p, it, jt: (b * nq + jt[p], 0)))
        args.append(cum)
    in_specs.append(pl.BlockSpec((tile, heads * dv), lambda b, p, it, jt: (b * nq + it[p], z_blk)))
    args.append(z_arr)
    return pl.pallas_call(
        functools.partial(_flash_kernel, fox=fox, heads=heads, dqk=dqk, dv=dv, tile=tile,
                          slab=min(2 * LANES, tile)),
        grid_spec=pltpu.PrefetchScalarGridSpec(
            num_scalar_prefetch=2,
            grid=(batch, len(pairs)),
            in_specs=in_specs,
            out_specs=pl.BlockSpec((tile, heads * dv), lambda b, p, it, jt: (b * nq + it[p], 0)),
            scratch_shapes=[pltpu.VMEM((heads, tile), F32), pltpu.VMEM((heads, tile), F32),
                            pltpu.VMEM((heads * dv, tile), F32)]),
        out_shape=jax.ShapeDtypeStruct((batch * seq, heads * dv), BF16),
        compiler_params=_params(("parallel", "arbitrary")),
        name="flash_fox" if fox else "flash_latent",
    )(i_tab, j_tab, *args)


def _swa_kernel(q_ref, kvc_ref, kvp_ref, z_ref, sink_ref, o_ref, *, slopes):
    i = pl.program_id(1)
    w = WINDOW
    pairs_per_group = C_HEADS // C_KV_HEADS // 2
    kv = jnp.concatenate([kvp_ref[...], kvc_ref[...]], axis=0).astype(F32)
    lane = lax.broadcasted_iota(jnp.int32, (2 * w, LANES), 1)
    low = lane < C_DIM

    def halves(t):
        g0_lo = jnp.where(low, t, 0.0)
        g1_hi = jnp.where(low, 0.0, t)
        return ((g0_lo, pltpu.roll(g0_lo, C_DIM, 1)), (pltpu.roll(g1_hi, C_DIM, 1), g1_hi))

    k_half = halves(kv[:, :LANES])
    v_half = halves(kv[:, LANES:])
    key = lax.broadcasted_iota(jnp.int32, (2 * w, w), 0)
    qry = lax.broadcasted_iota(jnp.int32, (2 * w, w), 1)
    dist = qry + w - key
    first_key = jnp.where(i > 0, 0, w)
    valid = (dist >= 0) & (dist < w) & (key >= first_key)
    dist_m = jnp.where(valid, dist.astype(F32), -NEG)

    q_rows = [jnp.concatenate([q_ref[:, (g * pairs_per_group + jj) * LANES:(g * pairs_per_group + jj + 1) * LANES]
                               for jj in range(pairs_per_group)], axis=0) for g in range(C_KV_HEADS)]
    logits = [[_dot_nt(k_half[g][half].astype(BF16), q_rows[g]) for half in range(2)] for g in range(C_KV_HEADS)]
    for g in range(C_KV_HEADS):
        out_t = None
        for half in range(2):
            probs, inv = [], []
            for jj in range(pairs_per_group):
                h = 2 * (g * pairs_per_group + jj) + half
                s = logits[g][half][:, jj * w:(jj + 1) * w] - (slopes[h] * LOG2E) * dist_m
                sink = jnp.full((1, w), sink_ref[h], F32) * LOG2E
                m = jnp.maximum(_col_reduce(s, jnp.maximum, jnp.max), sink)
                p = jnp.exp2(s - m)
                inv.append(1.0 / (_col_reduce(p, jnp.add, jnp.sum) + jnp.exp2(sink - m)))
                probs.append(p.astype(BF16))
            vt = v_half[g][half].T.astype(BF16)
            contrib = _dot(vt, jnp.concatenate(probs, axis=1)) * jnp.concatenate(inv, axis=1)
            out_t = contrib if out_t is None else out_t + contrib
        for jj in range(pairs_per_group):
            cols = slice((g * pairs_per_group + jj) * LANES, (g * pairs_per_group + jj + 1) * LANES)
            z = z_ref[:, cols].astype(F32)
            o_ref[:, cols] = (out_t[:, jj * w:(jj + 1) * w].T * jax.nn.silu(z)).astype(o_ref.dtype)


def _sliding_window(p_arr, sinks, batch, seq):
    w = WINDOW
    nb = seq // w
    kv_blk = P_OFF['c_k'] // (2 * LANES)
    return pl.pallas_call(
        functools.partial(_swa_kernel, slopes=_alibi_slopes(C_HEADS)),
        grid=(batch, nb),
        in_specs=[pl.BlockSpec((w, BRANCH_WIDTH), lambda b, i: (b * nb + i, P_OFF['c_q'] // BRANCH_WIDTH)),
                  pl.BlockSpec((w, 2 * LANES), lambda b, i: (b * nb + i, kv_blk)),
                  pl.BlockSpec((w, 2 * LANES), lambda b, i: (b * nb + jnp.maximum(i - 1, 0), kv_blk)),
                  pl.BlockSpec((w, BRANCH_WIDTH), lambda b, i: (b * nb + i, P_OFF['c_z'] // BRANCH_WIDTH)),
                  pl.BlockSpec(memory_space=pltpu.SMEM)],
        out_specs=pl.BlockSpec((w, BRANCH_WIDTH), lambda b, i: (b * nb + i, 0)),
        out_shape=jax.ShapeDtypeStruct((batch * seq, BRANCH_WIDTH), BF16),
        compiler_params=_params(("parallel", "parallel")),
        name="sliding_window",
    )(p_arr, p_arr, p_arr, p_arr, sinks.astype(F32))


def _latent_prep_kernel(cq_ref, ckv_ref, kr_ref, tab_ref, qg_ref, kvg_ref, qup_ref, kup_ref, vupt_ref,
                        q_ref, k_ref, vt_ref):
    def rms(x, g):
        return x * lax.rsqrt(jnp.mean(x * x, axis=-1, keepdims=True) + RMS_EPS) * g

    tab = tab_ref[...]
    lane = lax.broadcasted_iota(jnp.int32, tab.shape, 1)

    def rotate(t):
        r = t * tab
        return r + pltpu.roll(r, D_ROPE, 1)

    q = _dot(rms(cq_ref[...].astype(F32), qg_ref[...]).astype(BF16), qup_ref[...])
    ckv = rms(ckv_ref[...].astype(F32), kvg_ref[...]).astype(BF16)
    k_nope = _dot(ckv, kup_ref[...])
    vt_ref[...] = _dot_nt(vupt_ref[...], ckv).astype(vt_ref.dtype)
    k_rot = jnp.where(lane < D_ROPE, rotate(kr_ref[...].astype(F32)), 0.0).astype(k_ref.dtype)
    hw = 2 * LANES
    for h in range(D_HEADS):
        q_ref[:, h * hw:h * hw + LANES] = q[:, h * hw:h * hw + LANES].astype(q_ref.dtype)
        q_ref[:, h * hw + LANES:(h + 1) * hw] = rotate(q[:, h * hw + LANES:(h + 1) * hw]).astype(q_ref.dtype)
        k_ref[:, h * hw:h * hw + LANES] = k_nope[:, h * LANES:(h + 1) * LANES].astype(k_ref.dtype)
        k_ref[:, h * hw + LANES:(h + 1) * hw] = k_rot


def _latent_prep(p_arr, tab, q_gain, kv_gain, q_up, k_up, v_up_t, tm=512):
    t = p_arr.shape[0]
    tm = min(tm, t)
    hw = 2 * LANES

    def const(shape):
        return pl.BlockSpec(shape, lambda i: (0, 0))

    return pl.pallas_call(
        _latent_prep_kernel,
        grid=(t // tm,),
        in_specs=[pl.BlockSpec((tm, D_Q_LORA), lambda i: (i, P_OFF['d_cq'] // D_Q_LORA)),
                  pl.BlockSpec((tm, D_KV_LORA), lambda i: (i, P_OFF['d_ckv'] // D_KV_LORA)),
                  pl.BlockSpec((tm, LANES), lambda i: (i, P_OFF['d_kr'] // LANES)),
                  pl.BlockSpec((tm, LANES), lambda i: (i, 0)),
                  const((1, D_Q_LORA)), const((1, D_KV_LORA)),
                  const((D_Q_LORA, D_HEADS * hw)), const((D_KV_LORA, D_HEADS * LANES)),
                  const((D_HEADS * D_VDIM, D_KV_LORA))],
        out_specs=[pl.BlockSpec((tm, D_HEADS * hw), lambda i: (i, 0)),
                   pl.BlockSpec((tm, D_HEADS * hw), lambda i: (i, 0)),
                   pl.BlockSpec((D_HEADS * D_VDIM, tm), lambda i: (0, i))],
        out_shape=[jax.ShapeDtypeStruct((t, D_HEADS * hw), BF16),
                   jax.ShapeDtypeStruct((t, D_HEADS * hw), BF16),
                   jax.ShapeDtypeStruct((D_HEADS * D_VDIM, t), BF16)],
        compiler_params=_params(("parallel",)),
        name="latent_prep",
    )(p_arr, p_arr, p_arr, tab, q_gain.reshape(1, -1), kv_gain.reshape(1, -1), q_up, k_up, v_up_t)


def _sparse_kernel(q_ref, iq_ref, z_ref, ik_ref, k_ref, vt_ref, wt_ref, o_ref,
                   keys_ref, jcut_ref, m_ref, l_ref, acc_ref, *, topk, ck, slopes, idx_bits):
    i = pl.program_id(1)
    tq = q_ref.shape[0]
    nch = ((i + 1) * tq + ck - 1) // ck
    slab = min(2 * LANES, ck)
    q_pos = i * tq + lax.broadcasted_iota(jnp.int32, (ck, tq), 1)
    s_iota = lax.broadcasted_iota(jnp.int32, (ck, tq), 0)
    lane = lax.broadcasted_iota(jnp.int32, (ck, LANES), 1)
    low = lane < IDX_DIM
    w_all = wt_ref[0] * (IDX_HEADS ** -0.5 * IDX_DIM ** -0.5)

    def score_chunk(c, carry):
        start = pl.multiple_of(c * ck, ck)
        ikc = ik_ref[pl.ds(start, ck), :]
        ik_lo = jnp.where(low, ikc, jnp.zeros_like(ikc))
        ik_hi = jnp.where(low, jnp.zeros_like(ikc), ikc)
        acc = jnp.zeros((ck, tq), F32)
        for pair in range(IDX_HEADS // 2):
            iqp = iq_ref[:, pair * LANES:(pair + 1) * LANES]
            acc = acc + w_all[2 * pair:2 * pair + 1, :] * jnp.maximum(_dot_nt(ik_lo, iqp), 0.0)
            acc = acc + w_all[2 * pair + 1:2 * pair + 2, :] * jnp.maximum(_dot_nt(ik_hi, iqp), 0.0)
        acc = jnp.where(acc == 0.0, 0.0, acc)
        bits = lax.bitcast_convert_type(acc, jnp.int32)
        key = bits ^ ((bits >> 31) & 0x7FFFFFFF)
        key = jnp.where(start + s_iota <= q_pos, key, INT_MIN)
        keys_ref[pl.ds(start, ck), :] = key
        return carry

    lax.fori_loop(0, nch, score_chunk, 0)

    def count(pred_fn):
        def body(c, cnt):
            start = pl.multiple_of(c * ck, ck)
            hit = pred_fn(keys_ref[pl.ds(start, ck), :], start + s_iota)
            ways = 8
            accs = [cnt] + [jnp.zeros((SUBLANES, tq), F32)] * (ways - 1)
            for a, r in enumerate(range(0, ck, SUBLANES)):
                accs[a % ways] = jnp.where(hit[r:r + SUBLANES, :], accs[a % ways] + 1.0, accs[a % ways])
            return _tree(accs, jnp.add)
        part = lax.fori_loop(0, nch, body, jnp.zeros((SUBLANES, tq), F32))
        return jnp.sum(part, axis=0, keepdims=True)

    kf = float(topk)

    def thr_step(it, state):
        thr, n_ge = state
        cand = thr ^ (jnp.int32(1) << (31 - it))
        cnt = count(lambda kc, pos: kc >= cand)
        ok = cnt >= kf
        return jnp.where(ok, cand, thr), jnp.where(ok, cnt, n_ge)

    total = jnp.full((1, tq), 1.0, F32) * (nch * ck).astype(F32)
    thr, n_ge = lax.fori_loop(0, 32, thr_step, (jnp.full((1, tq), INT_MIN, jnp.int32), total))
    tied = (n_ge > kf) & (thr != INT_MIN)
    jcut_ref[...] = jnp.full(jcut_ref.shape, 2 ** 30, jnp.int32)

    @pl.when(jnp.max(tied.astype(F32)) > 0.0)
    def _():
        need = kf - count(lambda kc, pos: kc > thr)

        def cut_step(it, x):
            cand = x + (jnp.int32(1) << (idx_bits - 1 - it))
            cnt = count(lambda kc, pos: (kc == thr) & (pos < cand))
            return jnp.where(cnt < need, cand, x)
        x = lax.fori_loop(0, idx_bits, cut_step, jnp.zeros((1, tq), jnp.int32))
        jcut_ref[...] = jnp.broadcast_to(jnp.where(tied, x, 2 ** 30), jcut_ref.shape)

    jcut = jcut_ref[0:1, :]

    m_ref[...] = jnp.full(m_ref.shape, NEG, F32)
    l_ref[...] = jnp.zeros(l_ref.shape, F32)
    acc_ref[...] = jnp.zeros(acc_ref.shape, F32)
    rep = A_HEADS // A_KV_HEADS
    q_groups = [jnp.concatenate([q_ref[:, (g * rep + r) * A_DIM:(g * rep + r + 1) * A_DIM] for r in range(rep)],
                                axis=0) for g in range(A_KV_HEADS)]

    def attend_chunk(c, carry):
        start = pl.multiple_of(c * ck, ck)
        kc = keys_ref[pl.ds(start, ck), :]
        pos = start + s_iota
        sel = ((kc > thr) | ((kc == thr) & (pos <= jcut))) & (pos <= q_pos)
        dist = jnp.where(sel, (q_pos - pos).astype(F32), FAR)
        logits_g = [_dot_nt(k_ref[pl.ds(start, ck), g * A_DIM:(g + 1) * A_DIM], q_groups[g])
                    for g in range(A_KV_HEADS)]
        for g in range(A_KV_HEADS):
            vtg = vt_ref[0, c, g * A_DIM:(g + 1) * A_DIM, :]
            for r in range(rep):
                h = g * rep + r
                s = logits_g[g][:, r * tq:(r + 1) * tq] - (slopes[h] * LOG2E) * dist
                m_prev = m_ref[h:h + 1, :]
                m_new = jnp.maximum(m_prev, _col_reduce(s, jnp.maximum, jnp.max))
                alpha = jnp.exp2(m_prev - m_new)
                psum, pv = None, None
                for k0 in range(0, ck, slab):
                    p = jnp.exp2(s[k0:k0 + slab, :] - m_new)
                    part = _tree([p[r0:r0 + SUBLANES, :] for r0 in range(0, slab, SUBLANES)], jnp.add)
                    psum = part if psum is None else psum + part
                    contrib = _dot(vtg[:, k0:k0 + slab], p.astype(BF16))
                    pv = contrib if pv is None else pv + contrib
                l_ref[h:h + 1, :] = alpha * l_ref[h:h + 1, :] + jnp.sum(psum, axis=0, keepdims=True)
                acc_ref[h * A_DIM:(h + 1) * A_DIM, :] = alpha * acc_ref[h * A_DIM:(h + 1) * A_DIM, :] + pv
                m_ref[h:h + 1, :] = m_new
        return carry

    lax.fori_loop(0, nch, attend_chunk, 0)

    for h in range(A_HEADS):
        out_t = acc_ref[h * A_DIM:(h + 1) * A_DIM, :] / l_ref[h:h + 1, :]
        z = z_ref[:, h * A_DIM:(h + 1) * A_DIM].astype(F32)
        o_ref[:, h * A_DIM:(h + 1) * A_DIM] = (out_t.T * jax.nn.silu(z)).astype(o_ref.dtype)


def _key_chunk(seq):
    return min(512, seq)


def _sparse_attention(p_arr, vt, wt, batch, seq):
    tq = min(2 * LANES, seq)
    nq = seq // tq
    ck = _key_chunk(seq)
    topk = min(IDX_TOPK_MAX, seq // 4)
    idx_bits = int(seq).bit_length()
    slopes = _alibi_slopes(A_HEADS)
    assert seq % ck == 0 and ck % tq == 0 and min(slopes) * LOG2E * FAR > -NEG
    return pl.pallas_call(
        functools.partial(_sparse_kernel, topk=topk, ck=ck, slopes=slopes, idx_bits=idx_bits),
        grid=(batch, nq),
        in_specs=[pl.BlockSpec((tq, BRANCH_WIDTH), lambda b, i: (b * nq + i, P_OFF['a_q'] // BRANCH_WIDTH)),
                  pl.BlockSpec((tq, BRANCH_WIDTH), lambda b, i: (b * nq + i, P_OFF['a_iq'] // BRANCH_WIDTH)),
                  pl.BlockSpec((tq, BRANCH_WIDTH), lambda b, i: (b * nq + i, P_OFF['a_z'] // BRANCH_WIDTH)),
                  pl.BlockSpec((seq, LANES), lambda b, i: (b, P_OFF['a_ik'] // LANES)),
                  pl.BlockSpec((seq, 2 * LANES), lambda b, i: (b, P_OFF['a_k'] // (2 * LANES))),
                  pl.BlockSpec((1, seq // ck, 2 * LANES, ck), lambda b, i: (b, 0, 0, 0)),
                  pl.BlockSpec((1, IDX_HEADS, tq), lambda b, i: (b, 0, i))],
        out_specs=pl.BlockSpec((tq, BRANCH_WIDTH), lambda b, i: (b * nq + i, 0)),
        out_shape=jax.ShapeDtypeStruct((batch * seq, BRANCH_WIDTH), BF16),
        scratch_shapes=[pltpu.VMEM((seq, tq), jnp.int32), pltpu.VMEM((8, tq), jnp.int32),
                        pltpu.VMEM((A_HEADS, tq), F32), pltpu.VMEM((A_HEADS, tq), F32),
                        pltpu.VMEM((A_HEADS * A_DIM, tq), F32)],
        compiler_params=_params(("parallel", "arbitrary")),
        name="sparse_attention",
    )(p_arr, p_arr, p_arr, p_arr, p_arr, vt, wt)


def _swap_halves(w):
    half = w.shape[-1] // 2
    return jnp.concatenate([-w[..., half:], w[..., :half]], axis=-1)


def _prepare_in_proj(w_in):
    sizes = [n for _, n in IN_SEGMENTS]
    offs = np.concatenate([[0], np.cumsum(sizes)])
    span = {name: (int(offs[k]), int(offs[k + 1])) for k, (name, _) in enumerate(IN_SEGMENTS)}
    wt32 = jnp.swapaxes(w_in, 1, 2)
    wt16 = wt32.astype(BF16)
    seg16 = lambda name: wt16[:, span[name][0]:span[name][1], :]
    seg32 = lambda name: wt32[:, span[name][0]:span[name][1], :]
    rows = []
    for name, _ in P_LAYOUT:
        if name == 'a_ik':
            rows += [seg16('a_ik'), seg16('a_ik')]
        elif name == 'd_kr':
            kr = seg16('d_kr')
            rows += [kr, -kr[:, D_ROPE // 2:, :], kr[:, :D_ROPE // 2, :]]
        elif name == 'a_q':
            rows.append((seg32('a_q') * (A_DIM ** -0.5 * LOG2E)).astype(BF16))
        elif name == 'b_q':
            rows.append((seg32('b_q') * (B_DIM ** -0.5 * LOG2E)).astype(BF16))
        elif name == 'c_q':
            rows.append((seg32('c_q') * (C_DIM ** -0.5 * LOG2E)).astype(BF16))
        else:
            rows.append(seg16(name))
    w_main_t = jnp.concatenate(rows, axis=1)
    pad = jnp.zeros((w_in.shape[0], LANES - IDX_HEADS - B_HEADS, w_in.shape[1]), BF16)
    w_small_t = jnp.concatenate([seg16('a_iw'), seg16('b_f'), pad], axis=1)
    return w_main_t, w_small_t


def _prepare_latent(dq_up, dkv_up):
    q = dq_up.reshape(D_Q_LORA, D_HEADS, D_NOPE + D_ROPE) * ((D_NOPE + D_ROPE) ** -0.5 * LOG2E)
    rope = q[..., D_NOPE:]
    q_up = jnp.concatenate([q[..., :D_NOPE], rope, _swap_halves(rope)], axis=-1)
    kv = dkv_up.reshape(D_KV_LORA, D_HEADS, D_NOPE + D_VDIM)
    return (q_up.reshape(D_Q_LORA, -1).astype(BF16),
            kv[..., :D_NOPE].reshape(D_KV_LORA, -1).astype(BF16),
            kv[..., D_NOPE:].reshape(D_KV_LORA, -1).T.astype(BF16))


def _rope_table(positions):
    half = D_ROPE // 2
    inv_freq = ROPE_THETA ** (-jnp.arange(half, dtype=F32) / half)
    ang = positions.astype(F32)[..., None] * inv_freq
    cos, sin = jnp.cos(ang), jnp.sin(ang)
    tab = jnp.concatenate([cos, cos, sin, sin], axis=-1)
    return tab.reshape(-1, 4 * half)


def _layer(layer, x, xb, tab, batch, seq, w_main, w_small, w_gate, w_branch, w_out, dq_gain, dq_up, dkv_gain,
           dkv_up, f_bias, sinks, ln_gain, ln_bias, alpha, last):
    proj, small = _input_projection(xb, w_main, w_small, layer)

    ck = _key_chunk(seq)
    wt = small[:, :IDX_HEADS].reshape(batch, seq, IDX_HEADS).transpose(0, 2, 1)
    a_v = proj[:, P_OFF['a_v']:P_OFF['a_v'] + 2 * LANES]
    vt = a_v.reshape(batch, seq // ck, ck, 2 * LANES).transpose(0, 1, 3, 2)
    o_a = _sparse_attention(proj, vt, wt, batch, seq)

    f_rows = small[:, IDX_HEADS:IDX_HEADS + B_HEADS].reshape(batch, seq, B_HEADS).transpose(0, 2, 1)
    cum = _forget_cumsum(f_rows.reshape(batch * B_HEADS, seq),
                         jnp.tile(f_bias.astype(F32), batch).reshape(batch * B_HEADS, 1))
    cum = cum.reshape(batch, B_HEADS, seq).transpose(0, 2, 1).reshape(batch * seq, B_HEADS)
    cum = jnp.pad(cum, ((0, 0), (0, LANES - B_HEADS)))
    vt_b = proj[:, P_OFF['b_v']:P_OFF['b_v'] + BRANCH_WIDTH].T
    o_b = _flash_attention(proj, P_OFF['b_q'] // BRANCH_WIDTH, proj, P_OFF['b_k'] // BRANCH_WIDTH, vt_b,
                           proj, P_OFF['b_z'] // BRANCH_WIDTH, batch=batch, seq=seq, heads=B_HEADS,
                           dqk=B_DIM, dv=B_DIM, cum=cum)

    o_c = _sliding_window(proj, sinks, batch, seq)

    q_up, k_up, v_up_t = _prepare_latent(dq_up, dkv_up)
    q_d, k_d, vt_d = _latent_prep(proj, tab, dq_gain, dkv_gain, q_up, k_up, v_up_t)
    o_d = _flash_attention(q_d, 0, k_d, 0, vt_d, proj, P_OFF['d_z'] // BRANCH_WIDTH, batch=batch, seq=seq,
                           heads=D_HEADS, dqk=2 * LANES, dv=D_VDIM)

    merged = _gated_merge(xb, w_gate, (o_a, o_b, o_c, o_d), w_branch, layer)
    y = _matmul(merged, w_out, layer, F32)
    return _residual_layernorm(x, y, ln_gain, ln_bias, alpha, with_bf16=not last)


def kernel(x, positions, w_in, w_gate, w_branch, w_out, dq_gain, dq_up, dkv_gain, dkv_up, f_bias, sinks,
           ln_gain, ln_bias):
    batch, seq, d = x.shape
    depth = w_in.shape[0]
    alpha = (2 * depth) ** 0.25
    tab = _rope_table(positions)
    xf = x.reshape(batch * seq, d)
    xb = _cast_bf16(xf)
    w_main, w_small = _prepare_in_proj(w_in)
    w_gate, w_branch, w_out = w_gate.astype(BF16), w_branch.astype(BF16), w_out.astype(BF16)
    for l in range(depth):
        xf, xb = _layer(l, xf, xb, tab, batch, seq, w_main, w_small, w_gate, w_branch, w_out, dq_gain[l],
                        dq_up[l], dkv_gain[l], dkv_up[l], f_bias[l], sinks[l], ln_gain[l], ln_bias[l], alpha,
                        last=(l == depth - 1))
    return xf.reshape(batch, seq, d)
```

```python
import functools

import numpy as np
import jax
import jax.numpy as jnp
from jax import lax
from jax.experimental import pallas as pl
from jax.experimental.pallas import tpu as pltpu

BRANCH_WIDTH = 1024
A_HEADS, A_KV_HEADS, A_DIM = 8, 2, 128
IDX_HEADS, IDX_DIM, IDX_TOPK_MAX = 16, 64, 256
B_HEADS, B_DIM = 8, 128
C_HEADS, C_KV_HEADS, C_DIM, WINDOW = 16, 2, 64, 128
D_HEADS, D_Q_LORA, D_KV_LORA, D_NOPE, D_ROPE, D_VDIM = 8, 768, 256, 128, 64, 128
ROPE_THETA = 10000.0
RMS_EPS = 1e-6
LN_EPS = 1e-5

IN_SEGMENTS = (
    ('a_q', 1024), ('a_k', 256), ('a_v', 256), ('a_iq', 1024), ('a_ik', 64), ('a_iw', 16), ('a_z', 1024),
    ('b_q', 1024), ('b_k', 1024), ('b_v', 1024), ('b_f', 8), ('b_z', 1024),
    ('c_q', 1024), ('c_k', 128), ('c_v', 128), ('c_z', 1024),
    ('d_cq', 768), ('d_ckv', 256), ('d_kr', 64), ('d_z', 1024),
)

P_LAYOUT = (
    ('d_cq', 768), ('d_ckv', 256), ('a_q', 1024), ('a_iq', 1024), ('a_z', 1024),
    ('b_q', 1024), ('b_k', 1024), ('b_v', 1024), ('b_z', 1024), ('c_q', 1024), ('c_z', 1024),
    ('d_z', 1024), ('a_k', 256), ('a_v', 256), ('a_ik', 128), ('d_kr', 128), ('c_k', 128), ('c_v', 128),
)
P_OFF = {}
_o = 0
for _n, _w in P_LAYOUT:
    assert _o % _w == 0
    P_OFF[_n] = _o
    _o += _w
P_WIDTH = _o

LANES = 128
SUBLANES = 8
NEG = -1e30
FAR = 1e34
LOG2E = 1.4426950408889634
INT_MIN = -2 ** 31
VMEM_LIMIT = 56 * 1024 * 1024

F32 = jnp.float32
BF16 = jnp.bfloat16


def _params(sem, vmem=VMEM_LIMIT):
    return pltpu.CompilerParams(dimension_semantics=sem, vmem_limit_bytes=vmem)


def _dot(a, b):
    return jnp.dot(a, b, preferred_element_type=F32)


def _dot_nt(a, b):
    return lax.dot_general(a, b, (((1,), (1,)), ((), ())), preferred_element_type=F32)


def _alibi_slopes(n_heads):
    return [float(np.float32(2.0 ** (-8.0 * (h + 1) / n_heads))) for h in range(n_heads)]


def _mm_kernel(x_ref, w_ref, o_ref):
    o_ref[...] = _dot(x_ref[...], w_ref[0]).astype(o_ref.dtype)


def _mm_nt_kernel(x_ref, wt_ref, o_ref):
    o_ref[...] = _dot_nt(x_ref[...], wt_ref[0]).astype(o_ref.dtype)


def _matmul(x, w, layer, out_dtype, transposed=False, tm=1024, tn=1024):
    m, k = x.shape
    n = w.shape[1] if transposed else w.shape[2]
    tm, tn = min(tm, m), min(tn, n)
    assert m % tm == 0 and n % tn == 0, (m, n, tm, tn)
    if transposed:
        body, w_spec = _mm_nt_kernel, pl.BlockSpec((1, tn, k), lambda i, j: (layer, j, 0))
    else:
        body, w_spec = _mm_kernel, pl.BlockSpec((1, k, tn), lambda i, j: (layer, 0, j))
    return pl.pallas_call(
        body,
        grid=(m // tm, n // tn),
        in_specs=[pl.BlockSpec((tm, k), lambda i, j: (i, 0)), w_spec],
        out_specs=pl.BlockSpec((tm, tn), lambda i, j: (i, j)),
        out_shape=jax.ShapeDtypeStruct((m, n), out_dtype),
        compiler_params=_params(("parallel", "parallel")),
        name="dense_matmul",
    )(x, w)


def _cast_kernel(x_ref, o_ref):
    o_ref[...] = x_ref[...].astype(o_ref.dtype)


def _cast_bf16(x, tm=512):
    t, d = x.shape
    tm = min(tm, t)
    row = pl.BlockSpec((tm, d), lambda i: (i, 0))
    return pl.pallas_call(
        _cast_kernel, grid=(t // tm,), in_specs=[row], out_specs=row,
        out_shape=jax.ShapeDtypeStruct((t, d), BF16),
        compiler_params=_params(("parallel",)),
        name="cast_bf16",
    )(x)


def _merge_kernel(x_ref, wg_ref, b0_ref, b1_ref, b2_ref, b3_ref, wb_ref, o_ref, acc_ref, *, sub):
    n = pl.program_id(2)

    @pl.when((pl.program_id(0) == 0) & (pl.program_id(1) == 0) & (n == 0))
    def _():
        acc_ref[...] = jnp.zeros(acc_ref.shape, F32)

    branch = jnp.where(n == 0, b0_ref[...], jnp.where(n == 1, b1_ref[...],
                                                      jnp.where(n == 2, b2_ref[...], b3_ref[...])))
    x = x_ref[...]
    for c in range(o_ref.shape[1] // sub):
        cols = slice(c * sub, (c + 1) * sub)
        gate = jax.nn.sigmoid(_dot(x, wg_ref[0, 0, :, cols]))
        contrib = gate * _dot(branch, wb_ref[0, 0, :, cols])
        acc = jnp.where(n == 0, 0.0, acc_ref[:, cols]) + contrib
        acc_ref[:, cols] = acc
        o_ref[:, cols] = acc.astype(o_ref.dtype)


def _gated_merge(xb, wg, branches, wb, layer, tm=512, tn=1024, sub=512):
    t, d = xb.shape
    tm, tn = min(tm, t), min(tn, d)
    sub = min(sub, tn)
    bw = branches[0].shape[1]
    bspec = pl.BlockSpec((tm, bw), lambda i, j, n: (i, 0))
    return pl.pallas_call(
        functools.partial(_merge_kernel, sub=sub),
        grid=(t // tm, d // tn, 4),
        in_specs=[pl.BlockSpec((tm, d), lambda i, j, n: (i, 0)),
                  pl.BlockSpec((1, 1, d, tn), lambda i, j, n: (layer, n, 0, j)),
                  bspec, bspec, bspec, bspec,
                  pl.BlockSpec((1, 1, bw, tn), lambda i, j, n: (layer, n, 0, j))],
        out_specs=pl.BlockSpec((tm, tn), lambda i, j, n: (i, j)),
        out_shape=jax.ShapeDtypeStruct((t, d), BF16),
        scratch_shapes=[pltpu.VMEM((tm, tn), F32)],
        compiler_params=_params(("arbitrary", "arbitrary", "arbitrary")),
        name="gated_merge",
    )(xb, wg, *branches, wb)


def _ln_kernel(x_ref, y_ref, g_ref, b_ref, o_ref, *maybe_ob_ref, alpha):
    r = alpha * x_ref[...] + y_ref[...]
    mu = jnp.mean(r, axis=-1, keepdims=True)
    c = r - mu
    var = jnp.mean(c * c, axis=-1, keepdims=True)
    out = c * lax.rsqrt(var + LN_EPS) * g_ref[...] + b_ref[...]
    o_ref[...] = out
    for ob_ref in maybe_ob_ref:
        ob_ref[...] = out.astype(BF16)


def _residual_layernorm(x, y, gain, bias, alpha, with_bf16, tm=256):
    t, d = x.shape
    tm = min(tm, t)
    row = pl.BlockSpec((tm, d), lambda i: (i, 0))
    vec = pl.BlockSpec((1, d), lambda i: (0, 0))
    out_shape = [jax.ShapeDtypeStruct((t, d), F32)] + ([jax.ShapeDtypeStruct((t, d), BF16)] if with_bf16 else [])
    outs = pl.pallas_call(
        functools.partial(_ln_kernel, alpha=alpha),
        grid=(t // tm,),
        in_specs=[row, row, vec, vec],
        out_specs=[row] * len(out_shape),
        out_shape=out_shape,
        compiler_params=_params(("parallel",)),
        name="residual_layernorm",
    )(x, y, gain.reshape(1, d), bias.reshape(1, d))
    return (outs[0], outs[1]) if with_bf16 else (outs[0], None)


def _forget_cumsum_kernel(f_ref, bias_ref, c_ref):
    rows, s = f_ref.shape
    lane = lax.broadcasted_iota(jnp.int32, (rows, LANES), 1)
    carry = jnp.zeros((rows, 1), F32)
    for c in range(s // LANES):
        x = jax.nn.log_sigmoid(f_ref[:, c * LANES:(c + 1) * LANES] + bias_ref[...])
        shift = 1
        while shift < LANES:
            x = x + jnp.where(lane >= shift, pltpu.roll(x, shift, 1), 0.0)
            shift *= 2
        x = x + carry
        c_ref[:, c * LANES:(c + 1) * LANES] = x * LOG2E
        carry = x[:, LANES - 1:LANES]


def _forget_cumsum(f_rows, bias_rows):
    rows, s = f_rows.shape
    full = pl.BlockSpec((rows, s), lambda: (0, 0))
    return pl.pallas_call(
        _forget_cumsum_kernel,
        in_specs=[full, pl.BlockSpec((rows, 1), lambda: (0, 0))],
        out_specs=full,
        out_shape=jax.ShapeDtypeStruct((rows, s), F32),
        name="forget_cumsum",
    )(f_rows, bias_rows)


def _tree(parts, op, ways=8):
    parts = list(parts)
    if len(parts) > ways:
        accs = parts[:ways]
        for a in range(ways, len(parts)):
            accs[a % ways] = op(accs[a % ways], parts[a])
        parts = accs
    while len(parts) > 1:
        nxt = [op(parts[a], parts[a + 1]) for a in range(0, len(parts) - 1, 2)]
        if len(parts) % 2:
            nxt.append(parts[-1])
        parts = nxt
    return parts[0]


def _col_reduce(x, op, reduce_fn):
    rows = x.shape[0]
    part = _tree([x[r:r + SUBLANES, :] for r in range(0, rows, SUBLANES)], op)
    return reduce_fn(part, axis=0, keepdims=True)


def _flash_kernel(i_tab, j_tab, *refs, fox, heads, dqk, dv, tile, slab):
    if fox:
        q_ref, k_ref, vt_ref, ck_ref, z_ref, o_ref, m_sc, l_sc, acc_sc = refs
    else:
        q_ref, k_ref, vt_ref, z_ref, o_ref, m_sc, l_sc, acc_sc = refs
    pair = pl.program_id(1)
    i = i_tab[pair]
    j = j_tab[pair]

    @pl.when(j == 0)
    def _():
        m_sc[...] = jnp.full(m_sc.shape, NEG, F32)
        l_sc[...] = jnp.zeros(l_sc.shape, F32)
        acc_sc[...] = jnp.zeros(acc_sc.shape, F32)

    def step(masked):
        if masked:
            causal = (lax.broadcasted_iota(jnp.int32, (tile, tile), 0)
                      <= lax.broadcasted_iota(jnp.int32, (tile, tile), 1))
        def logits(h):
            return _dot_nt(k_ref[:, h * dqk:(h + 1) * dqk], q_ref[:, h * dqk:(h + 1) * dqk])

        s_next = logits(0)
        for h in range(heads):
            s = s_next
            if h + 1 < heads:
                s_next = logits(h + 1)
            if fox:
                s = s - ck_ref[:, h:h + 1]
            if masked:
                s = jnp.where(causal, s, NEG)
            m_prev = m_sc[h:h + 1, :]
            m_new = jnp.maximum(m_prev, _col_reduce(s, jnp.maximum, jnp.max))
            alpha = jnp.exp2(m_prev - m_new)
            psum, pv = None, None
            for k0 in range(0, tile, slab):
                p = jnp.exp2(s[k0:k0 + slab, :] - m_new)
                part = _tree([p[r:r + SUBLANES, :] for r in range(0, slab, SUBLANES)], jnp.add)
                psum = part if psum is None else psum + part
                contrib = _dot(vt_ref[h * dv:(h + 1) * dv, k0:k0 + slab], p.astype(BF16))
                pv = contrib if pv is None else pv + contrib
            l_sc[h:h + 1, :] = alpha * l_sc[h:h + 1, :] + jnp.sum(psum, axis=0, keepdims=True)
            acc_sc[h * dv:(h + 1) * dv, :] = alpha * acc_sc[h * dv:(h + 1) * dv, :] + pv
            m_sc[h:h + 1, :] = m_new

    @pl.when(j < i)
    def _():
        step(False)

    @pl.when(j == i)
    def _():
        step(True)
        for h in range(heads):
            out_t = acc_sc[h * dv:(h + 1) * dv, :] / l_sc[h:h + 1, :]
            z = z_ref[:, h * dv:(h + 1) * dv].astype(F32)
            o_ref[:, h * dv:(h + 1) * dv] = (out_t.T * jax.nn.silu(z)).astype(o_ref.dtype)


def _flash_attention(q_arr, q_blk, k_arr, k_blk, vt_arr, z_arr, z_blk, *, batch, seq, heads,
                     dqk, dv, cum=None, tile=512):
    tile = min(tile, seq)
    assert seq % tile == 0, (seq, tile)
    nq = seq // tile
    fox = cum is not None
    pairs = [(i, j) for i in range(nq) for j in range(i + 1)]
    i_tab = jnp.asarray([p[0] for p in pairs], jnp.int32)
    j_tab = jnp.asarray([p[1] for p in pairs], jnp.int32)
    in_specs = [
        pl.BlockSpec((tile, heads * dqk), lambda b, p, it, jt: (b * nq + it[p], q_blk)),
        pl.BlockSpec((tile, heads * dqk), lambda b, p, it, jt: (b * nq + jt[p], k_blk)),
        pl.BlockSpec((heads * dv, tile), lambda b, p, it, jt: (0, b * nq + jt[p])),
    ]
    args = [q_arr, k_arr, vt_arr]
    if fox:
        in_specs.append(pl.BlockSpec((tile, LANES), lambda b, p, it, jt: (b * nq + jt[p], 0)))
        args.append(cum)
    in_specs.append(pl.BlockSpec((tile, heads * dv), lambda b, p, it, jt: (b * nq + it[p], z_blk)))
    args.append(z_arr)
    return pl.pallas_call(
        functools.partial(_flash_kernel, fox=fox, heads=heads, dqk=dqk, dv=dv, tile=tile,
                          slab=min(2 * LANES, tile)),
        grid_spec=pltpu.PrefetchScalarGridSpec(
            num_scalar_prefetch=2,
            grid=(batch, len(pairs)),
            in_specs=in_specs,
            out_specs=pl.BlockSpec((tile, heads * dv), lambda b, p, it, jt: (b * nq + it[p], 0)),
            scratch_shapes=[pltpu.VMEM((heads, tile), F32), pltpu.VMEM((heads, tile), F32),
                            pltpu.VMEM((heads * dv, tile), F32)]),
        out_shape=jax.ShapeDtypeStruct((batch * seq, heads * dv), BF16),
        compiler_params=_params(("parallel", "arbitrary")),
        name="flash_fox" if fox else "flash_latent",
    )(i_tab, j_tab, *args)


def _swa_kernel(q_ref, kvc_ref, kvp_ref, z_ref, sink_ref, o_ref, *, slopes):
    i = pl.program_id(1)
    w = WINDOW
    pairs_per_group = C_HEADS // C_KV_HEADS // 2
    kv = jnp.concatenate([kvp_ref[...], kvc_ref[...]], axis=0).astype(F32)
    lane = lax.broadcasted_iota(jnp.int32, (2 * w, LANES), 1)
    low = lane < C_DIM

    def halves(t):
        g0_lo = jnp.where(low, t, 0.0)
        g1_hi = jnp.where(low, 0.0, t)
        return ((g0_lo, pltpu.roll(g0_lo, C_DIM, 1)), (pltpu.roll(g1_hi, C_DIM, 1), g1_hi))

    k_half = halves(kv[:, :LANES])
    v_half = halves(kv[:, LANES:])
    key = lax.broadcasted_iota(jnp.int32, (2 * w, w), 0)
    qry = lax.broadcasted_iota(jnp.int32, (2 * w, w), 1)
    dist = qry + w - key
    first_key = jnp.where(i > 0, 0, w)
    valid = (dist >= 0) & (dist < w) & (key >= first_key)
    dist_m = jnp.where(valid, dist.astype(F32), -NEG)

    q_rows = [jnp.concatenate([q_ref[:, (g * pairs_per_group + jj) * LANES:(g * pairs_per_group + jj + 1) * LANES]
                               for jj in range(pairs_per_group)], axis=0) for g in range(C_KV_HEADS)]
    logits = [[_dot_nt(k_half[g][half].astype(BF16), q_rows[g]) for half in range(2)] for g in range(C_KV_HEADS)]
    for g in range(C_KV_HEADS):
        out_t = None
        for half in range(2):
            probs, inv = [], []
            for jj in range(pairs_per_group):
                h = 2 * (g * pairs_per_group + jj) + half
                s = logits[g][half][:, jj * w:(jj + 1) * w] - (slopes[h] * LOG2E) * dist_m
                sink = jnp.full((1, w), sink_ref[h], F32) * LOG2E
                m = jnp.maximum(_col_reduce(s, jnp.maximum, jnp.max), sink)
                p = jnp.exp2(s - m)
                inv.append(1.0 / (_col_reduce(p, jnp.add, jnp.sum) + jnp.exp2(sink - m)))
                probs.append(p.astype(BF16))
            vt = v_half[g][half].T.astype(BF16)
            contrib = _dot(vt, jnp.concatenate(probs, axis=1)) * jnp.concatenate(inv, axis=1)
            out_t = contrib if out_t is None else out_t + contrib
        for jj in range(pairs_per_group):
            cols = slice((g * pairs_per_group + jj) * LANES, (g * pairs_per_group + jj + 1) * LANES)
            z = z_ref[:, cols].astype(F32)
            o_ref[:, cols] = (out_t[:, jj * w:(jj + 1) * w].T * jax.nn.silu(z)).astype(o_ref.dtype)


def _sliding_window(p_arr, sinks, batch, seq):
    w = WINDOW
    nb = seq // w
    kv_blk = P_OFF['c_k'] // (2 * LANES)
    return pl.pallas_call(
        functools.partial(_swa_kernel, slopes=_alibi_slopes(C_HEADS)),
        grid=(batch, nb),
        in_specs=[pl.BlockSpec((w, BRANCH_WIDTH), lambda b, i: (b * nb + i, P_OFF['c_q'] // BRANCH_WIDTH)),
                  pl.BlockSpec((w, 2 * LANES), lambda b, i: (b * nb + i, kv_blk)),
                  pl.BlockSpec((w, 2 * LANES), lambda b, i: (b * nb + jnp.maximum(i - 1, 0), kv_blk)),
                  pl.BlockSpec((w, BRANCH_WIDTH), lambda b, i: (b * nb + i, P_OFF['c_z'] // BRANCH_WIDTH)),
                  pl.BlockSpec(memory_space=pltpu.SMEM)],
        out_specs=pl.BlockSpec((w, BRANCH_WIDTH), lambda b, i: (b * nb + i, 0)),
        out_shape=jax.ShapeDtypeStruct((batch * seq, BRANCH_WIDTH), BF16),
        compiler_params=_params(("parallel", "parallel")),
        name="sliding_window",
    )(p_arr, p_arr, p_arr, p_arr, sinks.astype(F32))


def _latent_prep_kernel(cq_ref, ckv_ref, kr_ref, tab_ref, qg_ref, kvg_ref, qup_ref, kup_ref, vupt_ref,
                        q_ref, k_ref, vt_ref):
    def rms(x, g):
        return x * lax.rsqrt(jnp.mean(x * x, axis=-1, keepdims=True) + RMS_EPS) * g

    tab = tab_ref[...]
    lane = lax.broadcasted_iota(jnp.int32, tab.shape, 1)

    def rotate(t):
        r = t * tab
        return r + pltpu.roll(r, D_ROPE, 1)

    q = _dot(rms(cq_ref[...].astype(F32), qg_ref[...]).astype(BF16), qup_ref[...])
    ckv = rms(ckv_ref[...].astype(F32), kvg_ref[...]).astype(BF16)
    k_nope = _dot(ckv, kup_ref[...])
    vt_ref[...] = _dot_nt(vupt_ref[...], ckv).astype(vt_ref.dtype)
    k_rot = jnp.where(lane < D_ROPE, rotate(kr_ref[...].astype(F32)), 0.0).astype(k_ref.dtype)
    hw = 2 * LANES
    for h in range(D_HEADS):
        q_ref[:, h * hw:h * hw + LANES] = q[:, h * hw:h * hw + LANES].astype(q_ref.dtype)
        q_ref[:, h * hw + LANES:(h + 1) * hw] = rotate(q[:, h * hw + LANES:(h + 1) * hw]).astype(q_ref.dtype)
        k_ref[:, h * hw:h * hw + LANES] = k_nope[:, h * LANES:(h + 1) * LANES].astype(k_ref.dtype)
        k_ref[:, h * hw + LANES:(h + 1) * hw] = k_rot


def _latent_prep(p_arr, tab, q_gain, kv_gain, q_up, k_up, v_up_t, tm=512):
    t = p_arr.shape[0]
    tm = min(tm, t)
    hw = 2 * LANES

    def const(shape):
        return pl.BlockSpec(shape, lambda i: (0, 0))

    return pl.pallas_call(
        _latent_prep_kernel,
        grid=(t // tm,),
        in_specs=[pl.BlockSpec((tm, D_Q_LORA), lambda i: (i, P_OFF['d_cq'] // D_Q_LORA)),
                  pl.BlockSpec((tm, D_KV_LORA), lambda i: (i, P_OFF['d_ckv'] // D_KV_LORA)),
                  pl.BlockSpec((tm, LANES), lambda i: (i, P_OFF['d_kr'] // LANES)),
                  pl.BlockSpec((tm, LANES), lambda i: (i, 0)),
                  const((1, D_Q_LORA)), const((1, D_KV_LORA)),
                  const((D_Q_LORA, D_HEADS * hw)), const((D_KV_LORA, D_HEADS * LANES)),
                  const((D_HEADS * D_VDIM, D_KV_LORA))],
        out_specs=[pl.BlockSpec((tm, D_HEADS * hw), lambda i: (i, 0)),
                   pl.BlockSpec((tm, D_HEADS * hw), lambda i: (i, 0)),
                   pl.BlockSpec((D_HEADS * D_VDIM, tm), lambda i: (0, i))],
        out_shape=[jax.ShapeDtypeStruct((t, D_HEADS * hw), BF16),
                   jax.ShapeDtypeStruct((t, D_HEADS * hw), BF16),
                   jax.ShapeDtypeStruct((D_HEADS * D_VDIM, t), BF16)],
        compiler_params=_params(("parallel",)),
        name="latent_prep",
    )(p_arr, p_arr, p_arr, tab, q_gain.reshape(1, -1), kv_gain.reshape(1, -1), q_up, k_up, v_up_t)


def _sparse_kernel(q_ref, iq_ref, z_ref, ik_ref, k_ref, vt_ref, wt_ref, o_ref,
                   keys_ref, jcut_ref, m_ref, l_ref, acc_ref, *, topk, ck, slopes, idx_bits):
    i = pl.program_id(1)
    tq = q_ref.shape[0]
    nch = ((i + 1) * tq + ck - 1) // ck
    slab = min(2 * LANES, ck)
    q_pos = i * tq + lax.broadcasted_iota(jnp.int32, (ck, tq), 1)
    s_iota = lax.broadcasted_iota(jnp.int32, (ck, tq), 0)
    lane = lax.broadcasted_iota(jnp.int32, (ck, LANES), 1)
    low = lane < IDX_DIM
    w_all = wt_ref[0] * (IDX_HEADS ** -0.5 * IDX_DIM ** -0.5)

    def score_chunk(c, carry):
        start = pl.multiple_of(c * ck, ck)
        ikc = ik_ref[pl.ds(start, ck), :]
        ik_lo = jnp.where(low, ikc, jnp.zeros_like(ikc))
        ik_hi = jnp.where(low, jnp.zeros_like(ikc), ikc)
        acc = jnp.zeros((ck, tq), F32)
        for pair in range(IDX_HEADS // 2):
            iqp = iq_ref[:, pair * LANES:(pair + 1) * LANES]
            acc = acc + w_all[2 * pair:2 * pair + 1, :] * jnp.maximum(_dot_nt(ik_lo, iqp), 0.0)
            acc = acc + w_all[2 * pair + 1:2 * pair + 2, :] * jnp.maximum(_dot_nt(ik_hi, iqp), 0.0)
        acc = jnp.where(acc == 0.0, 0.0, acc)
        bits = lax.bitcast_convert_type(acc, jnp.int32)
        key = bits ^ ((bits >> 31) & 0x7FFFFFFF)
        key = jnp.where(start + s_iota <= q_pos, key, INT_MIN)
        keys_ref[pl.ds(start, ck), :] = key
        return carry

    lax.fori_loop(0, nch, score_chunk, 0)

    def count(pred_fn):
        def body(c, cnt):
            start = pl.multiple_of(c * ck, ck)
            hit = pred_fn(keys_ref[pl.ds(start, ck), :], start + s_iota)
            ways = 8
            accs = [cnt] + [jnp.zeros((SUBLANES, tq), F32)] * (ways - 1)
            for a, r in enumerate(range(0, ck, SUBLANES)):
                accs[a % ways] = jnp.where(hit[r:r + SUBLANES, :], accs[a % ways] + 1.0, accs[a % ways])
            return _tree(accs, jnp.add)
        part = lax.fori_loop(0, nch, body, jnp.zeros((SUBLANES, tq), F32))
        return jnp.sum(part, axis=0, keepdims=True)

    kf = float(topk)

    def thr_step(it, state):
        thr, n_ge = state
        cand = thr ^ (jnp.int32(1) << (31 - it))
        cnt = count(lambda kc, pos: kc >= cand)
        ok = cnt >= kf
        return jnp.where(ok, cand, thr), jnp.where(ok, cnt, n_ge)

    total = jnp.full((1, tq), 1.0, F32) * (nch * ck).astype(F32)
    thr, n_ge = lax.fori_loop(0, 32, thr_step, (jnp.full((1, tq), INT_MIN, jnp.int32), total))
    tied = (n_ge > kf) & (thr != INT_MIN)
    jcut_ref[...] = jnp.full(jcut_ref.shape, 2 ** 30, jnp.int32)

    @pl.when(jnp.max(tied.astype(F32)) > 0.0)
    def _():
        need = kf - count(lambda kc, pos: kc > thr)

        def cut_step(it, x):
            cand = x + (jnp.int32(1) << (idx_bits - 1 - it))
            cnt = count(lambda kc, pos: (kc == thr) & (pos < cand))
            return jnp.where(cnt < need, cand, x)
        x = lax.fori_loop(0, idx_bits, cut_step, jnp.zeros((1, tq), jnp.int32))
        jcut_ref[...] = jnp.broadcast_to(jnp.where(tied, x, 2 ** 30), jcut_ref.shape)

    jcut = jcut_ref[0:1, :]

    m_ref[...] = jnp.full(m_ref.shape, NEG, F32)
    l_ref[...] = jnp.zeros(l_ref.shape, F32)
    acc_ref[...] = jnp.zeros(acc_ref.shape, F32)
    rep = A_HEADS // A_KV_HEADS
    q_groups = [jnp.concatenate([q_ref[:, (g * rep + r) * A_DIM:(g * rep + r + 1) * A_DIM] for r in range(rep)],
                                axis=0) for g in range(A_KV_HEADS)]

    def attend_chunk(c, carry):
        start = pl.multiple_of(c * ck, ck)
        kc = keys_ref[pl.ds(start, ck), :]
        pos = start + s_iota
        sel = ((kc > thr) | ((kc == thr) & (pos <= jcut))) & (pos <= q_pos)
        dist = jnp.where(sel, (q_pos - pos).astype(F32), FAR)
        logits_g = [_dot_nt(k_ref[pl.ds(start, ck), g * A_DIM:(g + 1) * A_DIM], q_groups[g])
                    for g in range(A_KV_HEADS)]
        for g in range(A_KV_HEADS):
            vtg = vt_ref[0, c, g * A_DIM:(g + 1) * A_DIM, :]
            for r in range(rep):
                h = g * rep + r
                s = logits_g[g][:, r * tq:(r + 1) * tq] - (slopes[h] * LOG2E) * dist
                m_prev = m_ref[h:h + 1, :]
                m_new = jnp.maximum(m_prev, _col_reduce(s, jnp.maximum, jnp.max))
                alpha = jnp.exp2(m_prev - m_new)
                psum, pv = None, None
                for k0 in range(0, ck, slab):
                    p = jnp.exp2(s[k0:k0 + slab, :] - m_new)
                    part = _tree([p[r0:r0 + SUBLANES, :] for r0 in range(0, slab, SUBLANES)], jnp.add)
                    psum = part if psum is None else psum + part
                    contrib = _dot(vtg[:, k0:k0 + slab], p.astype(BF16))
                    pv = contrib if pv is None else pv + contrib
                l_ref[h:h + 1, :] = alpha * l_ref[h:h + 1, :] + jnp.sum(psum, axis=0, keepdims=True)
                acc_ref[h * A_DIM:(h + 1) * A_DIM, :] = alpha * acc_ref[h * A_DIM:(h + 1) * A_DIM, :] + pv
                m_ref[h:h + 1, :] = m_new
        return carry

    lax.fori_loop(0, nch, attend_chunk, 0)

    for h in range(A_HEADS):
        out_t = acc_ref[h * A_DIM:(h + 1) * A_DIM, :] / l_ref[h:h + 1, :]
        z = z_ref[:, h * A_DIM:(h + 1) * A_DIM].astype(F32)
        o_ref[:, h * A_DIM:(h + 1) * A_DIM] = (out_t.T * jax.nn.silu(z)).astype(o_ref.dtype)


def _key_chunk(seq):
    return min(512, seq)


def _sparse_attention(p_arr, vt, wt, batch, seq):
    tq = min(2 * LANES, seq)
    nq = seq // tq
    ck = _key_chunk(seq)
    topk = min(IDX_TOPK_MAX, seq // 4)
    idx_bits = int(seq).bit_length()
    slopes = _alibi_slopes(A_HEADS)
    assert seq % ck == 0 and ck % tq == 0 and min(slopes) * LOG2E * FAR > -NEG
    return pl.pallas_call(
        functools.partial(_sparse_kernel, topk=topk, ck=ck, slopes=slopes, idx_bits=idx_bits),
        grid=(batch, nq),
        in_specs=[pl.BlockSpec((tq, BRANCH_WIDTH), lambda b, i: (b * nq + i, P_OFF['a_q'] // BRANCH_WIDTH)),
                  pl.BlockSpec((tq, BRANCH_WIDTH), lambda b, i: (b * nq + i, P_OFF['a_iq'] // BRANCH_WIDTH)),
                  pl.BlockSpec((tq, BRANCH_WIDTH), lambda b, i: (b * nq + i, P_OFF['a_z'] // BRANCH_WIDTH)),
                  pl.BlockSpec((seq, LANES), lambda b, i: (b, P_OFF['a_ik'] // LANES)),
                  pl.BlockSpec((seq, 2 * LANES), lambda b, i: (b, P_OFF['a_k'] // (2 * LANES))),
                  pl.BlockSpec((1, seq // ck, 2 * LANES, ck), lambda b, i: (b, 0, 0, 0)),
                  pl.BlockSpec((1, IDX_HEADS, tq), lambda b, i: (b, 0, i))],
        out_specs=pl.BlockSpec((tq, BRANCH_WIDTH), lambda b, i: (b * nq + i, 0)),
        out_shape=jax.ShapeDtypeStruct((batch * seq, BRANCH_WIDTH), BF16),
        scratch_shapes=[pltpu.VMEM((seq, tq), jnp.int32), pltpu.VMEM((8, tq), jnp.int32),
                        pltpu.VMEM((A_HEADS, tq), F32), pltpu.VMEM((A_HEADS, tq), F32),
                        pltpu.VMEM((A_HEADS * A_DIM, tq), F32)],
        compiler_params=_params(("parallel", "arbitrary")),
        name="sparse_attention",
    )(p_arr, p_arr, p_arr, p_arr, p_arr, vt, wt)


def _swap_halves(w):
    half = w.shape[-1] // 2
    return jnp.concatenate([-w[..., half:], w[..., :half]], axis=-1)


def _prepare_in_proj(w_in):
    sizes = [n for _, n in IN_SEGMENTS]
    offs = np.concatenate([[0], np.cumsum(sizes)])
    span = {name: (int(offs[k]), int(offs[k + 1])) for k, (name, _) in enumerate(IN_SEGMENTS)}
    wt32 = jnp.swapaxes(w_in, 1, 2)
    wt16 = wt32.astype(BF16)
    seg16 = lambda name: wt16[:, span[name][0]:span[name][1], :]
    seg32 = lambda name: wt32[:, span[name][0]:span[name][1], :]
    rows = []
    for name, _ in P_LAYOUT:
        if name == 'a_ik':
            rows += [seg16('a_ik'), seg16('a_ik')]
        elif name == 'd_kr':
            kr = seg16('d_kr')
            rows += [kr, -kr[:, D_ROPE // 2:, :], kr[:, :D_ROPE // 2, :]]
        elif name == 'a_q':
            rows.append((seg32('a_q') * (A_DIM ** -0.5 * LOG2E)).astype(BF16))
        elif name == 'b_q':
            rows.append((seg32('b_q') * (B_DIM ** -0.5 * LOG2E)).astype(BF16))
        elif name == 'c_q':
            rows.append((seg32('c_q') * (C_DIM ** -0.5 * LOG2E)).astype(BF16))
        else:
            rows.append(seg16(name))
    w_main_t = jnp.concatenate(rows, axis=1)
    pad = jnp.zeros((w_in.shape[0], LANES - IDX_HEADS - B_HEADS, w_in.shape[1]), BF16)
    w_small_t = jnp.concatenate([seg16('a_iw'), seg16('b_f'), pad], axis=1)
    return w_main_t, w_small_t


def _prepare_latent(dq_up, dkv_up):
    q = dq_up.reshape(D_Q_LORA, D_HEADS, D_NOPE + D_ROPE) * ((D_NOPE + D_ROPE) ** -0.5 * LOG2E)
    rope = q[..., D_NOPE:]
    q_up = jnp.concatenate([q[..., :D_NOPE], rope, _swap_halves(rope)], axis=-1)
    kv = dkv_up.reshape(D_KV_LORA, D_HEADS, D_NOPE + D_VDIM)
    return (q_up.reshape(D_Q_LORA, -1).astype(BF16),
            kv[..., :D_NOPE].reshape(D_KV_LORA, -1).astype(BF16),
            kv[..., D_NOPE:].reshape(D_KV_LORA, -1).T.astype(BF16))


def _rope_table(positions):
    half = D_ROPE // 2
    inv_freq = ROPE_THETA ** (-jnp.arange(half, dtype=F32) / half)
    ang = positions.astype(F32)[..., None] * inv_freq
    cos, sin = jnp.cos(ang), jnp.sin(ang)
    tab = jnp.concatenate([cos, cos, sin, sin], axis=-1)
    return tab.reshape(-1, 4 * half)


def _layer(layer, x, xb, tab, batch, seq, w_main, w_small, w_gate, w_branch, w_out, dq_gain, dq_up, dkv_gain,
           dkv_up, f_bias, sinks, ln_gain, ln_bias, alpha, last):
    proj = _matmul(xb, w_main, layer, BF16, transposed=True)
    small = _matmul(xb, w_small, layer, F32, transposed=True, tn=LANES)

    ck = _key_chunk(seq)
    wt = small[:, :IDX_HEADS].reshape(batch, seq, IDX_HEADS).transpose(0, 2, 1)
    a_v = proj[:, P_OFF['a_v']:P_OFF['a_v'] + 2 * LANES]
    vt = a_v.reshape(batch, seq // ck, ck, 2 * LANES).transpose(0, 1, 3, 2)
    o_a = _sparse_attention(proj, vt, wt, batch, seq)

    f_rows = small[:, IDX_HEADS:IDX_HEADS + B_HEADS].reshape(batch, seq, B_HEADS).transpose(0, 2, 1)
    cum = _forget_cumsum(f_rows.reshape(batch * B_HEADS, seq),
                         jnp.tile(f_bias.astype(F32), batch).reshape(batch * B_HEADS, 1))
    cum = cum.reshape(batch, B_HEADS, seq).transpose(0, 2, 1).reshape(batch * seq, B_HEADS)
    cum = jnp.pad(cum, ((0, 0), (0, LANES - B_HEADS)))
    vt_b = proj[:, P_OFF['b_v']:P_OFF['b_v'] + BRANCH_WIDTH].T
    o_b = _flash_attention(proj, P_OFF['b_q'] // BRANCH_WIDTH, proj, P_OFF['b_k'] // BRANCH_WIDTH, vt_b,
                           proj, P_OFF['b_z'] // BRANCH_WIDTH, batch=batch, seq=seq, heads=B_HEADS,
                           dqk=B_DIM, dv=B_DIM, cum=cum)

    o_c = _sliding_window(proj, sinks, batch, seq)

    q_up, k_up, v_up_t = _prepare_latent(dq_up, dkv_up)
    q_d, k_d, vt_d = _latent_prep(proj, tab, dq_gain, dkv_gain, q_up, k_up, v_up_t)
    o_d = _flash_attention(q_d, 0, k_d, 0, vt_d, proj, P_OFF['d_z'] // BRANCH_WIDTH, batch=batch, seq=seq,
                           heads=D_HEADS, dqk=2 * LANES, dv=D_VDIM)

    merged = _gated_merge(xb, w_gate, (o_a, o_b, o_c, o_d), w_branch, layer)
    y = _matmul(merged, w_out, layer, F32)
    return _residual_layernorm(x, y, ln_gain, ln_bias, alpha, with_bf16=not last)


def kernel(x, positions, w_in, w_gate, w_branch, w_out, dq_gain, dq_up, dkv_gain, dkv_up, f_bias, sinks,
           ln_gain, ln_bias):
    batch, seq, d = x.shape
    depth = w_in.shape[0]
    alpha = (2 * depth) ** 0.25
    tab = _rope_table(positions)
    xf = x.reshape(batch * seq, d)
    xb = _cast_bf16(xf)
    w_main, w_small = _prepare_in_proj(w_in)
    w_gate, w_branch, w_out = w_gate.astype(BF16), w_branch.astype(BF16), w_out.astype(BF16)
    for l in range(depth):
        xf, xb = _layer(l, xf, xb, tab, batch, seq, w_main, w_small, w_gate, w_branch, w_out, dq_gain[l],
                        dq_up[l], dkv_gain[l], dkv_up[l], f_bias[l], sinks[l], ln_gain[l], ln_bias[l], alpha,
                        last=(l == depth - 1))
    return xf.reshape(batch, seq, d)
```

```python
import functools

import numpy as np
import jax
import jax.numpy as jnp
from jax import lax
from jax.experimental import pallas as pl
from jax.experimental.pallas import tpu as pltpu

BRANCH_WIDTH = 1024
A_HEADS, A_KV_HEADS, A_DIM = 8, 2, 128
IDX_HEADS, IDX_DIM, IDX_TOPK_MAX = 16, 64, 256
B_HEADS, B_DIM = 8, 128
C_HEADS, C_KV_HEADS, C_DIM, WINDOW = 16, 2, 64, 128
D_HEADS, D_Q_LORA, D_KV_LORA, D_NOPE, D_ROPE, D_VDIM = 8, 768, 256, 128, 64, 128
ROPE_THETA = 10000.0
RMS_EPS = 1e-6
LN_EPS = 1e-5

IN_SEGMENTS = (
    ('a_q', 1024), ('a_k', 256), ('a_v', 256), ('a_iq', 1024), ('a_ik', 64), ('a_iw', 16), ('a_z', 1024),
    ('b_q', 1024), ('b_k', 1024), ('b_v', 1024), ('b_f', 8), ('b_z', 1024),
    ('c_q', 1024), ('c_k', 128), ('c_v', 128), ('c_z', 1024),
    ('d_cq', 768), ('d_ckv', 256), ('d_kr', 64), ('d_z', 1024),
)

P_LAYOUT = (
    ('d_cq', 768), ('d_ckv', 256), ('a_q', 1024), ('a_iq', 1024), ('a_z', 1024),
    ('b_q', 1024), ('b_k', 1024), ('b_v', 1024), ('b_z', 1024), ('c_q', 1024), ('c_z', 1024),
    ('d_z', 1024), ('a_k', 256), ('a_v', 256), ('a_ik', 128), ('d_kr', 128), ('c_k', 128), ('c_v', 128),
)
P_OFF = {}
_o = 0
for _n, _w in P_LAYOUT:
    assert _o % _w == 0
    P_OFF[_n] = _o
    _o += _w
P_WIDTH = _o

LANES = 128
SUBLANES = 8
NEG = -1e30
FAR = 1e34
LOG2E = 1.4426950408889634
INT_MIN = -2 ** 31
VMEM_LIMIT = 56 * 1024 * 1024

F32 = jnp.float32
BF16 = jnp.bfloat16


def _params(sem, vmem=VMEM_LIMIT):
    return pltpu.CompilerParams(dimension_semantics=sem, vmem_limit_bytes=vmem)


def _dot(a, b):
    return jnp.dot(a, b, preferred_element_type=F32)


def _dot_nt(a, b):
    return lax.dot_general(a, b, (((1,), (1,)), ((), ())), preferred_element_type=F32)


def _alibi_slopes(n_heads):
    return [float(np.float32(2.0 ** (-8.0 * (h + 1) / n_heads))) for h in range(n_heads)]


def _mm_kernel(x_ref, w_ref, o_ref):
    o_ref[...] = _dot(x_ref[...], w_ref[0]).astype(o_ref.dtype)


def _mm_nt_kernel(x_ref, wt_ref, o_ref):
    o_ref[...] = _dot_nt(x_ref[...], wt_ref[0]).astype(o_ref.dtype)


def _matmul(x, w, layer, out_dtype, transposed=False, tm=1024, tn=1024):
    m, k = x.shape
    n = w.shape[1] if transposed else w.shape[2]
    tm, tn = min(tm, m), min(tn, n)
    assert m % tm == 0 and n % tn == 0, (m, n, tm, tn)
    if transposed:
        body, w_spec = _mm_nt_kernel, pl.BlockSpec((1, tn, k), lambda i, j: (layer, j, 0))
    else:
        body, w_spec = _mm_kernel, pl.BlockSpec((1, k, tn), lambda i, j: (layer, 0, j))
    return pl.pallas_call(
        body,
        grid=(m // tm, n // tn),
        in_specs=[pl.BlockSpec((tm, k), lambda i, j: (i, 0)), w_spec],
        out_specs=pl.BlockSpec((tm, tn), lambda i, j: (i, j)),
        out_shape=jax.ShapeDtypeStruct((m, n), out_dtype),
        compiler_params=_params(("parallel", "parallel")),
        name="dense_matmul",
    )(x, w)


def _cast_kernel(x_ref, o_ref):
    o_ref[...] = x_ref[...].astype(o_ref.dtype)


def _cast_bf16(x, tm=512):
    t, d = x.shape
    tm = min(tm, t)
    row = pl.BlockSpec((tm, d), lambda i: (i, 0))
    return pl.pallas_call(
        _cast_kernel, grid=(t // tm,), in_specs=[row], out_specs=row,
        out_shape=jax.ShapeDtypeStruct((t, d), BF16),
        compiler_params=_params(("parallel",)),
        name="cast_bf16",
    )(x)


def _merge_kernel(x_ref, wg_ref, b0_ref, b1_ref, b2_ref, b3_ref, wb_ref, o_ref, acc_ref, *, sub):
    n = pl.program_id(2)

    @pl.when((pl.program_id(0) == 0) & (pl.program_id(1) == 0) & (n == 0))
    def _():
        acc_ref[...] = jnp.zeros(acc_ref.shape, F32)

    branch = jnp.where(n == 0, b0_ref[...], jnp.where(n == 1, b1_ref[...],
                                                      jnp.where(n == 2, b2_ref[...], b3_ref[...])))
    x = x_ref[...]
    for c in range(o_ref.shape[1] // sub):
        cols = slice(c * sub, (c + 1) * sub)
        gate = jax.nn.sigmoid(_dot(x, wg_ref[0, 0, :, cols]))
        contrib = gate * _dot(branch, wb_ref[0, 0, :, cols])
        acc = jnp.where(n == 0, 0.0, acc_ref[:, cols]) + contrib
        acc_ref[:, cols] = acc
        o_ref[:, cols] = acc.astype(o_ref.dtype)


def _gated_merge(xb, wg, branches, wb, layer, tm=512, tn=1024, sub=512):
    t, d = xb.shape
    tm, tn = min(tm, t), min(tn, d)
    sub = min(sub, tn)
    bw = branches[0].shape[1]
    bspec = pl.BlockSpec((tm, bw), lambda i, j, n: (i, 0))
    return pl.pallas_call(
        functools.partial(_merge_kernel, sub=sub),
        grid=(t // tm, d // tn, 4),
        in_specs=[pl.BlockSpec((tm, d), lambda i, j, n: (i, 0)),
                  pl.BlockSpec((1, 1, d, tn), lambda i, j, n: (layer, n, 0, j)),
                  bspec, bspec, bspec, bspec,
                  pl.BlockSpec((1, 1, bw, tn), lambda i, j, n: (layer, n, 0, j))],
        out_specs=pl.BlockSpec((tm, tn), lambda i, j, n: (i, j)),
        out_shape=jax.ShapeDtypeStruct((t, d), BF16),
        scratch_shapes=[pltpu.VMEM((tm, tn), F32)],
        compiler_params=_params(("arbitrary", "arbitrary", "arbitrary")),
        name="gated_merge",
    )(xb, wg, *branches, wb)


def _ln_kernel(x_ref, y_ref, g_ref, b_ref, o_ref, *maybe_ob_ref, alpha):
    r = alpha * x_ref[...] + y_ref[...]
    mu = jnp.mean(r, axis=-1, keepdims=True)
    c = r - mu
    var = jnp.mean(c * c, axis=-1, keepdims=True)
    out = c * lax.rsqrt(var + LN_EPS) * g_ref[...] + b_ref[...]
    o_ref[...] = out
    for ob_ref in maybe_ob_ref:
        ob_ref[...] = out.astype(BF16)


def _residual_layernorm(x, y, gain, bias, alpha, with_bf16, tm=256):
    t, d = x.shape
    tm = min(tm, t)
    row = pl.BlockSpec((tm, d), lambda i: (i, 0))
    vec = pl.BlockSpec((1, d), lambda i: (0, 0))
    out_shape = [jax.ShapeDtypeStruct((t, d), F32)] + ([jax.ShapeDtypeStruct((t, d), BF16)] if with_bf16 else [])
    outs = pl.pallas_call(
        functools.partial(_ln_kernel, alpha=alpha),
        grid=(t // tm,),
        in_specs=[row, row, vec, vec],
        out_specs=[row] * len(out_shape),
        out_shape=out_shape,
        compiler_params=_params(("parallel",)),
        name="residual_layernorm",
    )(x, y, gain.reshape(1, d), bias.reshape(1, d))
    return (outs[0], outs[1]) if with_bf16 else (outs[0], None)


def _forget_cumsum_kernel(f_ref, bias_ref, c_ref):
    rows, s = f_ref.shape
    lane = lax.broadcasted_iota(jnp.int32, (rows, LANES), 1)
    carry = jnp.zeros((rows, 1), F32)
    for c in range(s // LANES):
        x = jax.nn.log_sigmoid(f_ref[:, c * LANES:(c + 1) * LANES] + bias_ref[...])
        shift = 1
        while shift < LANES:
            x = x + jnp.where(lane >= shift, pltpu.roll(x, shift, 1), 0.0)
            shift *= 2
        x = x + carry
        c_ref[:, c * LANES:(c + 1) * LANES] = x * LOG2E
        carry = x[:, LANES - 1:LANES]


def _forget_cumsum(f_rows, bias_rows):
    rows, s = f_rows.shape
    full = pl.BlockSpec((rows, s), lambda: (0, 0))
    return pl.pallas_call(
        _forget_cumsum_kernel,
        in_specs=[full, pl.BlockSpec((rows, 1), lambda: (0, 0))],
        out_specs=full,
        out_shape=jax.ShapeDtypeStruct((rows, s), F32),
        name="forget_cumsum",
    )(f_rows, bias_rows)


def _tree(parts, op, ways=8):
    parts = list(parts)
    if len(parts) > ways:
        accs = parts[:ways]
        for a in range(ways, len(parts)):
            accs[a % ways] = op(accs[a % ways], parts[a])
        parts = accs
    while len(parts) > 1:
        nxt = [op(parts[a], parts[a + 1]) for a in range(0, len(parts) - 1, 2)]
        if len(parts) % 2:
            nxt.append(parts[-1])
        parts = nxt
    return parts[0]


def _col_reduce(x, op, reduce_fn):
    rows = x.shape[0]
    part = _tree([x[r:r + SUBLANES, :] for r in range(0, rows, SUBLANES)], op)
    return reduce_fn(part, axis=0, keepdims=True)


def _flash_kernel(i_tab, j_tab, *refs, fox, heads, dqk, dv, tile, slab, kt):
    if fox:
        q_ref, k_ref, vt_ref, ck_ref, z_ref, o_ref, m_sc, l_sc, acc_sc = refs
    else:
        q_ref, k_ref, vt_ref, z_ref, o_ref, m_sc, l_sc, acc_sc = refs
    pair = pl.program_id(1)
    i = i_tab[pair]
    j = j_tab[pair]

    @pl.when(j == 0)
    def _():
        m_sc[...] = jnp.full(m_sc.shape, NEG, F32)
        l_sc[...] = jnp.zeros(l_sc.shape, F32)
        acc_sc[...] = jnp.zeros(acc_sc.shape, F32)

    def step(masked, sub):
        k0_tile = sub * tile
        if masked:
            causal = (lax.broadcasted_iota(jnp.int32, (tile, tile), 0)
                      <= lax.broadcasted_iota(jnp.int32, (tile, tile), 1))
        def logits(h):
            return _dot_nt(k_ref[k0_tile:k0_tile + tile, h * dqk:(h + 1) * dqk],
                           q_ref[:, h * dqk:(h + 1) * dqk])

        s_next = logits(0)
        for h in range(heads):
            s = s_next
            if h + 1 < heads:
                s_next = logits(h + 1)
            if fox:
                s = s - ck_ref[k0_tile:k0_tile + tile, h:h + 1]
            if masked:
                s = jnp.where(causal, s, NEG)
            m_prev = m_sc[h:h + 1, :]
            m_new = jnp.maximum(m_prev, _col_reduce(s, jnp.maximum, jnp.max))
            alpha = jnp.exp2(m_prev - m_new)
            psum, pv = None, None
            for k0 in range(0, tile, slab):
                p = jnp.exp2(s[k0:k0 + slab, :] - m_new)
                part = _tree([p[r:r + SUBLANES, :] for r in range(0, slab, SUBLANES)], jnp.add)
                psum = part if psum is None else psum + part
                contrib = _dot(vt_ref[h * dv:(h + 1) * dv, k0_tile + k0:k0_tile + k0 + slab], p.astype(BF16))
                pv = contrib if pv is None else pv + contrib
            l_sc[h:h + 1, :] = alpha * l_sc[h:h + 1, :] + jnp.sum(psum, axis=0, keepdims=True)
            acc_sc[h * dv:(h + 1) * dv, :] = alpha * acc_sc[h * dv:(h + 1) * dv, :] + pv
            m_sc[h:h + 1, :] = m_new

    for sub in range(kt):
        key_tile = j * kt + sub

        @pl.when(key_tile < i)
        def _(sub=sub):
            step(False, sub)

        @pl.when(key_tile == i)
        def _(sub=sub):
            step(True, sub)

    @pl.when(j == i // kt)
    def _():
        for h in range(heads):
            out_t = acc_sc[h * dv:(h + 1) * dv, :] / l_sc[h:h + 1, :]
            z = z_ref[:, h * dv:(h + 1) * dv].astype(F32)
            o_ref[:, h * dv:(h + 1) * dv] = (out_t.T * jax.nn.silu(z)).astype(o_ref.dtype)


def _flash_attention(q_arr, q_blk, k_arr, k_blk, vt_arr, z_arr, z_blk, *, batch, seq, heads,
                     dqk, dv, cum=None, tile=512):
    tile = min(tile, seq)
    assert seq % tile == 0, (seq, tile)
    nq = seq // tile
    fox = cum is not None
    kt = 2 if nq % 2 == 0 else 1
    nk = nq // kt
    pairs = [(i, j) for i in range(nq) for j in range(i // kt + 1)]
    i_tab = jnp.asarray([p[0] for p in pairs], jnp.int32)
    j_tab = jnp.asarray([p[1] for p in pairs], jnp.int32)
    in_specs = [
        pl.BlockSpec((tile, heads * dqk), lambda b, p, it, jt: (b * nq + it[p], q_blk)),
        pl.BlockSpec((kt * tile, heads * dqk), lambda b, p, it, jt: (b * nk + jt[p], k_blk)),
        pl.BlockSpec((heads * dv, kt * tile), lambda b, p, it, jt: (0, b * nk + jt[p])),
    ]
    args = [q_arr, k_arr, vt_arr]
    if fox:
        in_specs.append(pl.BlockSpec((kt * tile, LANES), lambda b, p, it, jt: (b * nk + jt[p], 0)))
        args.append(cum)
    in_specs.append(pl.BlockSpec((tile, heads * dv), lambda b, p, it, jt: (b * nq + it[p], z_blk)))
    args.append(z_arr)
    return pl.pallas_call(
        functools.partial(_flash_kernel, fox=fox, heads=heads, dqk=dqk, dv=dv, tile=tile,
                          slab=min(2 * LANES, tile), kt=kt),
        grid_spec=pltpu.PrefetchScalarGridSpec(
            num_scalar_prefetch=2,
            grid=(batch, len(pairs)),
            in_specs=in_specs,
            out_specs=pl.BlockSpec((tile, heads * dv), lambda b, p, it, jt: (b * nq + it[p], 0)),
            scratch_shapes=[pltpu.VMEM((heads, tile), F32), pltpu.VMEM((heads, tile), F32),
                            pltpu.VMEM((heads * dv, tile), F32)]),
        out_shape=jax.ShapeDtypeStruct((batch * seq, heads * dv), BF16),
        compiler_params=_params(("parallel", "arbitrary")),
        name="flash_fox" if fox else "flash_latent",
    )(i_tab, j_tab, *args)


def _swa_kernel(q_ref, kvc_ref, kvp_ref, z_ref, sink_ref, o_ref, *, slopes):
    i = pl.program_id(1)
    w = WINDOW
    pairs_per_group = C_HEADS // C_KV_HEADS // 2
    kv = jnp.concatenate([kvp_ref[...], kvc_ref[...]], axis=0).astype(F32)
    lane = lax.broadcasted_iota(jnp.int32, (2 * w, LANES), 1)
    low = lane < C_DIM

    def halves(t):
        g0_lo = jnp.where(low, t, 0.0)
        g1_hi = jnp.where(low, 0.0, t)
        return ((g0_lo, pltpu.roll(g0_lo, C_DIM, 1)), (pltpu.roll(g1_hi, C_DIM, 1), g1_hi))

    k_half = halves(kv[:, :LANES])
    v_half = halves(kv[:, LANES:])
    key = lax.broadcasted_iota(jnp.int32, (2 * w, w), 0)
    qry = lax.broadcasted_iota(jnp.int32, (2 * w, w), 1)
    dist = qry + w - key
    first_key = jnp.where(i > 0, 0, w)
    valid = (dist >= 0) & (dist < w) & (key >= first_key)
    dist_m = jnp.where(valid, dist.astype(F32), -NEG)

    q_rows = [jnp.concatenate([q_ref[:, (g * pairs_per_group + jj) * LANES:(g * pairs_per_group + jj + 1) * LANES]
                               for jj in range(pairs_per_group)], axis=0) for g in range(C_KV_HEADS)]
    logits = [[_dot_nt(k_half[g][half].astype(BF16), q_rows[g]) for half in range(2)] for g in range(C_KV_HEADS)]
    for g in range(C_KV_HEADS):
        out_t = None
        for half in range(2):
            probs, inv = [], []
            for jj in range(pairs_per_group):
                h = 2 * (g * pairs_per_group + jj) + half
                s = logits[g][half][:, jj * w:(jj + 1) * w] - (slopes[h] * LOG2E) * dist_m
                sink = jnp.full((1, w), sink_ref[h], F32) * LOG2E
                m = jnp.maximum(_col_reduce(s, jnp.maximum, jnp.max), sink)
                p = jnp.exp2(s - m)
                inv.append(1.0 / (_col_reduce(p, jnp.add, jnp.sum) + jnp.exp2(sink - m)))
                probs.append(p.astype(BF16))
            vt = v_half[g][half].T.astype(BF16)
            contrib = _dot(vt, jnp.concatenate(probs, axis=1)) * jnp.concatenate(inv, axis=1)
            out_t = contrib if out_t is None else out_t + contrib
        for jj in range(pairs_per_group):
            cols = slice((g * pairs_per_group + jj) * LANES, (g * pairs_per_group + jj + 1) * LANES)
            z = z_ref[:, cols].astype(F32)
            o_ref[:, cols] = (out_t[:, jj * w:(jj + 1) * w].T * jax.nn.silu(z)).astype(o_ref.dtype)


def _sliding_window(p_arr, sinks, batch, seq):
    w = WINDOW
    nb = seq // w
    kv_blk = P_OFF['c_k'] // (2 * LANES)
    return pl.pallas_call(
        functools.partial(_swa_kernel, slopes=_alibi_slopes(C_HEADS)),
        grid=(batch, nb),
        in_specs=[pl.BlockSpec((w, BRANCH_WIDTH), lambda b, i: (b * nb + i, P_OFF['c_q'] // BRANCH_WIDTH)),
                  pl.BlockSpec((w, 2 * LANES), lambda b, i: (b * nb + i, kv_blk)),
                  pl.BlockSpec((w, 2 * LANES), lambda b, i: (b * nb + jnp.maximum(i - 1, 0), kv_blk)),
                  pl.BlockSpec((w, BRANCH_WIDTH), lambda b, i: (b * nb + i, P_OFF['c_z'] // BRANCH_WIDTH)),
                  pl.BlockSpec(memory_space=pltpu.SMEM)],
        out_specs=pl.BlockSpec((w, BRANCH_WIDTH), lambda b, i: (b * nb + i, 0)),
        out_shape=jax.ShapeDtypeStruct((batch * seq, BRANCH_WIDTH), BF16),
        compiler_params=_params(("parallel", "parallel")),
        name="sliding_window",
    )(p_arr, p_arr, p_arr, p_arr, sinks.astype(F32))


def _latent_prep_kernel(cq_ref, ckv_ref, kr_ref, tab_ref, qg_ref, kvg_ref, qup_ref, kup_ref, vupt_ref,
                        q_ref, k_ref, vt_ref):
    def rms(x, g):
        return x * lax.rsqrt(jnp.mean(x * x, axis=-1, keepdims=True) + RMS_EPS) * g

    tab = tab_ref[...]
    lane = lax.broadcasted_iota(jnp.int32, tab.shape, 1)

    def rotate(t):
        r = t * tab
        return r + pltpu.roll(r, D_ROPE, 1)

    q = _dot(rms(cq_ref[...].astype(F32), qg_ref[...]).astype(BF16), qup_ref[...])
    ckv = rms(ckv_ref[...].astype(F32), kvg_ref[...]).astype(BF16)
    k_nope = _dot(ckv, kup_ref[...])
    vt_ref[...] = _dot_nt(vupt_ref[...], ckv).astype(vt_ref.dtype)
    k_rot = jnp.where(lane < D_ROPE, rotate(kr_ref[...].astype(F32)), 0.0).astype(k_ref.dtype)
    hw = 2 * LANES
    for h in range(D_HEADS):
        q_ref[:, h * hw:h * hw + LANES] = q[:, h * hw:h * hw + LANES].astype(q_ref.dtype)
        q_ref[:, h * hw + LANES:(h + 1) * hw] = rotate(q[:, h * hw + LANES:(h + 1) * hw]).astype(q_ref.dtype)
        k_ref[:, h * hw:h * hw + LANES] = k_nope[:, h * LANES:(h + 1) * LANES].astype(k_ref.dtype)
        k_ref[:, h * hw + LANES:(h + 1) * hw] = k_rot


def _latent_prep(p_arr, tab, q_gain, kv_gain, q_up, k_up, v_up_t, tm=512):
    t = p_arr.shape[0]
    tm = min(tm, t)
    hw = 2 * LANES

    def const(shape):
        return pl.BlockSpec(shape, lambda i: (0, 0))

    return pl.pallas_call(
        _latent_prep_kernel,
        grid=(t // tm,),
        in_specs=[pl.BlockSpec((tm, D_Q_LORA), lambda i: (i, P_OFF['d_cq'] // D_Q_LORA)),
                  pl.BlockSpec((tm, D_KV_LORA), lambda i: (i, P_OFF['d_ckv'] // D_KV_LORA)),
                  pl.BlockSpec((tm, LANES), lambda i: (i, P_OFF['d_kr'] // LANES)),
                  pl.BlockSpec((tm, LANES), lambda i: (i, 0)),
                  const((1, D_Q_LORA)), const((1, D_KV_LORA)),
                  const((D_Q_LORA, D_HEADS * hw)), const((D_KV_LORA, D_HEADS * LANES)),
                  const((D_HEADS * D_VDIM, D_KV_LORA))],
        out_specs=[pl.BlockSpec((tm, D_HEADS * hw), lambda i: (i, 0)),
                   pl.BlockSpec((tm, D_HEADS * hw), lambda i: (i, 0)),
                   pl.BlockSpec((D_HEADS * D_VDIM, tm), lambda i: (0, i))],
        out_shape=[jax.ShapeDtypeStruct((t, D_HEADS * hw), BF16),
                   jax.ShapeDtypeStruct((t, D_HEADS * hw), BF16),
                   jax.ShapeDtypeStruct((D_HEADS * D_VDIM, t), BF16)],
        compiler_params=_params(("parallel",)),
        name="latent_prep",
    )(p_arr, p_arr, p_arr, tab, q_gain.reshape(1, -1), kv_gain.reshape(1, -1), q_up, k_up, v_up_t)


def _sparse_kernel(q_ref, iq_ref, z_ref, ik_ref, k_ref, vt_ref, wt_ref, iqn_ref, wtn_ref, o_ref,
                   keys_ref, jcut_ref, m_ref, l_ref, acc_ref, *, topk, ck, slopes, idx_bits, nq):
    i = pl.program_id(1)
    tq = q_ref.shape[0]
    nch = ((i + 1) * tq + ck - 1) // ck
    nch_next = ((i + 2) * tq + ck - 1) // ck
    slot = lax.rem(i, 2)
    slab = min(2 * LANES, ck)
    lane_q = lax.broadcasted_iota(jnp.int32, (ck, tq), 1)
    q_pos = i * tq + lane_q
    s_iota = lax.broadcasted_iota(jnp.int32, (ck, tq), 0)
    lane = lax.broadcasted_iota(jnp.int32, (ck, LANES), 1)
    low = lane < IDX_DIM
    w_scale = IDX_HEADS ** -0.5 * IDX_DIM ** -0.5

    def score_chunk(c, iq_r, w_all, first_q, dst_slot):
        start = pl.multiple_of(c * ck, ck)
        ikc = ik_ref[pl.ds(start, ck), :]
        ik_lo = jnp.where(low, ikc, jnp.zeros_like(ikc))
        ik_hi = jnp.where(low, jnp.zeros_like(ikc), ikc)
        acc = jnp.zeros((ck, tq), F32)
        for pair in range(IDX_HEADS // 2):
            iqp = iq_r[:, pair * LANES:(pair + 1) * LANES]
            acc = acc + w_all[2 * pair:2 * pair + 1, :] * jnp.maximum(_dot_nt(ik_lo, iqp), 0.0)
            acc = acc + w_all[2 * pair + 1:2 * pair + 2, :] * jnp.maximum(_dot_nt(ik_hi, iqp), 0.0)
        acc = jnp.where(acc == 0.0, 0.0, acc)
        bits = lax.bitcast_convert_type(acc, jnp.int32)
        key = bits ^ ((bits >> 31) & 0x7FFFFFFF)
        key = jnp.where(start + s_iota <= first_q + lane_q, key, INT_MIN)
        keys_ref[dst_slot, pl.ds(start, ck), :] = key

    @pl.when(i == 0)
    def _():
        w_all = wt_ref[0] * w_scale

        def body(c, carry):
            score_chunk(c, iq_ref, w_all, 0, 0)
            return carry
        lax.fori_loop(0, nch, body, 0)

    def count(pred_fn):
        def body(c, cnt):
            start = pl.multiple_of(c * ck, ck)
            hit = pred_fn(keys_ref[slot, pl.ds(start, ck), :], start + s_iota)
            ways = 8
            accs = [cnt] + [jnp.zeros((SUBLANES, tq), F32)] * (ways - 1)
            for a, r in enumerate(range(0, ck, SUBLANES)):
                accs[a % ways] = jnp.where(hit[r:r + SUBLANES, :], accs[a % ways] + 1.0, accs[a % ways])
            return _tree(accs, jnp.add)
        part = lax.fori_loop(0, nch, body, jnp.zeros((SUBLANES, tq), F32))
        return jnp.sum(part, axis=0, keepdims=True)

    kf = float(topk)

    def thr_step(it, state):
        thr, n_ge = state
        cand = thr ^ (jnp.int32(1) << (31 - it))
        cnt = count(lambda kc, pos: kc >= cand)
        ok = cnt >= kf
        return jnp.where(ok, cand, thr), jnp.where(ok, cnt, n_ge)

    total = jnp.full((1, tq), 1.0, F32) * (nch * ck).astype(F32)
    thr, n_ge = lax.fori_loop(0, 32, thr_step, (jnp.full((1, tq), INT_MIN, jnp.int32), total))
    tied = (n_ge > kf) & (thr != INT_MIN)
    jcut_ref[...] = jnp.full(jcut_ref.shape, 2 ** 30, jnp.int32)

    @pl.when(jnp.max(tied.astype(F32)) > 0.0)
    def _():
        need = kf - count(lambda kc, pos: kc > thr)

        def cut_step(it, x):
            cand = x + (jnp.int32(1) << (idx_bits - 1 - it))
            cnt = count(lambda kc, pos: (kc == thr) & (pos < cand))
            return jnp.where(cnt < need, cand, x)
        x = lax.fori_loop(0, idx_bits, cut_step, jnp.zeros((1, tq), jnp.int32))
        jcut_ref[...] = jnp.broadcast_to(jnp.where(tied, x, 2 ** 30), jcut_ref.shape)

    jcut = jcut_ref[0:1, :]

    m_ref[...] = jnp.full(m_ref.shape, NEG, F32)
    l_ref[...] = jnp.zeros(l_ref.shape, F32)
    acc_ref[...] = jnp.zeros(acc_ref.shape, F32)
    rep = A_HEADS // A_KV_HEADS
    q_groups = [jnp.concatenate([q_ref[:, (g * rep + r) * A_DIM:(g * rep + r + 1) * A_DIM] for r in range(rep)],
                                axis=0) for g in range(A_KV_HEADS)]

    def attend_chunk(c, score_next):
        start = pl.multiple_of(c * ck, ck)
        kc = keys_ref[slot, pl.ds(start, ck), :]
        pos = start + s_iota
        sel = ((kc > thr) | ((kc == thr) & (pos <= jcut))) & (pos <= q_pos)
        dist = jnp.where(sel, (q_pos - pos).astype(F32), FAR)
        logits_g = [_dot_nt(k_ref[pl.ds(start, ck), g * A_DIM:(g + 1) * A_DIM], q_groups[g])
                    for g in range(A_KV_HEADS)]
        if score_next is not None:
            score_next(c)
        for g in range(A_KV_HEADS):
            vtg = vt_ref[0, c, g * A_DIM:(g + 1) * A_DIM, :]
            for r in range(rep):
                h = g * rep + r
                s = logits_g[g][:, r * tq:(r + 1) * tq] - (slopes[h] * LOG2E) * dist
                m_prev = m_ref[h:h + 1, :]
                m_new = jnp.maximum(m_prev, _col_reduce(s, jnp.maximum, jnp.max))
                alpha = jnp.exp2(m_prev - m_new)
                psum, pv = None, None
                for k0 in range(0, ck, slab):
                    p = jnp.exp2(s[k0:k0 + slab, :] - m_new)
                    part = _tree([p[r0:r0 + SUBLANES, :] for r0 in range(0, slab, SUBLANES)], jnp.add)
                    psum = part if psum is None else psum + part
                    contrib = _dot(vtg[:, k0:k0 + slab], p.astype(BF16))
                    pv = contrib if pv is None else pv + contrib
                l_ref[h:h + 1, :] = alpha * l_ref[h:h + 1, :] + jnp.sum(psum, axis=0, keepdims=True)
                acc_ref[h * A_DIM:(h + 1) * A_DIM, :] = alpha * acc_ref[h * A_DIM:(h + 1) * A_DIM, :] + pv
                m_ref[h:h + 1, :] = m_new

    @pl.when(i + 1 < nq)
    def _():
        w_next = wtn_ref[0] * w_scale
        score_next = lambda c: score_chunk(c, iqn_ref, w_next, (i + 1) * tq, 1 - slot)

        def both(c, carry):
            attend_chunk(c, score_next)
            return carry
        lax.fori_loop(0, nch, both, 0)

        def rest(c, carry):
            score_next(c)
            return carry
        lax.fori_loop(nch, nch_next, rest, 0)

    @pl.when(i + 1 == nq)
    def _():
        def only(c, carry):
            attend_chunk(c, None)
            return carry
        lax.fori_loop(0, nch, only, 0)

    for h in range(A_HEADS):
        out_t = acc_ref[h * A_DIM:(h + 1) * A_DIM, :] / l_ref[h:h + 1, :]
        z = z_ref[:, h * A_DIM:(h + 1) * A_DIM].astype(F32)
        o_ref[:, h * A_DIM:(h + 1) * A_DIM] = (out_t.T * jax.nn.silu(z)).astype(o_ref.dtype)


def _key_chunk(seq):
    return min(512, seq)


def _sparse_attention(p_arr, vt, wt, batch, seq):
    tq = min(2 * LANES, seq)
    nq = seq // tq
    ck = _key_chunk(seq)
    topk = min(IDX_TOPK_MAX, seq // 4)
    idx_bits = int(seq).bit_length()
    slopes = _alibi_slopes(A_HEADS)
    assert seq % ck == 0 and ck % tq == 0 and min(slopes) * LOG2E * FAR > -NEG
    return pl.pallas_call(
        functools.partial(_sparse_kernel, topk=topk, ck=ck, slopes=slopes, idx_bits=idx_bits, nq=nq),
        grid=(batch, nq),
        in_specs=[pl.BlockSpec((tq, BRANCH_WIDTH), lambda b, i: (b * nq + i, P_OFF['a_q'] // BRANCH_WIDTH)),
                  pl.BlockSpec((tq, BRANCH_WIDTH), lambda b, i: (b * nq + i, P_OFF['a_iq'] // BRANCH_WIDTH)),
                  pl.BlockSpec((tq, BRANCH_WIDTH), lambda b, i: (b * nq + i, P_OFF['a_z'] // BRANCH_WIDTH)),
                  pl.BlockSpec((seq, LANES), lambda b, i: (b, P_OFF['a_ik'] // LANES)),
                  pl.BlockSpec((seq, 2 * LANES), lambda b, i: (b, P_OFF['a_k'] // (2 * LANES))),
                  pl.BlockSpec((1, seq // ck, 2 * LANES, ck), lambda b, i: (b, 0, 0, 0)),
                  pl.BlockSpec((1, IDX_HEADS, tq), lambda b, i: (b, 0, i)),
                  pl.BlockSpec((tq, BRANCH_WIDTH),
                               lambda b, i: (b * nq + jnp.minimum(i + 1, nq - 1), P_OFF['a_iq'] // BRANCH_WIDTH)),
                  pl.BlockSpec((1, IDX_HEADS, tq), lambda b, i: (b, 0, jnp.minimum(i + 1, nq - 1)))],
        out_specs=pl.BlockSpec((tq, BRANCH_WIDTH), lambda b, i: (b * nq + i, 0)),
        out_shape=jax.ShapeDtypeStruct((batch * seq, BRANCH_WIDTH), BF16),
        scratch_shapes=[pltpu.VMEM((2, seq, tq), jnp.int32), pltpu.VMEM((8, tq), jnp.int32),
                        pltpu.VMEM((A_HEADS, tq), F32), pltpu.VMEM((A_HEADS, tq), F32),
                        pltpu.VMEM((A_HEADS * A_DIM, tq), F32)],
        compiler_params=_params(("parallel", "arbitrary")),
        name="sparse_attention",
    )(p_arr, p_arr, p_arr, p_arr, p_arr, vt, wt, p_arr, wt)


def _swap_halves(w):
    half = w.shape[-1] // 2
    return jnp.concatenate([-w[..., half:], w[..., :half]], axis=-1)


def _prepare_in_proj(w_in):
    sizes = [n for _, n in IN_SEGMENTS]
    offs = np.concatenate([[0], np.cumsum(sizes)])
    span = {name: (int(offs[k]), int(offs[k + 1])) for k, (name, _) in enumerate(IN_SEGMENTS)}
    wt32 = jnp.swapaxes(w_in, 1, 2)
    wt16 = wt32.astype(BF16)
    seg16 = lambda name: wt16[:, span[name][0]:span[name][1], :]
    seg32 = lambda name: wt32[:, span[name][0]:span[name][1], :]
    rows = []
    for name, _ in P_LAYOUT:
        if name == 'a_ik':
            rows += [seg16('a_ik'), seg16('a_ik')]
        elif name == 'd_kr':
            kr = seg16('d_kr')
            rows += [kr, -kr[:, D_ROPE // 2:, :], kr[:, :D_ROPE // 2, :]]
        elif name == 'a_q':
            rows.append((seg32('a_q') * (A_DIM ** -0.5 * LOG2E)).astype(BF16))
        elif name == 'b_q':
            rows.append((seg32('b_q') * (B_DIM ** -0.5 * LOG2E)).astype(BF16))
        elif name == 'c_q':
            rows.append((seg32('c_q') * (C_DIM ** -0.5 * LOG2E)).astype(BF16))
        else:
            rows.append(seg16(name))
    w_main_t = jnp.concatenate(rows, axis=1)
    pad = jnp.zeros((w_in.shape[0], LANES - IDX_HEADS - B_HEADS, w_in.shape[1]), BF16)
    w_small_t = jnp.concatenate([seg16('a_iw'), seg16('b_f'), pad], axis=1)
    return w_main_t, w_small_t


def _prepare_latent(dq_up, dkv_up):
    q = dq_up.reshape(D_Q_LORA, D_HEADS, D_NOPE + D_ROPE) * ((D_NOPE + D_ROPE) ** -0.5 * LOG2E)
    rope = q[..., D_NOPE:]
    q_up = jnp.concatenate([q[..., :D_NOPE], rope, _swap_halves(rope)], axis=-1)
    kv = dkv_up.reshape(D_KV_LORA, D_HEADS, D_NOPE + D_VDIM)
    return (q_up.reshape(D_Q_LORA, -1).astype(BF16),
            kv[..., :D_NOPE].reshape(D_KV_LORA, -1).astype(BF16),
            kv[..., D_NOPE:].reshape(D_KV_LORA, -1).T.astype(BF16))


def _rope_table(positions):
    half = D_ROPE // 2
    inv_freq = ROPE_THETA ** (-jnp.arange(half, dtype=F32) / half)
    ang = positions.astype(F32)[..., None] * inv_freq
    cos, sin = jnp.cos(ang), jnp.sin(ang)
    tab = jnp.concatenate([cos, cos, sin, sin], axis=-1)
    return tab.reshape(-1, 4 * half)


def _layer(layer, x, xb, tab, batch, seq, w_main, w_small, w_gate, w_branch, w_out, dq_gain, dq_up, dkv_gain,
           dkv_up, f_bias, sinks, ln_gain, ln_bias, alpha, last):
    proj = _matmul(xb, w_main, layer, BF16, transposed=True)
    small = _matmul(xb, w_small, layer, F32, transposed=True, tn=LANES)

    ck = _key_chunk(seq)
    wt = small[:, :IDX_HEADS].reshape(batch, seq, IDX_HEADS).transpose(0, 2, 1)
    a_v = proj[:, P_OFF['a_v']:P_OFF['a_v'] + 2 * LANES]
    vt = a_v.reshape(batch, seq // ck, ck, 2 * LANES).transpose(0, 1, 3, 2)
    o_a = _sparse_attention(proj, vt, wt, batch, seq)

    f_rows = small[:, IDX_HEADS:IDX_HEADS + B_HEADS].reshape(batch, seq, B_HEADS).transpose(0, 2, 1)
    cum = _forget_cumsum(f_rows.reshape(batch * B_HEADS, seq),
                         jnp.tile(f_bias.astype(F32), batch).reshape(batch * B_HEADS, 1))
    cum = cum.reshape(batch, B_HEADS, seq).transpose(0, 2, 1).reshape(batch * seq, B_HEADS)
    cum = jnp.pad(cum, ((0, 0), (0, LANES - B_HEADS)))
    vt_b = proj[:, P_OFF['b_v']:P_OFF['b_v'] + BRANCH_WIDTH].T
    o_b = _flash_attention(proj, P_OFF['b_q'] // BRANCH_WIDTH, proj, P_OFF['b_k'] // BRANCH_WIDTH, vt_b,
                           proj, P_OFF['b_z'] // BRANCH_WIDTH, batch=batch, seq=seq, heads=B_HEADS,
                           dqk=B_DIM, dv=B_DIM, cum=cum)

    o_c = _sliding_window(proj, sinks, batch, seq)

    q_up, k_up, v_up_t = _prepare_latent(dq_up, dkv_up)
    q_d, k_d, vt_d = _latent_prep(proj, tab, dq_gain, dkv_gain, q_up, k_up, v_up_t)
    o_d = _flash_attention(q_d, 0, k_d, 0, vt_d, proj, P_OFF['d_z'] // BRANCH_WIDTH, batch=batch, seq=seq,
                           heads=D_HEADS, dqk=2 * LANES, dv=D_VDIM)

    merged = _gated_merge(xb, w_gate, (o_a, o_b, o_c, o_d), w_branch, layer)
    y = _matmul(merged, w_out, layer, F32)
    return _residual_layernorm(x, y, ln_gain, ln_bias, alpha, with_bf16=not last)


def kernel(x, positions, w_in, w_gate, w_branch, w_out, dq_gain, dq_up, dkv_gain, dkv_up, f_bias, sinks,
           ln_gain, ln_bias):
    batch, seq, d = x.shape
    depth = w_in.shape[0]
    alpha = (2 * depth) ** 0.25
    tab = _rope_table(positions)
    xf = x.reshape(batch * seq, d)
    xb = _cast_bf16(xf)
    w_main, w_small = _prepare_in_proj(w_in)
    w_gate, w_branch, w_out = w_gate.astype(BF16), w_branch.astype(BF16), w_out.astype(BF16)
    for l in range(depth):
        xf, xb = _layer(l, xf, xb, tab, batch, seq, w_main, w_small, w_gate, w_branch, w_out, dq_gain[l],
                        dq_up[l], dkv_gain[l], dkv_up[l], f_bias[l], sinks[l], ln_gain[l], ln_bias[l], alpha,
                        last=(l == depth - 1))
    return xf.reshape(batch, seq, d)
```

```python
import functools

import numpy as np
import jax
import jax.numpy as jnp
from jax import lax
from jax.experimental import pallas as pl
from jax.experimental.pallas import tpu as pltpu

BRANCH_WIDTH = 1024
A_HEADS, A_KV_HEADS, A_DIM = 8, 2, 128
IDX_HEADS, IDX_DIM, IDX_TOPK_MAX = 16, 64, 256
B_HEADS, B_DIM = 8, 128
C_HEADS, C_KV_HEADS, C_DIM, WINDOW = 16, 2, 64, 128
D_HEADS, D_Q_LORA, D_KV_LORA, D_NOPE, D_ROPE, D_VDIM = 8, 768, 256, 128, 64, 128
ROPE_THETA = 10000.0
RMS_EPS = 1e-6
LN_EPS = 1e-5

IN_SEGMENTS = (
    ('a_q', 1024), ('a_k', 256), ('a_v', 256), ('a_iq', 1024), ('a_ik', 64), ('a_iw', 16), ('a_z', 1024),
    ('b_q', 1024), ('b_k', 1024), ('b_v', 1024), ('b_f', 8), ('b_z', 1024),
    ('c_q', 1024), ('c_k', 128), ('c_v', 128), ('c_z', 1024),
    ('d_cq', 768), ('d_ckv', 256), ('d_kr', 64), ('d_z', 1024),
)

P_LAYOUT = (
    ('d_cq', 768), ('d_ckv', 256), ('a_q', 1024), ('a_iq', 1024), ('a_z', 1024),
    ('b_q', 1024), ('b_k', 1024), ('b_v', 1024), ('b_z', 1024), ('c_q', 1024), ('c_z', 1024),
    ('d_z', 1024), ('a_k', 256), ('a_v', 256), ('a_ik', 128), ('d_kr', 128), ('c_k', 128), ('c_v', 128),
)
P_OFF = {}
_o = 0
for _n, _w in P_LAYOUT:
    assert _o % _w == 0
    P_OFF[_n] = _o
    _o += _w
P_WIDTH = _o

LANES = 128
SUBLANES = 8
NEG = -1e30
FAR = 1e34
LOG2E = 1.4426950408889634
INT_MIN = -2 ** 31
VMEM_LIMIT = 56 * 1024 * 1024

F32 = jnp.float32
BF16 = jnp.bfloat16


def _params(sem, vmem=VMEM_LIMIT):
    return pltpu.CompilerParams(dimension_semantics=sem, vmem_limit_bytes=vmem)


def _dot(a, b):
    return jnp.dot(a, b, preferred_element_type=F32)


def _dot_nt(a, b):
    return lax.dot_general(a, b, (((1,), (1,)), ((), ())), preferred_element_type=F32)


def _alibi_slopes(n_heads):
    return [float(np.float32(2.0 ** (-8.0 * (h + 1) / n_heads))) for h in range(n_heads)]


def _mm_kernel(x_ref, w_ref, o_ref):
    o_ref[...] = _dot(x_ref[...], w_ref[0]).astype(o_ref.dtype)


def _mm_nt_kernel(x_ref, wt_ref, o_ref):
    o_ref[...] = _dot_nt(x_ref[...], wt_ref[0]).astype(o_ref.dtype)


def _matmul(x, w, layer, out_dtype, transposed=False, tm=1024, tn=1024):
    m, k = x.shape
    n = w.shape[1] if transposed else w.shape[2]
    tm, tn = min(tm, m), min(tn, n)
    assert m % tm == 0 and n % tn == 0, (m, n, tm, tn)
    if transposed:
        body, w_spec = _mm_nt_kernel, pl.BlockSpec((1, tn, k), lambda i, j: (layer, j, 0))
    else:
        body, w_spec = _mm_kernel, pl.BlockSpec((1, k, tn), lambda i, j: (layer, 0, j))
    return pl.pallas_call(
        body,
        grid=(m // tm, n // tn),
        in_specs=[pl.BlockSpec((tm, k), lambda i, j: (i, 0)), w_spec],
        out_specs=pl.BlockSpec((tm, tn), lambda i, j: (i, j)),
        out_shape=jax.ShapeDtypeStruct((m, n), out_dtype),
        compiler_params=_params(("parallel", "parallel")),
        name="dense_matmul",
    )(x, w)


def _cast_kernel(x_ref, o_ref):
    o_ref[...] = x_ref[...].astype(o_ref.dtype)


def _cast_bf16(x, tm=512):
    t, d = x.shape
    tm = min(tm, t)
    row = pl.BlockSpec((tm, d), lambda i: (i, 0))
    return pl.pallas_call(
        _cast_kernel, grid=(t // tm,), in_specs=[row], out_specs=row,
        out_shape=jax.ShapeDtypeStruct((t, d), BF16),
        compiler_params=_params(("parallel",)),
        name="cast_bf16",
    )(x)


def _merge_kernel(x_ref, wg_ref, b0_ref, b1_ref, b2_ref, b3_ref, wb_ref, o_ref, acc_ref, *, sub):
    n = pl.program_id(2)

    @pl.when((pl.program_id(0) == 0) & (pl.program_id(1) == 0) & (n == 0))
    def _():
        acc_ref[...] = jnp.zeros(acc_ref.shape, F32)

    branch = jnp.where(n == 0, b0_ref[...], jnp.where(n == 1, b1_ref[...],
                                                      jnp.where(n == 2, b2_ref[...], b3_ref[...])))
    x = x_ref[...]
    for c in range(o_ref.shape[1] // sub):
        cols = slice(c * sub, (c + 1) * sub)
        gate = jax.nn.sigmoid(_dot(x, wg_ref[0, 0, :, cols]))
        contrib = gate * _dot(branch, wb_ref[0, 0, :, cols])
        acc = jnp.where(n == 0, 0.0, acc_ref[:, cols]) + contrib
        acc_ref[:, cols] = acc
        o_ref[:, cols] = acc.astype(o_ref.dtype)


def _gated_merge(xb, wg, branches, wb, layer, tm=1024, tn=1024, sub=512):
    t, d = xb.shape
    tm, tn = min(tm, t), min(tn, d)
    sub = min(sub, tn)
    bw = branches[0].shape[1]
    once = pl.Buffered(1)
    bspec = pl.BlockSpec((tm, bw), lambda i, j, n: (i, 0), pipeline_mode=once)
    return pl.pallas_call(
        functools.partial(_merge_kernel, sub=sub),
        grid=(t // tm, d // tn, 4),
        in_specs=[pl.BlockSpec((tm, d), lambda i, j, n: (i, 0), pipeline_mode=once),
                  pl.BlockSpec((1, 1, d, tn), lambda i, j, n: (layer, n, 0, j)),
                  bspec, bspec, bspec, bspec,
                  pl.BlockSpec((1, 1, bw, tn), lambda i, j, n: (layer, n, 0, j))],
        out_specs=pl.BlockSpec((tm, tn), lambda i, j, n: (i, j)),
        out_shape=jax.ShapeDtypeStruct((t, d), BF16),
        scratch_shapes=[pltpu.VMEM((tm, tn), F32)],
        compiler_params=_params(("arbitrary", "arbitrary", "arbitrary")),
        name="gated_merge",
    )(xb, wg, *branches, wb)


def _ln_kernel(x_ref, y_ref, g_ref, b_ref, o_ref, *maybe_ob_ref, alpha):
    r = alpha * x_ref[...] + y_ref[...]
    mu = jnp.mean(r, axis=-1, keepdims=True)
    c = r - mu
    var = jnp.mean(c * c, axis=-1, keepdims=True)
    out = c * lax.rsqrt(var + LN_EPS) * g_ref[...] + b_ref[...]
    o_ref[...] = out
    for ob_ref in maybe_ob_ref:
        ob_ref[...] = out.astype(BF16)


def _residual_layernorm(x, y, gain, bias, alpha, with_bf16, tm=256):
    t, d = x.shape
    tm = min(tm, t)
    row = pl.BlockSpec((tm, d), lambda i: (i, 0))
    vec = pl.BlockSpec((1, d), lambda i: (0, 0))
    out_shape = [jax.ShapeDtypeStruct((t, d), F32)] + ([jax.ShapeDtypeStruct((t, d), BF16)] if with_bf16 else [])
    outs = pl.pallas_call(
        functools.partial(_ln_kernel, alpha=alpha),
        grid=(t // tm,),
        in_specs=[row, row, vec, vec],
        out_specs=[row] * len(out_shape),
        out_shape=out_shape,
        compiler_params=_params(("parallel",)),
        name="residual_layernorm",
    )(x, y, gain.reshape(1, d), bias.reshape(1, d))
    return (outs[0], outs[1]) if with_bf16 else (outs[0], None)


def _forget_cumsum_kernel(f_ref, bias_ref, c_ref):
    rows, s = f_ref.shape
    lane = lax.broadcasted_iota(jnp.int32, (rows, LANES), 1)
    carry = jnp.zeros((rows, 1), F32)
    for c in range(s // LANES):
        x = jax.nn.log_sigmoid(f_ref[:, c * LANES:(c + 1) * LANES] + bias_ref[...])
        shift = 1
        while shift < LANES:
            x = x + jnp.where(lane >= shift, pltpu.roll(x, shift, 1), 0.0)
            shift *= 2
        x = x + carry
        c_ref[:, c * LANES:(c + 1) * LANES] = x * LOG2E
        carry = x[:, LANES - 1:LANES]


def _forget_cumsum(f_rows, bias_rows):
    rows, s = f_rows.shape
    full = pl.BlockSpec((rows, s), lambda: (0, 0))
    return pl.pallas_call(
        _forget_cumsum_kernel,
        in_specs=[full, pl.BlockSpec((rows, 1), lambda: (0, 0))],
        out_specs=full,
        out_shape=jax.ShapeDtypeStruct((rows, s), F32),
        name="forget_cumsum",
    )(f_rows, bias_rows)


def _tree(parts, op, ways=8):
    parts = list(parts)
    if len(parts) > ways:
        accs = parts[:ways]
        for a in range(ways, len(parts)):
            accs[a % ways] = op(accs[a % ways], parts[a])
        parts = accs
    while len(parts) > 1:
        nxt = [op(parts[a], parts[a + 1]) for a in range(0, len(parts) - 1, 2)]
        if len(parts) % 2:
            nxt.append(parts[-1])
        parts = nxt
    return parts[0]


def _col_reduce(x, op, reduce_fn):
    rows = x.shape[0]
    part = _tree([x[r:r + SUBLANES, :] for r in range(0, rows, SUBLANES)], op)
    return reduce_fn(part, axis=0, keepdims=True)


def _flash_kernel(i_tab, j_tab, *refs, fox, heads, dqk, dv, tile, slab):
    if fox:
        q_ref, k_ref, vt_ref, ck_ref, z_ref, o_ref, m_sc, l_sc, acc_sc = refs
    else:
        q_ref, k_ref, vt_ref, z_ref, o_ref, m_sc, l_sc, acc_sc = refs
    pair = pl.program_id(1)
    i = i_tab[pair]
    j = j_tab[pair]

    @pl.when(j == 0)
    def _():
        m_sc[...] = jnp.full(m_sc.shape, NEG, F32)
        l_sc[...] = jnp.zeros(l_sc.shape, F32)
        acc_sc[...] = jnp.zeros(acc_sc.shape, F32)

    def step(masked):
        if masked:
            causal = (lax.broadcasted_iota(jnp.int32, (tile, tile), 0)
                      <= lax.broadcasted_iota(jnp.int32, (tile, tile), 1))
        def logits(h):
            return _dot_nt(k_ref[:, h * dqk:(h + 1) * dqk], q_ref[:, h * dqk:(h + 1) * dqk])

        s_next = logits(0)
        for h in range(heads):
            s = s_next
            if h + 1 < heads:
                s_next = logits(h + 1)
            if fox:
                s = s - ck_ref[:, h:h + 1]
            if masked:
                s = jnp.where(causal, s, NEG)
            m_prev = m_sc[h:h + 1, :]
            m_new = jnp.maximum(m_prev, _col_reduce(s, jnp.maximum, jnp.max))
            alpha = jnp.exp2(m_prev - m_new)
            psum, pv = None, None
            for k0 in range(0, tile, slab):
                p = jnp.exp2(s[k0:k0 + slab, :] - m_new)
                part = _tree([p[r:r + SUBLANES, :] for r in range(0, slab, SUBLANES)], jnp.add)
                psum = part if psum is None else psum + part
                contrib = _dot(vt_ref[h * dv:(h + 1) * dv, k0:k0 + slab], p.astype(BF16))
                pv = contrib if pv is None else pv + contrib
            l_sc[h:h + 1, :] = alpha * l_sc[h:h + 1, :] + jnp.sum(psum, axis=0, keepdims=True)
            acc_sc[h * dv:(h + 1) * dv, :] = alpha * acc_sc[h * dv:(h + 1) * dv, :] + pv
            m_sc[h:h + 1, :] = m_new

    @pl.when(j < i)
    def _():
        step(False)

    @pl.when(j == i)
    def _():
        step(True)
        for h in range(heads):
            out_t = acc_sc[h * dv:(h + 1) * dv, :] / l_sc[h:h + 1, :]
            z = z_ref[:, h * dv:(h + 1) * dv].astype(F32)
            o_ref[:, h * dv:(h + 1) * dv] = (out_t.T * jax.nn.silu(z)).astype(o_ref.dtype)


def _flash_attention(q_arr, q_blk, k_arr, k_blk, vt_arr, z_arr, z_blk, *, batch, seq, heads,
                     dqk, dv, cum=None, tile=512):
    tile = min(tile, seq)
    assert seq % tile == 0, (seq, tile)
    nq = seq // tile
    fox = cum is not None
    pairs = [(i, j) for i in range(nq) for j in range(i + 1)]
    i_tab = jnp.asarray([p[0] for p in pairs], jnp.int32)
    j_tab = jnp.asarray([p[1] for p in pairs], jnp.int32)
    in_specs = [
        pl.BlockSpec((tile, heads * dqk), lambda b, p, it, jt: (b * nq + it[p], q_blk)),
        pl.BlockSpec((tile, heads * dqk), lambda b, p, it, jt: (b * nq + jt[p], k_blk)),
        pl.BlockSpec((heads * dv, tile), lambda b, p, it, jt: (0, b * nq + jt[p])),
    ]
    args = [q_arr, k_arr, vt_arr]
    if fox:
        in_specs.append(pl.BlockSpec((tile, LANES), lambda b, p, it, jt: (b * nq + jt[p], 0)))
        args.append(cum)
    in_specs.append(pl.BlockSpec((tile, heads * dv), lambda b, p, it, jt: (b * nq + it[p], z_blk)))
    args.append(z_arr)
    return pl.pallas_call(
        functools.partial(_flash_kernel, fox=fox, heads=heads, dqk=dqk, dv=dv, tile=tile,
                          slab=min(2 * LANES, tile)),
        grid_spec=pltpu.PrefetchScalarGridSpec(
            num_scalar_prefetch=2,
            grid=(batch, len(pairs)),
            in_specs=in_specs,
            out_specs=pl.BlockSpec((tile, heads * dv), lambda b, p, it, jt: (b * nq + it[p], 0)),
            scratch_shapes=[pltpu.VMEM((heads, tile), F32), pltpu.VMEM((heads, tile), F32),
                            pltpu.VMEM((heads * dv, tile), F32)]),
        out_shape=jax.ShapeDtypeStruct((batch * seq, heads * dv), BF16),
        compiler_params=_params(("parallel", "arbitrary")),
        name="flash_fox" if fox else "flash_latent",
    )(i_tab, j_tab, *args)


def _swa_kernel(q_ref, kvc_ref, kvp_ref, z_ref, sink_ref, o_ref, *, slopes):
    i = pl.program_id(1)
    w = WINDOW
    pairs_per_group = C_HEADS // C_KV_HEADS // 2
    kv = jnp.concatenate([kvp_ref[...], kvc_ref[...]], axis=0).astype(F32)
    lane = lax.broadcasted_iota(jnp.int32, (2 * w, LANES), 1)
    low = lane < C_DIM

    def halves(t):
        g0_lo = jnp.where(low, t, 0.0)
        g1_hi = jnp.where(low, 0.0, t)
        return ((g0_lo, pltpu.roll(g0_lo, C_DIM, 1)), (pltpu.roll(g1_hi, C_DIM, 1), g1_hi))

    k_half = halves(kv[:, :LANES])
    v_half = halves(kv[:, LANES:])
    key = lax.broadcasted_iota(jnp.int32, (2 * w, w), 0)
    qry = lax.broadcasted_iota(jnp.int32, (2 * w, w), 1)
    dist = qry + w - key
    first_key = jnp.where(i > 0, 0, w)
    valid = (dist >= 0) & (dist < w) & (key >= first_key)
    dist_m = jnp.where(valid, dist.astype(F32), -NEG)

    q_rows = [jnp.concatenate([q_ref[:, (g * pairs_per_group + jj) * LANES:(g * pairs_per_group + jj + 1) * LANES]
                               for jj in range(pairs_per_group)], axis=0) for g in range(C_KV_HEADS)]
    logits = [[_dot_nt(k_half[g][half].astype(BF16), q_rows[g]) for half in range(2)] for g in range(C_KV_HEADS)]
    for g in range(C_KV_HEADS):
        out_t = None
        for half in range(2):
            probs, inv = [], []
            for jj in range(pairs_per_group):
                h = 2 * (g * pairs_per_group + jj) + half
                s = logits[g][half][:, jj * w:(jj + 1) * w] - (slopes[h] * LOG2E) * dist_m
                sink = jnp.full((1, w), sink_ref[h], F32) * LOG2E
                m = jnp.maximum(_col_reduce(s, jnp.maximum, jnp.max), sink)
                p = jnp.exp2(s - m)
                inv.append(1.0 / (_col_reduce(p, jnp.add, jnp.sum) + jnp.exp2(sink - m)))
                probs.append(p.astype(BF16))
            vt = v_half[g][half].T.astype(BF16)
            contrib = _dot(vt, jnp.concatenate(probs, axis=1)) * jnp.concatenate(inv, axis=1)
            out_t = contrib if out_t is None else out_t + contrib
        for jj in range(pairs_per_group):
            cols = slice((g * pairs_per_group + jj) * LANES, (g * pairs_per_group + jj + 1) * LANES)
            z = z_ref[:, cols].astype(F32)
            o_ref[:, cols] = (out_t[:, jj * w:(jj + 1) * w].T * jax.nn.silu(z)).astype(o_ref.dtype)


def _sliding_window(p_arr, sinks, batch, seq):
    w = WINDOW
    nb = seq // w
    kv_blk = P_OFF['c_k'] // (2 * LANES)
    return pl.pallas_call(
        functools.partial(_swa_kernel, slopes=_alibi_slopes(C_HEADS)),
        grid=(batch, nb),
        in_specs=[pl.BlockSpec((w, BRANCH_WIDTH), lambda b, i: (b * nb + i, P_OFF['c_q'] // BRANCH_WIDTH)),
                  pl.BlockSpec((w, 2 * LANES), lambda b, i: (b * nb + i, kv_blk)),
                  pl.BlockSpec((w, 2 * LANES), lambda b, i: (b * nb + jnp.maximum(i - 1, 0), kv_blk)),
                  pl.BlockSpec((w, BRANCH_WIDTH), lambda b, i: (b * nb + i, P_OFF['c_z'] // BRANCH_WIDTH)),
                  pl.BlockSpec(memory_space=pltpu.SMEM)],
        out_specs=pl.BlockSpec((w, BRANCH_WIDTH), lambda b, i: (b * nb + i, 0)),
        out_shape=jax.ShapeDtypeStruct((batch * seq, BRANCH_WIDTH), BF16),
        compiler_params=_params(("parallel", "parallel")),
        name="sliding_window",
    )(p_arr, p_arr, p_arr, p_arr, sinks.astype(F32))


def _latent_prep_kernel(cq_ref, ckv_ref, kr_ref, tab_ref, qg_ref, kvg_ref, qup_ref, kup_ref, vupt_ref,
                        q_ref, k_ref, vt_ref):
    def rms(x, g):
        return x * lax.rsqrt(jnp.mean(x * x, axis=-1, keepdims=True) + RMS_EPS) * g

    tab = tab_ref[...]
    lane = lax.broadcasted_iota(jnp.int32, tab.shape, 1)

    def rotate(t):
        r = t * tab
        return r + pltpu.roll(r, D_ROPE, 1)

    q = _dot(rms(cq_ref[...].astype(F32), qg_ref[...]).astype(BF16), qup_ref[...])
    ckv = rms(ckv_ref[...].astype(F32), kvg_ref[...]).astype(BF16)
    k_nope = _dot(ckv, kup_ref[...])
    vt_ref[...] = _dot_nt(vupt_ref[...], ckv).astype(vt_ref.dtype)
    k_rot = jnp.where(lane < D_ROPE, rotate(kr_ref[...].astype(F32)), 0.0).astype(k_ref.dtype)
    hw = 2 * LANES
    for h in range(D_HEADS):
        q_ref[:, h * hw:h * hw + LANES] = q[:, h * hw:h * hw + LANES].astype(q_ref.dtype)
        q_ref[:, h * hw + LANES:(h + 1) * hw] = rotate(q[:, h * hw + LANES:(h + 1) * hw]).astype(q_ref.dtype)
        k_ref[:, h * hw:h * hw + LANES] = k_nope[:, h * LANES:(h + 1) * LANES].astype(k_ref.dtype)
        k_ref[:, h * hw + LANES:(h + 1) * hw] = k_rot


def _latent_prep(p_arr, tab, q_gain, kv_gain, q_up, k_up, v_up_t, tm=512):
    t = p_arr.shape[0]
    tm = min(tm, t)
    hw = 2 * LANES

    def const(shape):
        return pl.BlockSpec(shape, lambda i: (0, 0))

    return pl.pallas_call(
        _latent_prep_kernel,
        grid=(t // tm,),
        in_specs=[pl.BlockSpec((tm, D_Q_LORA), lambda i: (i, P_OFF['d_cq'] // D_Q_LORA)),
                  pl.BlockSpec((tm, D_KV_LORA), lambda i: (i, P_OFF['d_ckv'] // D_KV_LORA)),
                  pl.BlockSpec((tm, LANES), lambda i: (i, P_OFF['d_kr'] // LANES)),
                  pl.BlockSpec((tm, LANES), lambda i: (i, 0)),
                  const((1, D_Q_LORA)), const((1, D_KV_LORA)),
                  const((D_Q_LORA, D_HEADS * hw)), const((D_KV_LORA, D_HEADS * LANES)),
                  const((D_HEADS * D_VDIM, D_KV_LORA))],
        out_specs=[pl.BlockSpec((tm, D_HEADS * hw), lambda i: (i, 0)),
                   pl.BlockSpec((tm, D_HEADS * hw), lambda i: (i, 0)),
                   pl.BlockSpec((D_HEADS * D_VDIM, tm), lambda i: (0, i))],
        out_shape=[jax.ShapeDtypeStruct((t, D_HEADS * hw), BF16),
                   jax.ShapeDtypeStruct((t, D_HEADS * hw), BF16),
                   jax.ShapeDtypeStruct((D_HEADS * D_VDIM, t), BF16)],
        compiler_params=_params(("parallel",)),
        name="latent_prep",
    )(p_arr, p_arr, p_arr, tab, q_gain.reshape(1, -1), kv_gain.reshape(1, -1), q_up, k_up, v_up_t)


def _sparse_kernel(q_ref, iq_ref, z_ref, ik_ref, k_ref, vt_ref, wt_ref, iqn_ref, wtn_ref, o_ref,
                   keys_ref, jcut_ref, m_ref, l_ref, acc_ref, *, topk, ck, slopes, idx_bits, nq):
    i = pl.program_id(1)
    tq = q_ref.shape[0]
    nch = ((i + 1) * tq + ck - 1) // ck
    nch_next = ((i + 2) * tq + ck - 1) // ck
    slot = lax.rem(i, 2)
    slab = min(2 * LANES, ck)
    lane_q = lax.broadcasted_iota(jnp.int32, (ck, tq), 1)
    q_pos = i * tq + lane_q
    s_iota = lax.broadcasted_iota(jnp.int32, (ck, tq), 0)
    lane = lax.broadcasted_iota(jnp.int32, (ck, LANES), 1)
    low = lane < IDX_DIM
    w_scale = IDX_HEADS ** -0.5 * IDX_DIM ** -0.5

    def score_chunk(c, iq_r, w_all, first_q, dst_slot):
        start = pl.multiple_of(c * ck, ck)
        ikc = ik_ref[pl.ds(start, ck), :]
        ik_lo = jnp.where(low, ikc, jnp.zeros_like(ikc))
        ik_hi = jnp.where(low, jnp.zeros_like(ikc), ikc)
        acc = jnp.zeros((ck, tq), F32)
        for pair in range(IDX_HEADS // 2):
            iqp = iq_r[:, pair * LANES:(pair + 1) * LANES]
            acc = acc + w_all[2 * pair:2 * pair + 1, :] * jnp.maximum(_dot_nt(ik_lo, iqp), 0.0)
            acc = acc + w_all[2 * pair + 1:2 * pair + 2, :] * jnp.maximum(_dot_nt(ik_hi, iqp), 0.0)
        acc = jnp.where(acc == 0.0, 0.0, acc)
        bits = lax.bitcast_convert_type(acc, jnp.int32)
        key = bits ^ ((bits >> 31) & 0x7FFFFFFF)
        key = jnp.where(start + s_iota <= first_q + lane_q, key, INT_MIN)
        keys_ref[dst_slot, pl.ds(start, ck), :] = key

    @pl.when(i == 0)
    def _():
        w_all = wt_ref[0] * w_scale

        def body(c, carry):
            score_chunk(c, iq_ref, w_all, 0, 0)
            return carry
        lax.fori_loop(0, nch, body, 0)

    def count(pred_fn):
        def body(c, cnt):
            start = pl.multiple_of(c * ck, ck)
            hit = pred_fn(keys_ref[slot, pl.ds(start, ck), :], start + s_iota)
            ways = 8
            accs = [cnt] + [jnp.zeros((SUBLANES, tq), F32)] * (ways - 1)
            for a, r in enumerate(range(0, ck, SUBLANES)):
                accs[a % ways] = jnp.where(hit[r:r + SUBLANES, :], accs[a % ways] + 1.0, accs[a % ways])
            return _tree(accs, jnp.add)
        part = lax.fori_loop(0, nch, body, jnp.zeros((SUBLANES, tq), F32))
        return jnp.sum(part, axis=0, keepdims=True)

    kf = float(topk)

    def thr_step(it, state):
        thr, n_ge = state
        cand = thr ^ (jnp.int32(1) << (31 - it))
        cnt = count(lambda kc, pos: kc >= cand)
        ok = cnt >= kf
        return jnp.where(ok, cand, thr), jnp.where(ok, cnt, n_ge)

    total = jnp.full((1, tq), 1.0, F32) * (nch * ck).astype(F32)
    thr, n_ge = lax.fori_loop(0, 32, thr_step, (jnp.full((1, tq), INT_MIN, jnp.int32), total))
    tied = (n_ge > kf) & (thr != INT_MIN)
    jcut_ref[...] = jnp.full(jcut_ref.shape, 2 ** 30, jnp.int32)

    @pl.when(jnp.max(tied.astype(F32)) > 0.0)
    def _():
        need = kf - count(lambda kc, pos: kc > thr)

        def cut_step(it, x):
            cand = x + (jnp.int32(1) << (idx_bits - 1 - it))
            cnt = count(lambda kc, pos: (kc == thr) & (pos < cand))
            return jnp.where(cnt < need, cand, x)
        x = lax.fori_loop(0, idx_bits, cut_step, jnp.zeros((1, tq), jnp.int32))
        jcut_ref[...] = jnp.broadcast_to(jnp.where(tied, x, 2 ** 30), jcut_ref.shape)

    jcut = jcut_ref[0:1, :]

    m_ref[...] = jnp.full(m_ref.shape, NEG, F32)
    l_ref[...] = jnp.zeros(l_ref.shape, F32)
    acc_ref[...] = jnp.zeros(acc_ref.shape, F32)
    rep = A_HEADS // A_KV_HEADS
    q_groups = [jnp.concatenate([q_ref[:, (g * rep + r) * A_DIM:(g * rep + r + 1) * A_DIM] for r in range(rep)],
                                axis=0) for g in range(A_KV_HEADS)]

    def attend_chunk(c, score_next):
        start = pl.multiple_of(c * ck, ck)
        kc = keys_ref[slot, pl.ds(start, ck), :]
        pos = start + s_iota
        sel = ((kc > thr) | ((kc == thr) & (pos <= jcut))) & (pos <= q_pos)
        dist = jnp.where(sel, (q_pos - pos).astype(F32), FAR)
        logits_g = [_dot_nt(k_ref[pl.ds(start, ck), g * A_DIM:(g + 1) * A_DIM], q_groups[g])
                    for g in range(A_KV_HEADS)]
        if score_next is not None:
            score_next(c)
        for g in range(A_KV_HEADS):
            vtg = vt_ref[0, c, g * A_DIM:(g + 1) * A_DIM, :]
            for r in range(rep):
                h = g * rep + r
                s = logits_g[g][:, r * tq:(r + 1) * tq] - (slopes[h] * LOG2E) * dist
                m_prev = m_ref[h:h + 1, :]
                m_new = jnp.maximum(m_prev, _col_reduce(s, jnp.maximum, jnp.max))
                alpha = jnp.exp2(m_prev - m_new)
                psum, pv = None, None
                for k0 in range(0, ck, slab):
                    p = jnp.exp2(s[k0:k0 + slab, :] - m_new)
                    part = _tree([p[r0:r0 + SUBLANES, :] for r0 in range(0, slab, SUBLANES)], jnp.add)
                    psum = part if psum is None else psum + part
                    contrib = _dot(vtg[:, k0:k0 + slab], p.astype(BF16))
                    pv = contrib if pv is None else pv + contrib
                l_ref[h:h + 1, :] = alpha * l_ref[h:h + 1, :] + jnp.sum(psum, axis=0, keepdims=True)
                acc_ref[h * A_DIM:(h + 1) * A_DIM, :] = alpha * acc_ref[h * A_DIM:(h + 1) * A_DIM, :] + pv
                m_ref[h:h + 1, :] = m_new

    @pl.when(i + 1 < nq)
    def _():
        w_next = wtn_ref[0] * w_scale
        score_next = lambda c: score_chunk(c, iqn_ref, w_next, (i + 1) * tq, 1 - slot)

        def both(c, carry):
            attend_chunk(c, score_next)
            return carry
        lax.fori_loop(0, nch, both, 0)

        def rest(c, carry):
            score_next(c)
            return carry
        lax.fori_loop(nch, nch_next, rest, 0)

    @pl.when(i + 1 == nq)
    def _():
        def only(c, carry):
            attend_chunk(c, None)
            return carry
        lax.fori_loop(0, nch, only, 0)

    for h in range(A_HEADS):
        out_t = acc_ref[h * A_DIM:(h + 1) * A_DIM, :] / l_ref[h:h + 1, :]
        z = z_ref[:, h * A_DIM:(h + 1) * A_DIM].astype(F32)
        o_ref[:, h * A_DIM:(h + 1) * A_DIM] = (out_t.T * jax.nn.silu(z)).astype(o_ref.dtype)


def _key_chunk(seq):
    return min(512, seq)


def _sparse_attention(p_arr, vt, wt, batch, seq):
    tq = min(2 * LANES, seq)
    nq = seq // tq
    ck = _key_chunk(seq)
    topk = min(IDX_TOPK_MAX, seq // 4)
    idx_bits = int(seq).bit_length()
    slopes = _alibi_slopes(A_HEADS)
    assert seq % ck == 0 and ck % tq == 0 and min(slopes) * LOG2E * FAR > -NEG
    return pl.pallas_call(
        functools.partial(_sparse_kernel, topk=topk, ck=ck, slopes=slopes, idx_bits=idx_bits, nq=nq),
        grid=(batch, nq),
        in_specs=[pl.BlockSpec((tq, BRANCH_WIDTH), lambda b, i: (b * nq + i, P_OFF['a_q'] // BRANCH_WIDTH)),
                  pl.BlockSpec((tq, BRANCH_WIDTH), lambda b, i: (b * nq + i, P_OFF['a_iq'] // BRANCH_WIDTH)),
                  pl.BlockSpec((tq, BRANCH_WIDTH), lambda b, i: (b * nq + i, P_OFF['a_z'] // BRANCH_WIDTH)),
                  pl.BlockSpec((seq, LANES), lambda b, i: (b, P_OFF['a_ik'] // LANES)),
                  pl.BlockSpec((seq, 2 * LANES), lambda b, i: (b, P_OFF['a_k'] // (2 * LANES))),
                  pl.BlockSpec((1, seq // ck, 2 * LANES, ck), lambda b, i: (b, 0, 0, 0)),
                  pl.BlockSpec((1, IDX_HEADS, tq), lambda b, i: (b, 0, i)),
                  pl.BlockSpec((tq, BRANCH_WIDTH),
                               lambda b, i: (b * nq + jnp.minimum(i + 1, nq - 1), P_OFF['a_iq'] // BRANCH_WIDTH)),
                  pl.BlockSpec((1, IDX_HEADS, tq), lambda b, i: (b, 0, jnp.minimum(i + 1, nq - 1)))],
        out_specs=pl.BlockSpec((tq, BRANCH_WIDTH), lambda b, i: (b * nq + i, 0)),
        out_shape=jax.ShapeDtypeStruct((batch * seq, BRANCH_WIDTH), BF16),
        scratch_shapes=[pltpu.VMEM((2, seq, tq), jnp.int32), pltpu.VMEM((8, tq), jnp.int32),
                        pltpu.VMEM((A_HEADS, tq), F32), pltpu.VMEM((A_HEADS, tq), F32),
                        pltpu.VMEM((A_HEADS * A_DIM, tq), F32)],
        compiler_params=_params(("parallel", "arbitrary")),
        name="sparse_attention",
    )(p_arr, p_arr, p_arr, p_arr, p_arr, vt, wt, p_arr, wt)


def _swap_halves(w):
    half = w.shape[-1] // 2
    return jnp.concatenate([-w[..., half:], w[..., :half]], axis=-1)


def _prepare_in_proj(w_in):
    sizes = [n for _, n in IN_SEGMENTS]
    offs = np.concatenate([[0], np.cumsum(sizes)])
    span = {name: (int(offs[k]), int(offs[k + 1])) for k, (name, _) in enumerate(IN_SEGMENTS)}
    wt32 = jnp.swapaxes(w_in, 1, 2)
    wt16 = wt32.astype(BF16)
    seg16 = lambda name: wt16[:, span[name][0]:span[name][1], :]
    seg32 = lambda name: wt32[:, span[name][0]:span[name][1], :]
    rows = []
    for name, _ in P_LAYOUT:
        if name == 'a_ik':
            rows += [seg16('a_ik'), seg16('a_ik')]
        elif name == 'd_kr':
            kr = seg16('d_kr')
            rows += [kr, -kr[:, D_ROPE // 2:, :], kr[:, :D_ROPE // 2, :]]
        elif name == 'a_q':
            rows.append((seg32('a_q') * (A_DIM ** -0.5 * LOG2E)).astype(BF16))
        elif name == 'b_q':
            rows.append((seg32('b_q') * (B_DIM ** -0.5 * LOG2E)).astype(BF16))
        elif name == 'c_q':
            rows.append((seg32('c_q') * (C_DIM ** -0.5 * LOG2E)).astype(BF16))
        else:
            rows.append(seg16(name))
    w_main_t = jnp.concatenate(rows, axis=1)
    pad = jnp.zeros((w_in.shape[0], LANES - IDX_HEADS - B_HEADS, w_in.shape[1]), BF16)
    w_small_t = jnp.concatenate([seg16('a_iw'), seg16('b_f'), pad], axis=1)
    return w_main_t, w_small_t


def _prepare_latent(dq_up, dkv_up):
    q = dq_up.reshape(D_Q_LORA, D_HEADS, D_NOPE + D_ROPE) * ((D_NOPE + D_ROPE) ** -0.5 * LOG2E)
    rope = q[..., D_NOPE:]
    q_up = jnp.concatenate([q[..., :D_NOPE], rope, _swap_halves(rope)], axis=-1)
    kv = dkv_up.reshape(D_KV_LORA, D_HEADS, D_NOPE + D_VDIM)
    return (q_up.reshape(D_Q_LORA, -1).astype(BF16),
            kv[..., :D_NOPE].reshape(D_KV_LORA, -1).astype(BF16),
            kv[..., D_NOPE:].reshape(D_KV_LORA, -1).T.astype(BF16))


def _rope_table(positions):
    half = D_ROPE // 2
    inv_freq = ROPE_THETA ** (-jnp.arange(half, dtype=F32) / half)
    ang = positions.astype(F32)[..., None] * inv_freq
    cos, sin = jnp.cos(ang), jnp.sin(ang)
    tab = jnp.concatenate([cos, cos, sin, sin], axis=-1)
    return tab.reshape(-1, 4 * half)


def _layer(layer, x, xb, tab, batch, seq, w_main, w_small, w_gate, w_branch, w_out, dq_gain, dq_up, dkv_gain,
           dkv_up, f_bias, sinks, ln_gain, ln_bias, alpha, last):
    proj = _matmul(xb, w_main, layer, BF16, transposed=True)
    small = _matmul(xb, w_small, layer, F32, transposed=True, tn=LANES)

    ck = _key_chunk(seq)
    wt = small[:, :IDX_HEADS].reshape(batch, seq, IDX_HEADS).transpose(0, 2, 1)
    a_v = proj[:, P_OFF['a_v']:P_OFF['a_v'] + 2 * LANES]
    vt = a_v.reshape(batch, seq // ck, ck, 2 * LANES).transpose(0, 1, 3, 2)
    o_a = _sparse_attention(proj, vt, wt, batch, seq)

    f_rows = small[:, IDX_HEADS:IDX_HEADS + B_HEADS].reshape(batch, seq, B_HEADS).transpose(0, 2, 1)
    cum = _forget_cumsum(f_rows.reshape(batch * B_HEADS, seq),
                         jnp.tile(f_bias.astype(F32), batch).reshape(batch * B_HEADS, 1))
    cum = cum.reshape(batch, B_HEADS, seq).transpose(0, 2, 1).reshape(batch * seq, B_HEADS)
    cum = jnp.pad(cum, ((0, 0), (0, LANES - B_HEADS)))
    vt_b = proj[:, P_OFF['b_v']:P_OFF['b_v'] + BRANCH_WIDTH].T
    o_b = _flash_attention(proj, P_OFF['b_q'] // BRANCH_WIDTH, proj, P_OFF['b_k'] // BRANCH_WIDTH, vt_b,
                           proj, P_OFF['b_z'] // BRANCH_WIDTH, batch=batch, seq=seq, heads=B_HEADS,
                           dqk=B_DIM, dv=B_DIM, cum=cum)

    o_c = _sliding_window(proj, sinks, batch, seq)

    q_up, k_up, v_up_t = _prepare_latent(dq_up, dkv_up)
    q_d, k_d, vt_d = _latent_prep(proj, tab, dq_gain, dkv_gain, q_up, k_up, v_up_t)
    o_d = _flash_attention(q_d, 0, k_d, 0, vt_d, proj, P_OFF['d_z'] // BRANCH_WIDTH, batch=batch, seq=seq,
                           heads=D_HEADS, dqk=2 * LANES, dv=D_VDIM)

    merged = _gated_merge(xb, w_gate, (o_a, o_b, o_c, o_d), w_branch, layer)
    y = _matmul(merged, w_out, layer, F32)
    return _residual_layernorm(x, y, ln_gain, ln_bias, alpha, with_bf16=not last)


def kernel(x, positions, w_in, w_gate, w_branch, w_out, dq_gain, dq_up, dkv_gain, dkv_up, f_bias, sinks,
           ln_gain, ln_bias):
    batch, seq, d = x.shape
    depth = w_in.shape[0]
    alpha = (2 * depth) ** 0.25
    tab = _rope_table(positions)
    xf = x.reshape(batch * seq, d)
    xb = _cast_bf16(xf)
    w_main, w_small = _prepare_in_proj(w_in)
    w_gate, w_branch, w_out = w_gate.astype(BF16), w_branch.astype(BF16), w_out.astype(BF16)
    for l in range(depth):
        xf, xb = _layer(l, xf, xb, tab, batch, seq, w_main, w_small, w_gate, w_branch, w_out, dq_gain[l],
                        dq_up[l], dkv_gain[l], dkv_up[l], f_bias[l], sinks[l], ln_gain[l], ln_bias[l], alpha,
                        last=(l == depth - 1))
    return xf.reshape(batch, seq, d)
```

```python
import functools

import numpy as np
import jax
import jax.numpy as jnp
from jax import lax
from jax.experimental import pallas as pl
from jax.experimental.pallas import tpu as pltpu

BRANCH_WIDTH = 1024
A_HEADS, A_KV_HEADS, A_DIM = 8, 2, 128
IDX_HEADS, IDX_DIM, IDX_TOPK_MAX = 16, 64, 256
B_HEADS, B_DIM = 8, 128
C_HEADS, C_KV_HEADS, C_DIM, WINDOW = 16, 2, 64, 128
D_HEADS, D_Q_LORA, D_KV_LORA, D_NOPE, D_ROPE, D_VDIM = 8, 768, 256, 128, 64, 128
ROPE_THETA = 10000.0
RMS_EPS = 1e-6
LN_EPS = 1e-5

IN_SEGMENTS = (
    ('a_q', 1024), ('a_k', 256), ('a_v', 256), ('a_iq', 1024), ('a_ik', 64), ('a_iw', 16), ('a_z', 1024),
    ('b_q', 1024), ('b_k', 1024), ('b_v', 1024), ('b_f', 8), ('b_z', 1024),
    ('c_q', 1024), ('c_k', 128), ('c_v', 128), ('c_z', 1024),
    ('d_cq', 768), ('d_ckv', 256), ('d_kr', 64), ('d_z', 1024),
)

P_LAYOUT = (
    ('d_cq', 768), ('d_ckv', 256), ('a_q', 1024), ('a_iq', 1024), ('a_z', 1024),
    ('b_q', 1024), ('b_k', 1024), ('b_v', 1024), ('b_z', 1024), ('c_q', 1024), ('c_z', 1024),
    ('d_z', 1024), ('a_k', 256), ('a_v', 256), ('a_ik', 128), ('d_kr', 128), ('c_k', 128), ('c_v', 128),
)
P_OFF = {}
_o = 0
for _n, _w in P_LAYOUT:
    assert _o % _w == 0
    P_OFF[_n] = _o
    _o += _w
P_WIDTH = _o

LANES = 128
SUBLANES = 8
NEG = -1e30
FAR = 1e34
LOG2E = 1.4426950408889634
INT_MIN = -2 ** 31
VMEM_LIMIT = 56 * 1024 * 1024

F32 = jnp.float32
BF16 = jnp.bfloat16


def _params(sem, vmem=VMEM_LIMIT):
    return pltpu.CompilerParams(dimension_semantics=sem, vmem_limit_bytes=vmem)


def _dot(a, b):
    return jnp.dot(a, b, preferred_element_type=F32)


def _dot_nt(a, b):
    return lax.dot_general(a, b, (((1,), (1,)), ((), ())), preferred_element_type=F32)


def _alibi_slopes(n_heads):
    return [float(np.float32(2.0 ** (-8.0 * (h + 1) / n_heads))) for h in range(n_heads)]


def _mm_kernel(x_ref, w_ref, o_ref):
    o_ref[...] = _dot(x_ref[...], w_ref[0]).astype(o_ref.dtype)


def _mm_nt_kernel(x_ref, wt_ref, o_ref):
    o_ref[...] = _dot_nt(x_ref[...], wt_ref[0]).astype(o_ref.dtype)


def _matmul(x, w, layer, out_dtype, transposed=False, tm=1024, tn=1024):
    m, k = x.shape
    n = w.shape[1] if transposed else w.shape[2]
    tm, tn = min(tm, m), min(tn, n)
    assert m % tm == 0 and n % tn == 0, (m, n, tm, tn)
    if transposed:
        body, w_spec = _mm_nt_kernel, pl.BlockSpec((1, tn, k), lambda i, j: (layer, j, 0))
    else:
        body, w_spec = _mm_kernel, pl.BlockSpec((1, k, tn), lambda i, j: (layer, 0, j))
    return pl.pallas_call(
        body,
        grid=(m // tm, n // tn),
        in_specs=[pl.BlockSpec((tm, k), lambda i, j: (i, 0)), w_spec],
        out_specs=pl.BlockSpec((tm, tn), lambda i, j: (i, j)),
        out_shape=jax.ShapeDtypeStruct((m, n), out_dtype),
        compiler_params=_params(("parallel", "parallel")),
        name="dense_matmul",
    )(x, w)


def _cast_kernel(x_ref, o_ref):
    o_ref[...] = x_ref[...].astype(o_ref.dtype)


def _cast_bf16(x, tm=512):
    t, d = x.shape
    tm = min(tm, t)
    row = pl.BlockSpec((tm, d), lambda i: (i, 0))
    return pl.pallas_call(
        _cast_kernel, grid=(t // tm,), in_specs=[row], out_specs=row,
        out_shape=jax.ShapeDtypeStruct((t, d), BF16),
        compiler_params=_params(("parallel",)),
        name="cast_bf16",
    )(x)


def _merge_kernel(x_ref, wg_ref, b0_ref, b1_ref, b2_ref, b3_ref, wb_ref, o_ref, acc_ref, *, sub):
    n = pl.program_id(2)

    @pl.when((pl.program_id(0) == 0) & (pl.program_id(1) == 0) & (n == 0))
    def _():
        acc_ref[...] = jnp.zeros(acc_ref.shape, F32)

    branch = jnp.where(n == 0, b0_ref[...], jnp.where(n == 1, b1_ref[...],
                                                      jnp.where(n == 2, b2_ref[...], b3_ref[...])))
    x = x_ref[...]
    for c in range(o_ref.shape[1] // sub):
        cols = slice(c * sub, (c + 1) * sub)
        gate = jax.nn.sigmoid(_dot(x, wg_ref[0, 0, :, cols]))
        contrib = gate * _dot(branch, wb_ref[0, 0, :, cols])
        acc = jnp.where(n == 0, 0.0, acc_ref[:, cols]) + contrib
        acc_ref[:, cols] = acc
        o_ref[:, cols] = acc.astype(o_ref.dtype)


def _gated_merge(xb, wg, branches, wb, layer, tm=512, tn=1024, sub=512):
    t, d = xb.shape
    tm, tn = min(tm, t), min(tn, d)
    sub = min(sub, tn)
    bw = branches[0].shape[1]
    bspec = pl.BlockSpec((tm, bw), lambda i, j, n: (i, 0))
    return pl.pallas_call(
        functools.partial(_merge_kernel, sub=sub),
        grid=(t // tm, d // tn, 4),
        in_specs=[pl.BlockSpec((tm, d), lambda i, j, n: (i, 0)),
                  pl.BlockSpec((1, 1, d, tn), lambda i, j, n: (layer, n, 0, j)),
                  bspec, bspec, bspec, bspec,
                  pl.BlockSpec((1, 1, bw, tn), lambda i, j, n: (layer, n, 0, j))],
        out_specs=pl.BlockSpec((tm, tn), lambda i, j, n: (i, j)),
        out_shape=jax.ShapeDtypeStruct((t, d), BF16),
        scratch_shapes=[pltpu.VMEM((tm, tn), F32)],
        compiler_params=_params(("arbitrary", "arbitrary", "arbitrary")),
        name="gated_merge",
    )(xb, wg, *branches, wb)


def _ln_kernel(x_ref, y_ref, g_ref, b_ref, o_ref, *maybe_ob_ref, alpha):
    r = alpha * x_ref[...] + y_ref[...]
    mu = jnp.mean(r, axis=-1, keepdims=True)
    c = r - mu
    var = jnp.mean(c * c, axis=-1, keepdims=True)
    out = c * lax.rsqrt(var + LN_EPS) * g_ref[...] + b_ref[...]
    o_ref[...] = out
    for ob_ref in maybe_ob_ref:
        ob_ref[...] = out.astype(BF16)


def _residual_layernorm(x, y, gain, bias, alpha, with_bf16, tm=256):
    t, d = x.shape
    tm = min(tm, t)
    row = pl.BlockSpec((tm, d), lambda i: (i, 0))
    vec = pl.BlockSpec((1, d), lambda i: (0, 0))
    out_shape = [jax.ShapeDtypeStruct((t, d), F32)] + ([jax.ShapeDtypeStruct((t, d), BF16)] if with_bf16 else [])
    outs = pl.pallas_call(
        functools.partial(_ln_kernel, alpha=alpha),
        grid=(t // tm,),
        in_specs=[row, row, vec, vec],
        out_specs=[row] * len(out_shape),
        out_shape=out_shape,
        compiler_params=_params(("parallel",)),
        name="residual_layernorm",
    )(x, y, gain.reshape(1, d), bias.reshape(1, d))
    return (outs[0], outs[1]) if with_bf16 else (outs[0], None)


def _forget_cumsum_kernel(f_ref, bias_ref, c_ref):
    rows, s = f_ref.shape
    lane = lax.broadcasted_iota(jnp.int32, (rows, LANES), 1)
    carry = jnp.zeros((rows, 1), F32)
    for c in range(s // LANES):
        x = jax.nn.log_sigmoid(f_ref[:, c * LANES:(c + 1) * LANES] + bias_ref[...])
        shift = 1
        while shift < LANES:
            x = x + jnp.where(lane >= shift, pltpu.roll(x, shift, 1), 0.0)
            shift *= 2
        x = x + carry
        c_ref[:, c * LANES:(c + 1) * LANES] = x * LOG2E
        carry = x[:, LANES - 1:LANES]


def _forget_cumsum(f_rows, bias_rows):
    rows, s = f_rows.shape
    full = pl.BlockSpec((rows, s), lambda: (0, 0))
    return pl.pallas_call(
        _forget_cumsum_kernel,
        in_specs=[full, pl.BlockSpec((rows, 1), lambda: (0, 0))],
        out_specs=full,
        out_shape=jax.ShapeDtypeStruct((rows, s), F32),
        name="forget_cumsum",
    )(f_rows, bias_rows)


def _tree(parts, op, ways=8):
    parts = list(parts)
    if len(parts) > ways:
        accs = parts[:ways]
        for a in range(ways, len(parts)):
            accs[a % ways] = op(accs[a % ways], parts[a])
        parts = accs
    while len(parts) > 1:
        nxt = [op(parts[a], parts[a + 1]) for a in range(0, len(parts) - 1, 2)]
        if len(parts) % 2:
            nxt.append(parts[-1])
        parts = nxt
    return parts[0]


def _col_reduce(x, op, reduce_fn):
    rows = x.shape[0]
    part = _tree([x[r:r + SUBLANES, :] for r in range(0, rows, SUBLANES)], op)
    return reduce_fn(part, axis=0, keepdims=True)


def _flash_kernel(i_tab, j_tab, *refs, fox, heads, dqk, dv, tile, slab):
    if fox:
        q_ref, k_ref, vt_ref, ck_ref, z_ref, o_ref, m_sc, l_sc, acc_sc = refs
    else:
        q_ref, k_ref, vt_ref, z_ref, o_ref, m_sc, l_sc, acc_sc = refs
    pair = pl.program_id(1)
    i = i_tab[pair]
    j = j_tab[pair]

    @pl.when(j == 0)
    def _():
        m_sc[...] = jnp.full(m_sc.shape, NEG, F32)
        l_sc[...] = jnp.zeros(l_sc.shape, F32)
        acc_sc[...] = jnp.zeros(acc_sc.shape, F32)

    def step(masked):
        if masked:
            causal = (lax.broadcasted_iota(jnp.int32, (tile, tile), 0)
                      <= lax.broadcasted_iota(jnp.int32, (tile, tile), 1))
        def logits(h):
            return _dot_nt(k_ref[:, h * dqk:(h + 1) * dqk], q_ref[:, h * dqk:(h + 1) * dqk])

        s_next = logits(0)
        for h in range(heads):
            s = s_next
            if h + 1 < heads:
                s_next = logits(h + 1)
            if fox:
                s = s - ck_ref[:, h:h + 1]
            if masked:
                s = jnp.where(causal, s, NEG)
            m_prev = m_sc[h:h + 1, :]
            m_new = jnp.maximum(m_prev, _col_reduce(s, jnp.maximum, jnp.max))
            alpha = jnp.exp2(m_prev - m_new)
            psum, pv = None, None
            for k0 in range(0, tile, slab):
                p = jnp.exp2(s[k0:k0 + slab, :] - m_new)
                part = _tree([p[r:r + SUBLANES, :] for r in range(0, slab, SUBLANES)], jnp.add)
                psum = part if psum is None else psum + part
                contrib = _dot(vt_ref[h * dv:(h + 1) * dv, k0:k0 + slab], p.astype(BF16))
                pv = contrib if pv is None else pv + contrib
            l_sc[h:h + 1, :] = alpha * l_sc[h:h + 1, :] + jnp.sum(psum, axis=0, keepdims=True)
            acc_sc[h * dv:(h + 1) * dv, :] = alpha * acc_sc[h * dv:(h + 1) * dv, :] + pv
            m_sc[h:h + 1, :] = m_new

    @pl.when(j < i)
    def _():
        step(False)

    @pl.when(j == i)
    def _():
        step(True)
        for h in range(heads):
            out_t = acc_sc[h * dv:(h + 1) * dv, :] / l_sc[h:h + 1, :]
            z = z_ref[:, h * dv:(h + 1) * dv].astype(F32)
            o_ref[:, h * dv:(h + 1) * dv] = (out_t.T * jax.nn.silu(z)).astype(o_ref.dtype)


def _flash_attention(q_arr, q_blk, k_arr, k_blk, vt_arr, z_arr, z_blk, *, batch, seq, heads,
                     dqk, dv, cum=None, tile=512):
    tile = min(tile, seq)
    assert seq % tile == 0, (seq, tile)
    nq = seq // tile
    fox = cum is not None
    pairs = [(i, j) for i in range(nq) for j in range(i + 1)]
    i_tab = jnp.asarray([p[0] for p in pairs], jnp.int32)
    j_tab = jnp.asarray([p[1] for p in pairs], jnp.int32)
    in_specs = [
        pl.BlockSpec((tile, heads * dqk), lambda b, p, it, jt: (b * nq + it[p], q_blk)),
        pl.BlockSpec((tile, heads * dqk), lambda b, p, it, jt: (b * nq + jt[p], k_blk)),
        pl.BlockSpec((heads * dv, tile), lambda b, p, it, jt: (0, b * nq + jt[p])),
    ]
    args = [q_arr, k_arr, vt_arr]
    if fox:
        in_specs.append(pl.BlockSpec((tile, LANES), lambda b, p, it, jt: (b * nq + jt[p], 0)))
        args.append(cum)
    in_specs.append(pl.BlockSpec((tile, heads * dv), lambda b, p, it, jt: (b * nq + it[p], z_blk)))
    args.append(z_arr)
    return pl.pallas_call(
        functools.partial(_flash_kernel, fox=fox, heads=heads, dqk=dqk, dv=dv, tile=tile,
                          slab=min(2 * LANES, tile)),
        grid_spec=pltpu.PrefetchScalarGridSpec(
            num_scalar_prefetch=2,
            grid=(batch, len(pairs)),
            in_specs=in_specs,
            out_specs=pl.BlockSpec((tile, heads * dv), lambda b, p, it, jt: (b * nq + it[p], 0)),
            scratch_shapes=[pltpu.VMEM((heads, tile), F32), pltpu.VMEM((heads, tile), F32),
                            pltpu.VMEM((heads * dv, tile), F32)]),
        out_shape=jax.ShapeDtypeStruct((batch * seq, heads * dv), BF16),
        compiler_params=_params(("parallel", "arbitrary")),
        name="flash_fox" if fox else "flash_latent",
    )(i_tab, j_tab, *args)


def _swa_kernel(q_ref, kvc_ref, kvp_ref, z_ref, sink_ref, o_ref, *, slopes):
    i = pl.program_id(1)
    w = WINDOW
    pairs_per_group = C_HEADS // C_KV_HEADS // 2
    kv = jnp.concatenate([kvp_ref[...], kvc_ref[...]], axis=0).astype(F32)
    lane = lax.broadcasted_iota(jnp.int32, (2 * w, LANES), 1)
    low = lane < C_DIM

    def halves(t):
        g0_lo = jnp.where(low, t, 0.0)
        g1_hi = jnp.where(low, 0.0, t)
        return ((g0_lo, pltpu.roll(g0_lo, C_DIM, 1)), (pltpu.roll(g1_hi, C_DIM, 1), g1_hi))

    k_half = halves(kv[:, :LANES])
    v_half = halves(kv[:, LANES:])
    key = lax.broadcasted_iota(jnp.int32, (2 * w, w), 0)
    qry = lax.broadcasted_iota(jnp.int32, (2 * w, w), 1)
    dist = qry + w - key
    first_key = jnp.where(i > 0, 0, w)
    valid = (dist >= 0) & (dist < w) & (key >= first_key)
    dist_m = jnp.where(valid, dist.astype(F32), -NEG)

    q_rows = [jnp.concatenate([q_ref[:, (g * pairs_per_group + jj) * LANES:(g * pairs_per_group + jj + 1) * LANES]
                               for jj in range(pairs_per_group)], axis=0) for g in range(C_KV_HEADS)]
    logits = [[_dot_nt(k_half[g][half].astype(BF16), q_rows[g]) for half in range(2)] for g in range(C_KV_HEADS)]
    for g in range(C_KV_HEADS):
        out_t = None
        for half in range(2):
            probs, inv = [], []
            for jj in range(pairs_per_group):
                h = 2 * (g * pairs_per_group + jj) + half
                s = logits[g][half][:, jj * w:(jj + 1) * w] - (slopes[h] * LOG2E) * dist_m
                sink = jnp.full((1, w), sink_ref[h], F32) * LOG2E
                m = jnp.maximum(_col_reduce(s, jnp.maximum, jnp.max), sink)
                p = jnp.exp2(s - m)
                inv.append(1.0 / (_col_reduce(p, jnp.add, jnp.sum) + jnp.exp2(sink - m)))
                probs.append(p.astype(BF16))
            vt = v_half[g][half].T.astype(BF16)
            contrib = _dot(vt, jnp.concatenate(probs, axis=1)) * jnp.concatenate(inv, axis=1)
            out_t = contrib if out_t is None else out_t + contrib
        for jj in range(pairs_per_group):
            cols = slice((g * pairs_per_group + jj) * LANES, (g * pairs_per_group + jj + 1) * LANES)
            z = z_ref[:, cols].astype(F32)
            o_ref[:, cols] = (out_t[:, jj * w:(jj + 1) * w].T * jax.nn.silu(z)).astype(o_ref.dtype)


def _sliding_window(p_arr, sinks, batch, seq):
    w = WINDOW
    nb = seq // w
    kv_blk = P_OFF['c_k'] // (2 * LANES)
    return pl.pallas_call(
        functools.partial(_swa_kernel, slopes=_alibi_slopes(C_HEADS)),
        grid=(batch, nb),
        in_specs=[pl.BlockSpec((w, BRANCH_WIDTH), lambda b, i: (b * nb + i, P_OFF['c_q'] // BRANCH_WIDTH)),
                  pl.BlockSpec((w, 2 * LANES), lambda b, i: (b * nb + i, kv_blk)),
                  pl.BlockSpec((w, 2 * LANES), lambda b, i: (b * nb + jnp.maximum(i - 1, 0), kv_blk)),
                  pl.BlockSpec((w, BRANCH_WIDTH), lambda b, i: (b * nb + i, P_OFF['c_z'] // BRANCH_WIDTH)),
                  pl.BlockSpec(memory_space=pltpu.SMEM)],
        out_specs=pl.BlockSpec((w, BRANCH_WIDTH), lambda b, i: (b * nb + i, 0)),
        out_shape=jax.ShapeDtypeStruct((batch * seq, BRANCH_WIDTH), BF16),
        compiler_params=_params(("parallel", "parallel")),
        name="sliding_window",
    )(p_arr, p_arr, p_arr, p_arr, sinks.astype(F32))


def _latent_prep_kernel(cq_ref, ckv_ref, kr_ref, tab_ref, qg_ref, kvg_ref, qup_ref, kup_ref, vupt_ref,
                        q_ref, k_ref, vt_ref):
    def rms(x, g):
        return x * lax.rsqrt(jnp.mean(x * x, axis=-1, keepdims=True) + RMS_EPS) * g

    tab = tab_ref[...]
    lane = lax.broadcasted_iota(jnp.int32, tab.shape, 1)

    def rotate(t):
        r = t * tab
        return r + pltpu.roll(r, D_ROPE, 1)

    q = _dot(rms(cq_ref[...].astype(F32), qg_ref[...]).astype(BF16), qup_ref[...])
    ckv = rms(ckv_ref[...].astype(F32), kvg_ref[...]).astype(BF16)
    k_nope = _dot(ckv, kup_ref[...])
    vt_ref[...] = _dot_nt(vupt_ref[...], ckv).astype(vt_ref.dtype)
    k_rot = jnp.where(lane < D_ROPE, rotate(kr_ref[...].astype(F32)), 0.0).astype(k_ref.dtype)
    hw = 2 * LANES
    for h in range(D_HEADS):
        q_ref[:, h * hw:h * hw + LANES] = q[:, h * hw:h * hw + LANES].astype(q_ref.dtype)
        q_ref[:, h * hw + LANES:(h + 1) * hw] = rotate(q[:, h * hw + LANES:(h + 1) * hw]).astype(q_ref.dtype)
        k_ref[:, h * hw:h * hw + LANES] = k_nope[:, h * LANES:(h + 1) * LANES].astype(k_ref.dtype)
        k_ref[:, h * hw + LANES:(h + 1) * hw] = k_rot


def _latent_prep(p_arr, tab, q_gain, kv_gain, q_up, k_up, v_up_t, tm=512):
    t = p_arr.shape[0]
    tm = min(tm, t)
    hw = 2 * LANES

    def const(shape):
        return pl.BlockSpec(shape, lambda i: (0, 0))

    return pl.pallas_call(
        _latent_prep_kernel,
        grid=(t // tm,),
        in_specs=[pl.BlockSpec((tm, D_Q_LORA), lambda i: (i, P_OFF['d_cq'] // D_Q_LORA)),
                  pl.BlockSpec((tm, D_KV_LORA), lambda i: (i, P_OFF['d_ckv'] // D_KV_LORA)),
                  pl.BlockSpec((tm, LANES), lambda i: (i, P_OFF['d_kr'] // LANES)),
                  pl.BlockSpec((tm, LANES), lambda i: (i, 0)),
                  const((1, D_Q_LORA)), const((1, D_KV_LORA)),
                  const((D_Q_LORA, D_HEADS * hw)), const((D_KV_LORA, D_HEADS * LANES)),
                  const((D_HEADS * D_VDIM, D_KV_LORA))],
        out_specs=[pl.BlockSpec((tm, D_HEADS * hw), lambda i: (i, 0)),
                   pl.BlockSpec((tm, D_HEADS * hw), lambda i: (i, 0)),
                   pl.BlockSpec((D_HEADS * D_VDIM, tm), lambda i: (0, i))],
        out_shape=[jax.ShapeDtypeStruct((t, D_HEADS * hw), BF16),
                   jax.ShapeDtypeStruct((t, D_HEADS * hw), BF16),
                   jax.ShapeDtypeStruct((D_HEADS * D_VDIM, t), BF16)],
        compiler_params=_params(("parallel",)),
        name="latent_prep",
    )(p_arr, p_arr, p_arr, tab, q_gain.reshape(1, -1), kv_gain.reshape(1, -1), q_up, k_up, v_up_t)


def _sparse_kernel(q_ref, iq_ref, z_ref, ik_ref, k_ref, vt_ref, wt_ref, iqn_ref, wtn_ref, o_ref,
                   keys_ref, jcut_ref, m_ref, l_ref, acc_ref, *, topk, ck, slopes, idx_bits, nq):
    i = pl.program_id(1)
    tq = q_ref.shape[0]
    nch = ((i + 1) * tq + ck - 1) // ck
    nch_next = ((i + 2) * tq + ck - 1) // ck
    slot = lax.rem(i, 2)
    slab = min(2 * LANES, ck)
    lane_q = lax.broadcasted_iota(jnp.int32, (ck, tq), 1)
    q_pos = i * tq + lane_q
    s_iota = lax.broadcasted_iota(jnp.int32, (ck, tq), 0)
    lane = lax.broadcasted_iota(jnp.int32, (ck, LANES), 1)
    low = lane < IDX_DIM
    w_scale = IDX_HEADS ** -0.5 * IDX_DIM ** -0.5

    def score_chunk(c, iq_r, w_all, first_q, dst_slot):
        start = pl.multiple_of(c * ck, ck)
        ikc = ik_ref[pl.ds(start, ck), :]
        ik_lo = jnp.where(low, ikc, jnp.zeros_like(ikc))
        ik_hi = jnp.where(low, jnp.zeros_like(ikc), ikc)
        acc = jnp.zeros((ck, tq), F32)
        for pair in range(IDX_HEADS // 2):
            iqp = iq_r[:, pair * LANES:(pair + 1) * LANES]
            acc = acc + w_all[2 * pair:2 * pair + 1, :] * jnp.maximum(_dot_nt(ik_lo, iqp), 0.0)
            acc = acc + w_all[2 * pair + 1:2 * pair + 2, :] * jnp.maximum(_dot_nt(ik_hi, iqp), 0.0)
        acc = jnp.where(acc == 0.0, 0.0, acc)
        bits = lax.bitcast_convert_type(acc, jnp.int32)
        key = bits ^ ((bits >> 31) & 0x7FFFFFFF)
        key = jnp.where(start + s_iota <= first_q + lane_q, key, INT_MIN)
        keys_ref[dst_slot, pl.ds(start, ck), :] = key

    @pl.when(i == 0)
    def _():
        w_all = wt_ref[0] * w_scale

        def body(c, carry):
            score_chunk(c, iq_ref, w_all, 0, 0)
            return carry
        lax.fori_loop(0, nch, body, 0)

    def count(pred_fn):
        def body(c, cnt):
            start = pl.multiple_of(c * ck, ck)
            hit = pred_fn(keys_ref[slot, pl.ds(start, ck), :], start + s_iota)
            ways = 8
            accs = [cnt] + [jnp.zeros((SUBLANES, tq), F32)] * (ways - 1)
            for a, r in enumerate(range(0, ck, SUBLANES)):
                accs[a % ways] = jnp.where(hit[r:r + SUBLANES, :], accs[a % ways] + 1.0, accs[a % ways])
            return _tree(accs, jnp.add)
        part = lax.fori_loop(0, nch, body, jnp.zeros((SUBLANES, tq), F32))
        return jnp.sum(part, axis=0, keepdims=True)

    kf = float(topk)

    def thr_step(it, state):
        thr, n_ge = state
        cand = thr ^ (jnp.int32(1) << (31 - it))
        cnt = count(lambda kc, pos: kc >= cand)
        ok = cnt >= kf
        return jnp.where(ok, cand, thr), jnp.where(ok, cnt, n_ge)

    total = jnp.full((1, tq), 1.0, F32) * (nch * ck).astype(F32)
    thr, n_ge = lax.fori_loop(0, 32, thr_step, (jnp.full((1, tq), INT_MIN, jnp.int32), total))
    tied = (n_ge > kf) & (thr != INT_MIN)
    jcut_ref[...] = jnp.full(jcut_ref.shape, 2 ** 30, jnp.int32)

    @pl.when(jnp.max(tied.astype(F32)) > 0.0)
    def _():
        need = kf - count(lambda kc, pos: kc > thr)

        def cut_step(it, x):
            cand = x + (jnp.int32(1) << (idx_bits - 1 - it))
            cnt = count(lambda kc, pos: (kc == thr) & (pos < cand))
            return jnp.where(cnt < need, cand, x)
        x = lax.fori_loop(0, idx_bits, cut_step, jnp.zeros((1, tq), jnp.int32))
        jcut_ref[...] = jnp.broadcast_to(jnp.where(tied, x, 2 ** 30), jcut_ref.shape)

    jcut = jcut_ref[0:1, :]

    m_ref[...] = jnp.full(m_ref.shape, NEG, F32)
    l_ref[...] = jnp.zeros(l_ref.shape, F32)
    acc_ref[...] = jnp.zeros(acc_ref.shape, F32)
    rep = A_HEADS // A_KV_HEADS
    def q_group(g):
        return jnp.concatenate([q_ref[:, (g * rep + r) * A_DIM:(g * rep + r + 1) * A_DIM] for r in range(rep)],
                               axis=0)

    def attend_chunk(c, score_next):
        start = pl.multiple_of(c * ck, ck)
        kc = keys_ref[slot, pl.ds(start, ck), :]
        pos = start + s_iota
        sel = ((kc > thr) | ((kc == thr) & (pos <= jcut))) & (pos <= q_pos)
        dist = jnp.where(sel, (q_pos - pos).astype(F32), FAR)
        logits_g = [_dot_nt(k_ref[pl.ds(start, ck), g * A_DIM:(g + 1) * A_DIM], q_group(g))
                    for g in range(A_KV_HEADS)]
        if score_next is not None:
            score_next(c)
        for g in range(A_KV_HEADS):
            vtg = vt_ref[0, c, g * A_DIM:(g + 1) * A_DIM, :]
            for r in range(rep):
                h = g * rep + r
                s = logits_g[g][:, r * tq:(r + 1) * tq] - (slopes[h] * LOG2E) * dist
                m_prev = m_ref[h:h + 1, :]
                m_new = jnp.maximum(m_prev, _col_reduce(s, jnp.maximum, jnp.max))
                alpha = jnp.exp2(m_prev - m_new)
                psum, pv = None, None
                for k0 in range(0, ck, slab):
                    p = jnp.exp2(s[k0:k0 + slab, :] - m_new)
                    part = _tree([p[r0:r0 + SUBLANES, :] for r0 in range(0, slab, SUBLANES)], jnp.add)
                    psum = part if psum is None else psum + part
                    contrib = _dot(vtg[:, k0:k0 + slab], p.astype(BF16))
                    pv = contrib if pv is None else pv + contrib
                l_ref[h:h + 1, :] = alpha * l_ref[h:h + 1, :] + jnp.sum(psum, axis=0, keepdims=True)
                acc_ref[h * A_DIM:(h + 1) * A_DIM, :] = alpha * acc_ref[h * A_DIM:(h + 1) * A_DIM, :] + pv
                m_ref[h:h + 1, :] = m_new

    @pl.when(i + 1 < nq)
    def _():
        w_next = wtn_ref[0] * w_scale
        score_next = lambda c: score_chunk(c, iqn_ref, w_next, (i + 1) * tq, 1 - slot)

        def both(c, carry):
            attend_chunk(c, score_next)
            return carry
        lax.fori_loop(0, nch, both, 0)

        def rest(c, carry):
            score_next(c)
            return carry
        lax.fori_loop(nch, nch_next, rest, 0)

    @pl.when(i + 1 == nq)
    def _():
        def only(c, carry):
            attend_chunk(c, None)
            return carry
        lax.fori_loop(0, nch, only, 0)

    for h in range(A_HEADS):
        out_t = acc_ref[h * A_DIM:(h + 1) * A_DIM, :] / l_ref[h:h + 1, :]
        z = z_ref[:, h * A_DIM:(h + 1) * A_DIM].astype(F32)
        o_ref[:, h * A_DIM:(h + 1) * A_DIM] = (out_t.T * jax.nn.silu(z)).astype(o_ref.dtype)


def _key_chunk(seq):
    return min(512, seq)


def _sparse_attention(p_arr, vt, wt, batch, seq):
    tq = min(2 * LANES, seq)
    nq = seq // tq
    ck = _key_chunk(seq)
    topk = min(IDX_TOPK_MAX, seq // 4)
    idx_bits = int(seq).bit_length()
    slopes = _alibi_slopes(A_HEADS)
    assert seq % ck == 0 and ck % tq == 0 and min(slopes) * LOG2E * FAR > -NEG
    return pl.pallas_call(
        functools.partial(_sparse_kernel, topk=topk, ck=ck, slopes=slopes, idx_bits=idx_bits, nq=nq),
        grid=(batch, nq),
        in_specs=[pl.BlockSpec((tq, BRANCH_WIDTH), lambda b, i: (b * nq + i, P_OFF['a_q'] // BRANCH_WIDTH)),
                  pl.BlockSpec((tq, BRANCH_WIDTH), lambda b, i: (b * nq + i, P_OFF['a_iq'] // BRANCH_WIDTH)),
                  pl.BlockSpec((tq, BRANCH_WIDTH), lambda b, i: (b * nq + i, P_OFF['a_z'] // BRANCH_WIDTH)),
                  pl.BlockSpec((seq, LANES), lambda b, i: (b, P_OFF['a_ik'] // LANES)),
                  pl.BlockSpec((seq, 2 * LANES), lambda b, i: (b, P_OFF['a_k'] // (2 * LANES))),
                  pl.BlockSpec((1, seq // ck, 2 * LANES, ck), lambda b, i: (b, 0, 0, 0)),
                  pl.BlockSpec((1, IDX_HEADS, tq), lambda b, i: (b, 0, i)),
                  pl.BlockSpec((tq, BRANCH_WIDTH),
                               lambda b, i: (b * nq + jnp.minimum(i + 1, nq - 1), P_OFF['a_iq'] // BRANCH_WIDTH)),
                  pl.BlockSpec((1, IDX_HEADS, tq), lambda b, i: (b, 0, jnp.minimum(i + 1, nq - 1)))],
        out_specs=pl.BlockSpec((tq, BRANCH_WIDTH), lambda b, i: (b * nq + i, 0)),
        out_shape=jax.ShapeDtypeStruct((batch * seq, BRANCH_WIDTH), BF16),
        scratch_shapes=[pltpu.VMEM((2, seq, tq), jnp.int32), pltpu.VMEM((8, tq), jnp.int32),
                        pltpu.VMEM((A_HEADS, tq), F32), pltpu.VMEM((A_HEADS, tq), F32),
                        pltpu.VMEM((A_HEADS * A_DIM, tq), F32)],
        compiler_params=_params(("parallel", "arbitrary")),
        name="sparse_attention",
    )(p_arr, p_arr, p_arr, p_arr, p_arr, vt, wt, p_arr, wt)


def _swap_halves(w):
    half = w.shape[-1] // 2
    return jnp.concatenate([-w[..., half:], w[..., :half]], axis=-1)


def _prepare_in_proj(w_in):
    sizes = [n for _, n in IN_SEGMENTS]
    offs = np.concatenate([[0], np.cumsum(sizes)])
    span = {name: (int(offs[k]), int(offs[k + 1])) for k, (name, _) in enumerate(IN_SEGMENTS)}
    wt32 = jnp.swapaxes(w_in, 1, 2)
    wt16 = wt32.astype(BF16)
    seg16 = lambda name: wt16[:, span[name][0]:span[name][1], :]
    seg32 = lambda name: wt32[:, span[name][0]:span[name][1], :]
    rows = []
    for name, _ in P_LAYOUT:
        if name == 'a_ik':
            rows += [seg16('a_ik'), seg16('a_ik')]
        elif name == 'd_kr':
            kr = seg16('d_kr')
            rows += [kr, -kr[:, D_ROPE // 2:, :], kr[:, :D_ROPE // 2, :]]
        elif name == 'a_q':
            rows.append((seg32('a_q') * (A_DIM ** -0.5 * LOG2E)).astype(BF16))
        elif name == 'b_q':
            rows.append((seg32('b_q') * (B_DIM ** -0.5 * LOG2E)).astype(BF16))
        elif name == 'c_q':
            rows.append((seg32('c_q') * (C_DIM ** -0.5 * LOG2E)).astype(BF16))
        else:
            rows.append(seg16(name))
    w_main_t = jnp.concatenate(rows, axis=1)
    pad = jnp.zeros((w_in.shape[0], LANES - IDX_HEADS - B_HEADS, w_in.shape[1]), BF16)
    w_small_t = jnp.concatenate([seg16('a_iw'), seg16('b_f'), pad], axis=1)
    return w_main_t, w_small_t


def _prepare_latent(dq_up, dkv_up):
    q = dq_up.reshape(D_Q_LORA, D_HEADS, D_NOPE + D_ROPE) * ((D_NOPE + D_ROPE) ** -0.5 * LOG2E)
    rope = q[..., D_NOPE:]
    q_up = jnp.concatenate([q[..., :D_NOPE], rope, _swap_halves(rope)], axis=-1)
    kv = dkv_up.reshape(D_KV_LORA, D_HEADS, D_NOPE + D_VDIM)
    return (q_up.reshape(D_Q_LORA, -1).astype(BF16),
            kv[..., :D_NOPE].reshape(D_KV_LORA, -1).astype(BF16),
            kv[..., D_NOPE:].reshape(D_KV_LORA, -1).T.astype(BF16))


def _rope_table(positions):
    half = D_ROPE // 2
    inv_freq = ROPE_THETA ** (-jnp.arange(half, dtype=F32) / half)
    ang = positions.astype(F32)[..., None] * inv_freq
    cos, sin = jnp.cos(ang), jnp.sin(ang)
    tab = jnp.concatenate([cos, cos, sin, sin], axis=-1)
    return tab.reshape(-1, 4 * half)


def _layer(layer, x, xb, tab, batch, seq, w_main, w_small, w_gate, w_branch, w_out, dq_gain, dq_up, dkv_gain,
           dkv_up, f_bias, sinks, ln_gain, ln_bias, alpha, last):
    proj = _matmul(xb, w_main, layer, BF16, transposed=True)
    small = _matmul(xb, w_small, layer, F32, transposed=True, tn=LANES)

    ck = _key_chunk(seq)
    wt = small[:, :IDX_HEADS].reshape(batch, seq, IDX_HEADS).transpose(0, 2, 1)
    a_v = proj[:, P_OFF['a_v']:P_OFF['a_v'] + 2 * LANES]
    vt = a_v.reshape(batch, seq // ck, ck, 2 * LANES).transpose(0, 1, 3, 2)
    o_a = _sparse_attention(proj, vt, wt, batch, seq)

    f_rows = small[:, IDX_HEADS:IDX_HEADS + B_HEADS].reshape(batch, seq, B_HEADS).transpose(0, 2, 1)
    cum = _forget_cumsum(f_rows.reshape(batch * B_HEADS, seq),
                         jnp.tile(f_bias.astype(F32), batch).reshape(batch * B_HEADS, 1))
    cum = cum.reshape(batch, B_HEADS, seq).transpose(0, 2, 1).reshape(batch * seq, B_HEADS)
    cum = jnp.pad(cum, ((0, 0), (0, LANES - B_HEADS)))
    vt_b = proj[:, P_OFF['b_v']:P_OFF['b_v'] + BRANCH_WIDTH].T
    o_b = _flash_attention(proj, P_OFF['b_q'] // BRANCH_WIDTH, proj, P_OFF['b_k'] // BRANCH_WIDTH, vt_b,
                           proj, P_OFF['b_z'] // BRANCH_WIDTH, batch=batch, seq=seq, heads=B_HEADS,
                           dqk=B_DIM, dv=B_DIM, cum=cum)

    o_c = _sliding_window(proj, sinks, batch, seq)

    q_up, k_up, v_up_t = _prepare_latent(dq_up, dkv_up)
    q_d, k_d, vt_d = _latent_prep(proj, tab, dq_gain, dkv_gain, q_up, k_up, v_up_t)
    o_d = _flash_attention(q_d, 0, k_d, 0, vt_d, proj, P_OFF['d_z'] // BRANCH_WIDTH, batch=batch, seq=seq,
                           heads=D_HEADS, dqk=2 * LANES, dv=D_VDIM)

    merged = _gated_merge(xb, w_gate, (o_a, o_b, o_c, o_d), w_branch, layer)
    y = _matmul(merged, w_out, layer, F32)
    return _residual_layernorm(x, y, ln_gain, ln_bias, alpha, with_bf16=not last)


def kernel(x, positions, w_in, w_gate, w_branch, w_out, dq_gain, dq_up, dkv_gain, dkv_up, f_bias, sinks,
           ln_gain, ln_bias):
    batch, seq, d = x.shape
    depth = w_in.shape[0]
    alpha = (2 * depth) ** 0.25
    tab = _rope_table(positions)
    xf = x.reshape(batch * seq, d)
    xb = _cast_bf16(xf)
    w_main, w_small = _prepare_in_proj(w_in)
    w_gate, w_branch, w_out = w_gate.astype(BF16), w_branch.astype(BF16), w_out.astype(BF16)
    for l in range(depth):
        xf, xb = _layer(l, xf, xb, tab, batch, seq, w_main, w_small, w_gate, w_branch, w_out, dq_gain[l],
                        dq_up[l], dkv_gain[l], dkv_up[l], f_bias[l], sinks[l], ln_gain[l], ln_bias[l], alpha,
                        last=(l == depth - 1))
    return xf.reshape(batch, seq, d)
```
